```python
import math
import jax, jax.numpy as jnp
from jax import lax
import numpy as np

D_MODEL = 1024
BATCH = 8
SEQ = 2048
DEPTH = 1
DEC_BATCH = 128
DEC_SEQ = 8
PAST_LEN = 2048
PAGE_SIZE = 128

ATT_W = D_MODEL // 2
HEAD_DIM = 64
N_HEADS = ATT_W // HEAD_DIM
MOBA_BLOCK = 256
MOBA_TOPK = 3
Q_CHUNK = 64
SSM_W = D_MODEL // 2
SSM_GROUP = 16
SSM_GROUPS = SSM_W // SSM_GROUP
SSM_STATE = 64
PEER_HEADS = 8
PEER_NKEYS = 128
PEER_EXPERTS = PEER_NKEYS * PEER_NKEYS
PEER_DKEY = 256
PEER_TOPK = 16
TOK_CHUNK = 128
PROJ_W = 3 * ATT_W + SSM_W + 2 * D_MODEL
DN_ALPHA = (2.0 * DEPTH) ** 0.25
DN_BETA = (8.0 * DEPTH) ** -0.25
LN_EPS = 1e-5

kernel_name = 'moba_s5_peer_hybrid_step'


def layer_norm(x, g, b):
    xf = x.astype(jnp.float32)
    mu = jnp.mean(xf, axis=-1, keepdims=True)
    var = jnp.mean(jnp.square(xf - mu), axis=-1, keepdims=True)
    y = (xf - mu) * lax.rsqrt(var + LN_EPS) * g.astype(jnp.float32) + b.astype(jnp.float32)
    return y.astype(x.dtype)


def alibi_slopes():
    return jnp.exp2(-8.0 * (jnp.arange(N_HEADS, dtype=jnp.float32) + 1.0) / N_HEADS)


def moba_sequence(q, k, v, q_pos, chunk, slopes):
    n_blocks = k.shape[1] // MOBA_BLOCK
    kb = k.reshape(N_HEADS, n_blocks, MOBA_BLOCK, HEAD_DIM)
    vb = v.reshape(N_HEADS, n_blocks, MOBA_BLOCK, HEAD_DIM)
    kmean = jnp.mean(kb.astype(jnp.float32), axis=2)
    h_ix = jnp.arange(N_HEADS)[:, None, None]
    blk_ids = jnp.arange(n_blocks, dtype=jnp.int32)
    offs = jnp.arange(MOBA_BLOCK, dtype=jnp.int32)
    scale = HEAD_DIM ** -0.5

    def attend(args):
        qc, pc = args
        c = pc.shape[0]
        cur = pc // MOBA_BLOCK
        gate = jnp.einsum('hcd,hnd->hcn', qc.astype(jnp.float32), kmean)
        fully_past = blk_ids[None, None, :] < cur[None, :, None]
        gate = jnp.where(fully_past, gate, -jnp.inf)
        _, top = lax.top_k(gate, MOBA_TOPK)
        own = jnp.broadcast_to(cur[None, :, None], (N_HEADS, c, 1)).astype(top.dtype)
        blocks = jnp.concatenate([top, own], axis=-1)
        valid = jnp.concatenate([top < cur[None, :, None], jnp.ones((N_HEADS, c, 1), bool)], axis=-1)
        kg = kb[h_ix, blocks]
        vg = vb[h_ix, blocks]
        kpos = blocks[..., None] * MOBA_BLOCK + offs
        dist = (pc[None, :, None, None] - kpos).astype(jnp.float32)
        s = jnp.einsum('hcd,hcsbd->hcsb', qc, kg).astype(jnp.float32) * scale
        s = s - slopes[:, None, None, None] * dist
        mask = valid[..., None] & (dist >= 0)
        s = jnp.where(mask, s, -jnp.inf)
        p = jax.nn.softmax(s.reshape(N_HEADS, c, -1), axis=-1).reshape(s.shape)
        return jnp.einsum('hcsb,hcsbd->hcd', p.astype(vg.dtype), vg)

    t = q.shape[1]
    nq = t // chunk
    qs = q.reshape(N_HEADS, nq, chunk, HEAD_DIM).transpose(1, 0, 2, 3)
    ps = q_pos.reshape(nq, chunk)
    outs = lax.map(attend, (qs, ps))
    return outs.transpose(1, 0, 2, 3).reshape(N_HEADS, t, HEAD_DIM)


def moba_attention(q, k_all, v_all, start):
    bn, t, _ = q.shape
    l = k_all.shape[1]
    n_blocks = max(-(-l // MOBA_BLOCK), MOBA_TOPK)
    pad = n_blocks * MOBA_BLOCK - l
    qh = q.reshape(bn, t, N_HEADS, HEAD_DIM).transpose(0, 2, 1, 3)
    kh = jnp.pad(k_all, ((0, 0), (0, pad), (0, 0), (0, 0))).transpose(0, 2, 1, 3)
    vh = jnp.pad(v_all, ((0, 0), (0, pad), (0, 0), (0, 0))).transpose(0, 2, 1, 3)
    q_pos = start + jnp.arange(t, dtype=jnp.int32)
    chunk = Q_CHUNK if t % Q_CHUNK == 0 else t
    slopes = alibi_slopes()
    out = lax.map(lambda a: moba_sequence(a[0], a[1], a[2], q_pos, chunk, slopes), (qh, kh, vh))
    return out.transpose(0, 2, 1, 3).reshape(bn, t, ATT_W)


def _ssm_combine(e1, e2):
    a1r, a1i, b1r, b1i = e1
    a2r, a2i, b2r, b2i = e2
    return (a2r * a1r - a2i * a1i,
            a2r * a1i + a2i * a1r,
            a2r * b1r - a2i * b1i + b2r,
            a2r * b1i + a2i * b1r + b2i)


def s5_branch(u, h0_re, h0_im, a_re, a_im, log_dt, b_re, b_im, c_re, c_im, d_skip, w_glu, b_glu):
    bn, t, _ = u.shape
    f32 = jnp.float32
    uf = u.astype(f32).reshape(bn, t, SSM_GROUPS, SSM_GROUP)
    dt = jnp.exp(log_dt.astype(f32))[:, None]
    ar, ai = a_re.astype(f32), a_im.astype(f32)
    mag = jnp.exp(dt * ar)
    ang = dt * ai
    abar_re, abar_im = mag * jnp.cos(ang), mag * jnp.sin(ang)
    den = ar * ar + ai * ai
    nr, ni = abar_re - 1.0, abar_im
    f_re = (nr * ar + ni * ai) / den
    f_im = (ni * ar - nr * ai) / den
    br, bi = b_re.astype(f32), b_im.astype(f32)
    bbar_re = f_re[:, :, None] * br - f_im[:, :, None] * bi
    bbar_im = f_re[:, :, None] * bi + f_im[:, :, None] * br
    bu_re = jnp.einsum('btgc,gpc->btgp', uf, bbar_re)
    bu_im = jnp.einsum('btgc,gpc->btgp', uf, bbar_im)
    h0r, h0i = h0_re.astype(f32), h0_im.astype(f32)
    bu_re = bu_re.at[:, 0].add(abar_re * h0r - abar_im * h0i)
    bu_im = bu_im.at[:, 0].add(abar_re * h0i + abar_im * h0r)
    a_r_t = jnp.broadcast_to(abar_re, bu_re.shape)
    a_i_t = jnp.broadcast_to(abar_im, bu_re.shape)
    _, _, h_re, h_im = lax.associative_scan(_ssm_combine, (a_r_t, a_i_t, bu_re, bu_im), axis=1)
    y = (jnp.einsum('btgp,gcp->btgc', h_re, c_re.astype(f32))
         - jnp.einsum('btgp,gcp->btgc', h_im, c_im.astype(f32))
         + d_skip.astype(f32).reshape(SSM_GROUPS, SSM_GROUP) * uf)
    z = jax.nn.gelu(y.reshape(bn, t, SSM_W))
    out = z * jax.nn.sigmoid(z @ w_glu.astype(f32) + b_glu.astype(f32))
    return out.astype(u.dtype), h_re[:, -1].astype(h0_re.dtype), h_im[:, -1].astype(h0_re.dtype)


def peer_ffn(x, w_pq, sub_k1, sub_k2, peer_u, peer_v):
    shape = x.shape
    xf = x.reshape(-1, D_MODEL)
    n = xf.shape[0]
    n_pad = -(-n // TOK_CHUNK) * TOK_CHUNK
    xp = jnp.pad(xf, ((0, n_pad - n), (0, 0))).reshape(n_pad // TOK_CHUNK, TOK_CHUNK, D_MODEL)
    half = PEER_DKEY // 2

    def chunk_fn(xc):
        q = (xc @ w_pq).astype(jnp.float32).reshape(TOK_CHUNK, PEER_HEADS, PEER_DKEY)
        s1 = jnp.einsum('nhk,hmk->nhm', q[..., :half], sub_k1.astype(jnp.float32))
        s2 = jnp.einsum('nhk,hmk->nhm', q[..., half:], sub_k2.astype(jnp.float32))
        v1, i1 = lax.top_k(s1, PEER_TOPK)
        v2, i2 = lax.top_k(s2, PEER_TOPK)
        cand = (v1[..., :, None] + v2[..., None, :]).reshape(TOK_CHUNK, PEER_HEADS, PEER_TOPK * PEER_TOPK)
        sc, ci = lax.top_k(cand, PEER_TOPK)
        e = (jnp.take_along_axis(i1, ci // PEER_TOPK, axis=-1) * PEER_NKEYS
             + jnp.take_along_axis(i2, ci % PEER_TOPK, axis=-1))
        g = jax.nn.softmax(sc, axis=-1)
        ue = peer_u[e]
        a = jax.nn.gelu(jnp.einsum('nd,nhkd->nhk', xc, ue).astype(jnp.float32))
        return jnp.einsum('nhk,nhkd->nd', (g * a).astype(x.dtype), peer_v[e])

    y = lax.map(chunk_fn, xp).reshape(n_pad, D_MODEL)[:n]
    return y.reshape(shape)


def hybrid_layer(x, past_k, past_v, h0_re, h0_im, start,
                 w_in, b_in, w_a, w_b, w_o, ln1_g, ln1_b,
                 a_re, a_im, log_dt, b_re, b_im, c_re, c_im, d_skip, w_glu, b_glu,
                 ln2_g, ln2_b, w_pq, sub_k1, sub_k2, peer_u, peer_v):
    bn, t, _ = x.shape
    proj = x @ w_in + b_in
    q, k, v, u, ga, gb = jnp.split(
        proj, [ATT_W, 2 * ATT_W, 3 * ATT_W, 3 * ATT_W + SSM_W, 3 * ATT_W + SSM_W + D_MODEL], axis=-1)
    k_rows = k.reshape(bn, t, N_HEADS, HEAD_DIM)
    v_rows = v.reshape(bn, t, N_HEADS, HEAD_DIM)
    if past_k is None:
        k_all, v_all = k_rows, v_rows
    else:
        k_all = jnp.concatenate([past_k.astype(k_rows.dtype), k_rows], axis=1)
        v_all = jnp.concatenate([past_v.astype(v_rows.dtype), v_rows], axis=1)
    y_att = moba_attention(q, k_all, v_all, start)
    y_ssm, h_re, h_im = s5_branch(u, h0_re, h0_im, a_re, a_im, log_dt, b_re, b_im,
                                  c_re, c_im, d_skip, w_glu, b_glu)
    merged = jax.nn.sigmoid(ga) * (y_att @ w_a) + jax.nn.sigmoid(gb) * (y_ssm @ w_b)
    x1 = layer_norm(DN_ALPHA * x + merged @ w_o, ln1_g, ln1_b)
    x2 = layer_norm(DN_ALPHA * x1 + peer_ffn(x1, w_pq, sub_k1, sub_k2, peer_u, peer_v), ln2_g, ln2_b)
    return x2, k_rows, v_rows, h_re, h_im


def setup_inputs(seed: int = 0) -> dict:
    key = jax.random.key(seed)
    ks = jax.random.split(key, 32)
    nrm = lambda i, shape, s: jax.random.normal(ks[i], shape, jnp.float32) * s
    n_pages = PAST_LEN // PAGE_SIZE
    n_pool = (DEC_BATCH * n_pages * 5 + 3) // 4
    page_table = jax.random.permutation(ks[0], n_pool)[:DEC_BATCH * n_pages].reshape(
        DEC_BATCH, n_pages).astype(jnp.int32)
    a_im0 = math.pi * jnp.arange(SSM_STATE, dtype=jnp.float32)[None, :]
    return {
        'x_prompt': nrm(1, (BATCH, SEQ, D_MODEL), 1.0),
        'x_sample': nrm(2, (DEC_BATCH, DEC_SEQ, D_MODEL), 1.0),
        'cache_k': nrm(3, (n_pool, PAGE_SIZE, N_HEADS, HEAD_DIM), 1.0),
        'cache_v': nrm(4, (n_pool, PAGE_SIZE, N_HEADS, HEAD_DIM), 1.0),
        'state_ssm_re': nrm(5, (DEC_BATCH, SSM_GROUPS, SSM_STATE), 0.5),
        'state_ssm_im': nrm(6, (DEC_BATCH, SSM_GROUPS, SSM_STATE), 0.5),
        'page_table': page_table,
        'w_in': nrm(7, (D_MODEL, PROJ_W), D_MODEL ** -0.5),
        'b_in': nrm(8, (PROJ_W,), 0.01),
        'w_a': nrm(9, (ATT_W, D_MODEL), DN_BETA * ATT_W ** -0.5),
        'w_b': nrm(10, (SSM_W, D_MODEL), DN_BETA * SSM_W ** -0.5),
        'w_o': nrm(11, (D_MODEL, D_MODEL), DN_BETA * D_MODEL ** -0.5),
        'ln1_g': 1.0 + nrm(12, (D_MODEL,), 0.01),
        'ln1_b': nrm(13, (D_MODEL,), 0.01),
        'a_re': -0.5 + nrm(14, (SSM_GROUPS, SSM_STATE), 0.01),
        'a_im': a_im0 + nrm(15, (SSM_GROUPS, SSM_STATE), 0.01),
        'log_dt': jax.random.uniform(ks[16], (SSM_GROUPS,), jnp.float32, math.log(1e-3), math.log(1e-1)),
        'b_re': nrm(17, (SSM_GROUPS, SSM_STATE, SSM_GROUP), (2.0 * SSM_GROUP) ** -0.5),
        'b_im': nrm(18, (SSM_GROUPS, SSM_STATE, SSM_GROUP), (2.0 * SSM_GROUP) ** -0.5),
        'c_re': nrm(19, (SSM_GROUPS, SSM_GROUP, SSM_STATE), (2.0 * SSM_STATE) ** -0.5),
        'c_im': nrm(20, (SSM_GROUPS, SSM_GROUP, SSM_STATE), (2.0 * SSM_STATE) ** -0.5),
        'd_skip': nrm(21, (SSM_W,), 1.0),
        'w_glu': nrm(22, (SSM_W, SSM_W), SSM_W ** -0.5),
        'b_glu': nrm(23, (SSM_W,), 0.01),
        'ln2_g': 1.0 + nrm(24, (D_MODEL,), 0.01),
        'ln2_b': nrm(25, (D_MODEL,), 0.01),
        'w_pq': nrm(26, (D_MODEL, PEER_HEADS * PEER_DKEY), D_MODEL ** -0.5),
        'sub_k1': nrm(27, (PEER_HEADS, PEER_NKEYS, PEER_DKEY // 2), (PEER_DKEY // 2) ** -0.5),
        'sub_k2': nrm(28, (PEER_HEADS, PEER_NKEYS, PEER_DKEY // 2), (PEER_DKEY // 2) ** -0.5),
        'peer_u': nrm(29, (PEER_EXPERTS, D_MODEL), D_MODEL ** -0.5),
        'peer_v': nrm(30, (PEER_EXPERTS, D_MODEL), DN_BETA * PEER_HEADS ** -0.5),
    }


def reference(x_prompt, x_sample, cache_k, cache_v, state_ssm_re, state_ssm_im, page_table,
              w_in, b_in, w_a, w_b, w_o, ln1_g, ln1_b,
              a_re, a_im, log_dt, b_re, b_im, c_re, c_im, d_skip, w_glu, b_glu,
              ln2_g, ln2_b, w_pq, sub_k1, sub_k2, peer_u, peer_v):
    weights = (w_in, b_in, w_a, w_b, w_o, ln1_g, ln1_b,
               a_re, a_im, log_dt, b_re, b_im, c_re, c_im, d_skip, w_glu, b_glu,
               ln2_g, ln2_b, w_pq, sub_k1, sub_k2, peer_u, peer_v)
    h0 = jnp.zeros((x_prompt.shape[0], SSM_GROUPS, SSM_STATE), x_prompt.dtype)
    y_prompt = x_prompt
    for _ in range(DEPTH):
        y_prompt, k_prompt, v_prompt, ssm_re_prompt, ssm_im_prompt = hybrid_layer(
            y_prompt, None, None, h0, h0, 0, *weights)
    db, n_pages = page_table.shape
    past_k = cache_k[page_table].reshape(db, n_pages * PAGE_SIZE, N_HEADS, HEAD_DIM)
    past_v = cache_v[page_table].reshape(db, n_pages * PAGE_SIZE, N_HEADS, HEAD_DIM)
    y_sample = x_sample
    for _ in range(DEPTH):
        y_sample, k_sample, v_sample, ssm_re_sample, ssm_im_sample = hybrid_layer(
            y_sample, past_k, past_v, state_ssm_re, state_ssm_im, n_pages * PAGE_SIZE, *weights)
    return (y_prompt, y_sample, k_prompt, v_prompt, k_sample, v_sample,
            ssm_re_prompt, ssm_im_prompt, ssm_re_sample, ssm_im_sample)
```

```python
import functools
import math

import jax
import jax.numpy as jnp
from jax import lax
from jax.experimental import pallas as pl
from jax.experimental.pallas import tpu as pltpu

F32 = jnp.float32
BF16 = jnp.bfloat16

D_MODEL = 1024
ATT_W = 512
HEAD_DIM = 64
N_HEADS = 8
MOBA_BLOCK = 256
MOBA_TOPK = 3
SSM_W = 512
SSM_GROUP = 16
SSM_GROUPS = 32
SSM_STATE = 64
SSM_FLAT = SSM_GROUPS * SSM_STATE
PEER_HEADS = 8
PEER_NKEYS = 128
PEER_EXPERTS = PEER_NKEYS * PEER_NKEYS
PEER_DKEY = 256
PEER_TOPK = 16
PROJ_W = 3 * ATT_W + SSM_W + 2 * D_MODEL
DEPTH = 1
DN_ALPHA = (2.0 * DEPTH) ** 0.25
LN_EPS = 1e-5
PAGE_SIZE = 128

LANES = 128
NEG_INF = float("-inf")
VMEM_LIMIT = 56 * 1024 * 1024

NT_DIMS = (((1,), (1,)), ((), ()))


def _cparams(*sem):
    return pltpu.CompilerParams(dimension_semantics=sem, vmem_limit_bytes=VMEM_LIMIT)


def _dot(a, b):
    return jnp.dot(a, b, preferred_element_type=F32)


def _dot_nt(a, b):
    return lax.dot_general(a, b, NT_DIMS, preferred_element_type=F32)


def _split(x):
    hi = x.astype(BF16)
    lo = (x - hi.astype(F32)).astype(BF16)
    return hi, lo


def _gelu(x):
    c = math.sqrt(2.0 / math.pi)
    return 0.5 * x * (1.0 + jnp.tanh(c * (x + 0.044715 * (x * x * x))))


def _layer_norm(y, g, b):
    mu = jnp.mean(y, axis=-1, keepdims=True)
    yc = y - mu
    var = jnp.mean(yc * yc, axis=-1, keepdims=True)
    return yc * lax.rsqrt(var + LN_EPS) * g + b


PROJ_TM = 512


def _proj_kernel(x_ref, w_ref, b_ref, q_ref, k_ref, v_ref, u_ref, g_ref):
    xb = x_ref[...].astype(BF16)

    def seg(lo, hi):
        return _dot(xb, w_ref[:, lo:hi]) + b_ref[:, lo:hi]

    q_ref[...] = seg(0, ATT_W)
    k_ref[...] = seg(ATT_W, 2 * ATT_W)
    v_ref[...] = seg(2 * ATT_W, 3 * ATT_W)
    u_ref[...] = seg(3 * ATT_W, 3 * ATT_W + SSM_W)
    g_ref[...] = jax.nn.sigmoid(seg(3 * ATT_W + SSM_W, PROJ_W))


def _proj(x, w_bf, b_row):
    n = x.shape[0]
    tm = PROJ_TM
    row = lambda w: pl.BlockSpec((tm, w), lambda i: (i, 0))
    full = lambda a: pl.BlockSpec(a.shape, lambda i: (0,) * a.ndim)
    return pl.pallas_call(
        _proj_kernel,
        grid=(n // tm,),
        in_specs=[row(D_MODEL), full(w_bf), full(b_row)],
        out_specs=[row(ATT_W), row(ATT_W), row(ATT_W), row(SSM_W), row(2 * D_MODEL)],
        out_shape=[jax.ShapeDtypeStruct((n, w), F32)
                   for w in (ATT_W, ATT_W, ATT_W, SSM_W, 2 * D_MODEL)],
        compiler_params=_cparams("parallel"),
        name="proj",
    )(x, w_bf, b_row)


def _moba_select_bias(gate, cur):
    nb = gate.shape[0]
    n_id = lax.broadcasted_iota(jnp.int32, gate.shape, 0)
    past = n_id < cur
    g = jnp.where(past, gate, NEG_INF)
    rank = jnp.zeros(gate.shape, jnp.int32)
    for m in range(nb):
        gm = g[m:m + 1, :]
        ahead = (gm > g) | ((gm == g) & (m < n_id))
        rank = rank + ahead.astype(jnp.int32)
    sel = past & (rank < MOBA_TOPK)
    return jnp.where(sel, 0.0, NEG_INF)


def _moba_prompt_kernel(slopes_ref, q_ref, k_ref, v_ref, o_ref,
                        kb_ref, vt_ref, selb_ref, ot_ref):
    t = q_ref.shape[0]
    nb = t // MOBA_BLOCK
    blk = MOBA_BLOCK
    hp = pl.program_id(1)
    lane = lax.broadcasted_iota(jnp.int32, (1, LANES), 1)

    k = k_ref[...]
    kb_ref[...] = k.astype(BF16)
    vt_ref[...] = v_ref[...].T.astype(BF16)
    kmean = jnp.mean(k.reshape(nb, blk, LANES), axis=1)
    q_all = q_ref[...]
    q_hi, q_lo = _split(q_all)
    cur = lax.broadcasted_iota(jnp.int32, (1, t), 1) // blk
    d0 = (lax.broadcasted_iota(jnp.int32, (blk, blk), 1)
          - lax.broadcasted_iota(jnp.int32, (blk, blk), 0)).astype(F32)

    for hh in range(2):
        hmask = (lane >= HEAD_DIM * hh) & (lane < HEAD_DIM * (hh + 1))
        slope = slopes_ref[2 * hp + hh]
        km_hi, km_lo = _split(jnp.where(hmask, kmean, 0.0))
        gate = _dot_nt(km_hi, q_hi) + _dot_nt(km_hi, q_lo) + _dot_nt(km_lo, q_hi)
        selb_ref[...] = _moba_select_bias(gate, cur)

        def q_block(i, _):
            qs = pl.multiple_of(i * blk, blk)
            qi = jnp.where(hmask, q_ref[pl.ds(qs, blk), :] * (HEAD_DIM ** -0.5), 0.0).astype(BF16)
            s = _dot_nt(kb_ref[pl.ds(qs, blk), :], qi) - slope * d0
            s = jnp.where(d0 >= 0.0, s, NEG_INF)
            m0 = jnp.max(s, axis=0, keepdims=True)
            p = jnp.exp(s - m0)
            l0 = jnp.sum(p, axis=0, keepdims=True)
            acc0 = _dot(vt_ref[:, pl.ds(qs, blk)], p.astype(BF16))

            def k_block(j, carry):
                m, l, acc = carry
                ks = pl.multiple_of(j * blk, blk)
                dist = d0 + ((i - j) * blk).astype(F32)
                s = _dot_nt(kb_ref[pl.ds(ks, blk), :], qi) - slope * dist
                s = s + selb_ref[pl.ds(j, 1), pl.ds(qs, blk)]
                m_new = jnp.maximum(m, jnp.max(s, axis=0, keepdims=True))
                alpha = jnp.exp(m - m_new)
                p = jnp.exp(s - m_new)
                l_new = alpha * l + jnp.sum(p, axis=0, keepdims=True)
                acc_new = alpha * acc + _dot(vt_ref[:, pl.ds(ks, blk)], p.astype(BF16))
                return m_new, l_new, acc_new

            m, l, acc = lax.fori_loop(0, i, k_block, (m0, l0, acc0))
            res = acc / l
            rows = slice(HEAD_DIM * hh, HEAD_DIM * (hh + 1))
            ot_ref[rows, pl.ds(qs, blk)] = res[rows, :]
            return 0

        lax.fori_loop(0, nb, q_block, 0)

    o_ref[...] = ot_ref[...].T


def _moba_prompt(q, k, v, slopes):
    b, t, _ = q.shape
    spec = pl.BlockSpec((None, t, LANES), lambda bi, hp, *_: (bi, 0, hp))
    grid_spec = pltpu.PrefetchScalarGridSpec(
        num_scalar_prefetch=1,
        grid=(b, ATT_W // LANES),
        in_specs=[spec, spec, spec],
        out_specs=spec,
        scratch_shapes=[
            pltpu.VMEM((t, LANES), BF16),
            pltpu.VMEM((LANES, t), BF16),
            pltpu.VMEM((t // MOBA_BLOCK, t), F32),
            pltpu.VMEM((LANES, t), F32),
        ],
    )
    return pl.pallas_call(
        _moba_prompt_kernel,
        grid_spec=grid_spec,
        out_shape=jax.ShapeDtypeStruct((b, t, ATT_W), F32),
        compiler_params=_cparams("parallel", "parallel"),
        name="moba_prompt",
    )(slopes, q, k, v)


def _moba_sample_kernel(pt_ref, slopes_ref, q_ref, kn_ref, vn_ref, *rest, n_pages):
    kp_refs = rest[:n_pages]
    vp_refs = rest[n_pages:2 * n_pages]
    o_ref = rest[2 * n_pages]
    nq = q_ref.shape[0]
    n_past = n_pages * PAGE_SIZE
    nb = n_past // MOBA_BLOCK
    pages_per_block = MOBA_BLOCK // PAGE_SIZE
    rows = LANES
    assert N_HEADS * nq <= rows

    q = q_ref[...]
    r_id = lax.broadcasted_iota(jnp.int32, (rows, ATT_W), 0)
    c_id = lax.broadcasted_iota(jnp.int32, (rows, ATT_W), 1)
    q_rep = jnp.concatenate([q] * (rows // nq), axis=0)
    own = (c_id // HEAD_DIM == r_id // nq) & (r_id < N_HEADS * nq)
    qm = jnp.where(own, q_rep, 0.0)
    qm_hi, qm_lo = _split(qm)
    qs = (qm * (HEAD_DIM ** -0.5)).astype(BF16)

    col = lax.broadcasted_iota(jnp.int32, (1, rows), 1)
    tq = (col % nq).astype(F32)
    slope = jnp.zeros((1, rows), F32)
    for h in range(N_HEADS):
        slope = jnp.where(col // nq == h, slopes_ref[h], slope)

    means = []
    for n in range(nb):
        acc = jnp.zeros((1, ATT_W), F32)
        for pg in range(pages_per_block):
            acc = acc + jnp.sum(kp_refs[n * pages_per_block + pg][...], axis=0, keepdims=True)
        means.append(acc * (1.0 / MOBA_BLOCK))
    kmean = jnp.concatenate(means, axis=0)
    km_hi, km_lo = _split(kmean)
    gate = _dot_nt(km_hi, qm_hi) + _dot_nt(km_hi, qm_lo) + _dot_nt(km_lo, qm_hi)
    cur = jnp.full((1, rows), nb, jnp.int32)
    selb = _moba_select_bias(gate, cur)

    kpos0 = lax.broadcasted_iota(jnp.int32, (PAGE_SIZE, rows), 0).astype(F32)
    pieces = []
    for pg in range(n_pages):
        s = _dot_nt(kp_refs[pg][...].astype(BF16), qs)
        dist = (n_past - pg * PAGE_SIZE) + tq - kpos0
        s = s - slope * dist + selb[pg // pages_per_block:pg // pages_per_block + 1, :]
        pieces.append(s)
    s_new = _dot_nt(kn_ref[...].astype(BF16), qs)
    dist_new = tq - lax.broadcasted_iota(jnp.int32, (nq, rows), 0).astype(F32)
    s_new = jnp.where(dist_new >= 0.0, s_new - slope * dist_new, NEG_INF)

    m = jnp.max(s_new, axis=0, keepdims=True)
    for s in pieces:
        m = jnp.maximum(m, jnp.max(s, axis=0, keepdims=True))
    p_new = jnp.exp(s_new - m)
    probs = [jnp.exp(s - m) for s in pieces]
    l = jnp.sum(p_new, axis=0, keepdims=True)
    for p in probs:
        l = l + jnp.sum(p, axis=0, keepdims=True)
    inv_l = 1.0 / l
    p_new_t = jnp.concatenate([p_new * inv_l, jnp.zeros((LANES - nq, rows), F32)], axis=0).T
    v_new = jnp.concatenate([vn_ref[...], jnp.zeros((LANES - nq, ATT_W), F32)], axis=0)
    out = _dot(p_new_t.astype(BF16), v_new.astype(BF16))
    for pg in range(n_pages):
        out = out + _dot((probs[pg] * inv_l).T.astype(BF16), vp_refs[pg][...].astype(BF16))

    res = jnp.zeros((nq, ATT_W), F32)
    c_q = lax.broadcasted_iota(jnp.int32, (nq, ATT_W), 1)
    for h in range(N_HEADS):
        res = jnp.where(c_q // HEAD_DIM == h, out[h * nq:(h + 1) * nq, :], res)
    o_ref[...] = res


def _moba_sample(q, k_new, v_new, cache_k, cache_v, page_table, slopes):
    db, nq, _ = q.shape
    n_pages = page_table.shape[1]
    new_spec = pl.BlockSpec((None, nq, ATT_W), lambda s, *_: (s, 0, 0))

    def page_spec(pg):
        return pl.BlockSpec((None, PAGE_SIZE, ATT_W), lambda s, pt, sl: (pt[s, pg], 0, 0))

    grid_spec = pltpu.PrefetchScalarGridSpec(
        num_scalar_prefetch=2,
        grid=(db,),
        in_specs=([new_spec] * 3 + [page_spec(pg) for pg in range(n_pages)] * 2),
        out_specs=new_spec,
    )
    ck = cache_k.reshape(cache_k.shape[0], PAGE_SIZE, ATT_W)
    cv = cache_v.reshape(cache_v.shape[0], PAGE_SIZE, ATT_W)
    return pl.pallas_call(
        functools.partial(_moba_sample_kernel, n_pages=n_pages),
        grid_spec=grid_spec,
        out_shape=jax.ShapeDtypeStruct((db, nq, ATT_W), F32),
        compiler_params=_cparams("parallel"),
        name="moba_sample",
    )(page_table, slopes, q, k_new, v_new, *([ck] * n_pages), *([cv] * n_pages))


S5_CHUNK = 256


def _s5_prep_kernel(are_ref, aim_ref, ldt_ref, bre_ref, bim_ref,
                    bbar_ref, apr_ref, api_ref):
    ar, ai = are_ref[...], aim_ref[...]
    dt = jnp.exp(ldt_ref[...])
    mag = jnp.exp(dt * ar)
    ang = dt * ai
    abr, abi = mag * jnp.cos(ang), mag * jnp.sin(ang)
    den = ar * ar + ai * ai
    nr, ni = abr - 1.0, abi
    f_re = (nr * ar + ni * ai) / den
    f_im = (ni * ar - nr * ai) / den
    bre, bim = bre_ref[...], bim_ref[...]
    bbar_ref[:, :SSM_FLAT] = (f_re * bre - f_im * bim).astype(BF16)
    bbar_ref[:, SSM_FLAT:] = (f_re * bim + f_im * bre).astype(BF16)
    apr_ref[0:1, :] = abr
    api_ref[0:1, :] = abi
    d = 1
    while d < S5_CHUNK:
        pr, pi = apr_ref[0:d, :], api_ref[0:d, :]
        er, ei = apr_ref[d - 1:d, :], api_ref[d - 1:d, :]
        apr_ref[d:2 * d, :] = pr * er - pi * ei
        api_ref[d:2 * d, :] = pr * ei + pi * er
        d *= 2


def _s5_prep(a_re, a_im, log_dt, b_re, b_im):
    eye = jnp.eye(SSM_GROUPS, dtype=F32)
    blockdiag = lambda b: jnp.einsum('gpc,gh->gchp', b, eye).reshape(SSM_W, SSM_FLAT)
    flat = lambda a: a.reshape(1, SSM_FLAT)
    ldt = jnp.broadcast_to(log_dt[:, None], (SSM_GROUPS, SSM_STATE))
    return pl.pallas_call(
        _s5_prep_kernel,
        out_shape=[jax.ShapeDtypeStruct((SSM_W, 2 * SSM_FLAT), BF16),
                   jax.ShapeDtypeStruct((S5_CHUNK, SSM_FLAT), F32),
                   jax.ShapeDtypeStruct((S5_CHUNK, SSM_FLAT), F32)],
        compiler_params=pltpu.CompilerParams(vmem_limit_bytes=VMEM_LIMIT),
        name="s5_prep",
    )(flat(a_re), flat(a_im), flat(ldt), blockdiag(b_re), blockdiag(b_im))


def _s5_readout(h_re, h_im, u, ccat_ref, d_ref, wglu_ref, bglu_ref):
    hcat = jnp.concatenate([h_re.astype(BF16), h_im.astype(BF16)], axis=1)
    y = _dot(hcat, ccat_ref[...]) + d_ref[...] * u
    z = _gelu(y)
    return z * jax.nn.sigmoid(_dot(z.astype(BF16), wglu_ref[...]) + bglu_ref[...])


def _s5_prompt_kernel(u_ref, bbar_ref, apr_ref, api_ref, ccat_ref, d_ref, wglu_ref, bglu_ref,
                      y_ref, hre_ref, him_ref, cr_ref, ci_ref):
    c = pl.program_id(1)
    n = u_ref.shape[0]

    @pl.when(c == 0)
    def _():
        cr_ref[...] = jnp.zeros_like(cr_ref)
        ci_ref[...] = jnp.zeros_like(ci_ref)

    u = u_ref[...]
    bu = _dot(u.astype(BF16), bbar_ref[...])
    hr, hi = bu[:, :SSM_FLAT], bu[:, SSM_FLAT:]
    row = lax.broadcasted_iota(jnp.int32, (n, SSM_FLAT), 0)
    d = 1
    while d < n:
        er, ei = apr_ref[d - 1:d, :], api_ref[d - 1:d, :]
        keep = row >= d
        sr = jnp.where(keep, pltpu.roll(hr, d, axis=0), 0.0)
        si = jnp.where(keep, pltpu.roll(hi, d, axis=0), 0.0)
        hr, hi = hr + er * sr - ei * si, hi + er * si + ei * sr
        d *= 2
    cr, ci = cr_ref[...], ci_ref[...]
    pr, pi = apr_ref[...], api_ref[...]
    hr = hr + pr * cr - pi * ci
    hi = hi + pr * ci + pi * cr
    cr_ref[...] = hr[n - 1:n, :]
    ci_ref[...] = hi[n - 1:n, :]
    y_ref[...] = _s5_readout(hr, hi, u, ccat_ref, d_ref, wglu_ref, bglu_ref)
    hre_ref[...] = hr[n - 1:n, :]
    him_ref[...] = hi[n - 1:n, :]


def _s5_prompt(u, bbar, apr, api, ccat, d_row, wglu_bf, bglu_row):
    b, t, _ = u.shape
    n = S5_CHUNK
    full = lambda a: pl.BlockSpec(a.shape, lambda bi, c: (0,) * a.ndim)
    tok = pl.BlockSpec((None, n, SSM_W), lambda bi, c: (bi, c, 0))
    st = pl.BlockSpec((None, 1, SSM_FLAT), lambda bi, c: (bi, 0, 0))
    y, hre, him = pl.pallas_call(
        _s5_prompt_kernel,
        grid=(b, t // n),
        in_specs=[tok, full(bbar), full(apr), full(api), full(ccat), full(d_row),
                  full(wglu_bf), full(bglu_row)],
        out_specs=[tok, st, st],
        out_shape=[jax.ShapeDtypeStruct((b, t, SSM_W), F32),
                   jax.ShapeDtypeStruct((b, 1, SSM_FLAT), F32),
                   jax.ShapeDtypeStruct((b, 1, SSM_FLAT), F32)],
        scratch_shapes=[pltpu.VMEM((1, SSM_FLAT), F32), pltpu.VMEM((1, SSM_FLAT), F32)],
        compiler_params=_cparams("parallel", "arbitrary"),
        name="s5_prompt",
    )(u, bbar, apr, api, ccat, d_row, wglu_bf, bglu_row)
    return y, hre.reshape(b, SSM_GROUPS, SSM_STATE), him.reshape(b, SSM_GROUPS, SSM_STATE)


def _s5_sample_kernel(u_ref, h0r_ref, h0i_ref, bbar_ref, apr_ref, api_ref, ccat_ref, d_ref,
                      wglu_ref, bglu_ref, y_ref, hre_ref, him_ref, *, n_steps):
    hr, hi = h0r_ref[...], h0i_ref[...]
    ar, ai = apr_ref[0:1, :], api_ref[0:1, :]
    for t in range(n_steps):
        u = u_ref[:, t * SSM_W:(t + 1) * SSM_W]
        bu = _dot(u.astype(BF16), bbar_ref[...])
        hr, hi = (ar * hr - ai * hi + bu[:, :SSM_FLAT],
                  ar * hi + ai * hr + bu[:, SSM_FLAT:])
        y_ref[:, t * SSM_W:(t + 1) * SSM_W] = _s5_readout(
            hr, hi, u, ccat_ref, d_ref, wglu_ref, bglu_ref)
    hre_ref[...] = hr
    him_ref[...] = hi


def _s5_sample(u, h0_re, h0_im, bbar, apr, api, ccat, d_row, wglu_bf, bglu_row):
    db, t, _ = u.shape
    y, hre, him = pl.pallas_call(
        functools.partial(_s5_sample_kernel, n_steps=t),
        out_shape=[jax.ShapeDtypeStruct((db, t * SSM_W), F32),
                   jax.ShapeDtypeStruct((db, SSM_FLAT), F32),
                   jax.ShapeDtypeStruct((db, SSM_FLAT), F32)],
        compiler_params=pltpu.CompilerParams(vmem_limit_bytes=VMEM_LIMIT),
        name="s5_sample",
    )(u.reshape(db, t * SSM_W), h0_re.reshape(db, SSM_FLAT), h0_im.reshape(db, SSM_FLAT),
      bbar, apr, api, ccat, d_row, wglu_bf, bglu_row)
    return (y.reshape(db, t, SSM_W), hre.reshape(db, SSM_GROUPS, SSM_STATE),
            him.reshape(db, SSM_GROUPS, SSM_STATE))


MERGE_TM = 512


def _merge_kernel(x_ref, att_ref, ssm_ref, g_ref, wa_ref, wb_ref, wo_ref, lg_ref, lb_ref, o_ref):
    ya = _dot(att_ref[...].astype(BF16), wa_ref[...])
    yb = _dot(ssm_ref[...].astype(BF16), wb_ref[...])
    merged = g_ref[:, :D_MODEL] * ya + g_ref[:, D_MODEL:] * yb
    y = DN_ALPHA * x_ref[...] + _dot(merged.astype(BF16), wo_ref[...])
    o_ref[...] = _layer_norm(y, lg_ref[...], lb_ref[...])


def _merge(x, att, ssm, gates, wa_bf, wb_bf, wo_bf, lg_row, lb_row):
    n = x.shape[0]
    tm = MERGE_TM
    row = lambda w: pl.BlockSpec((tm, w), lambda i: (i, 0))
    full = lambda a: pl.BlockSpec(a.shape, lambda i: (0,) * a.ndim)
    return pl.pallas_call(
        _merge_kernel,
        grid=(n // tm,),
        in_specs=[row(D_MODEL), row(ATT_W), row(SSM_W), row(2 * D_MODEL),
                  full(wa_bf), full(wb_bf), full(wo_bf), full(lg_row), full(lb_row)],
        out_specs=row(D_MODEL),
        out_shape=jax.ShapeDtypeStruct((n, D_MODEL), F32),
        compiler_params=_cparams("parallel"),
        name="merge",
    )(x, att, ssm, gates, wa_bf, wb_bf, wo_bf, lg_row, lb_row)


PEER_TS = 512
PEER_EC = 1024
PEER_NCH = PEER_EXPERTS // PEER_EC
PEER_ROWS_PER_STEP = PEER_EC // PEER_NKEYS
HALF_KEY = PEER_DKEY // 2


def _top_values(s, k):
    vals = []
    for r in range(k):
        m = jnp.max(s, axis=0, keepdims=True)
        vals.append(m)
        if r + 1 < k:
            s = jnp.where(s == m, NEG_INF, s)
    return vals


def _dot3(a_hi, a_lo, b):
    b_hi, b_lo = _split(b)
    return _dot(a_hi, b_hi) + _dot(a_hi, b_lo) + _dot(a_lo, b_hi)


def _peer_kernel(x_ref, wqh_ref, wql_ref, k1h_ref, k1l_ref, k2h_ref, k2l_ref,
                 u_ref, vt_ref, lg_ref, lb_ref, o_ref,
                 xb_ref, s2_ref, p2_ref, th_ref, cw_ref, acc_ref, st_ref, ht_ref, v2_ref):
    c = pl.program_id(1)
    ts = x_ref.shape[0]

    @pl.when(c == 0)
    def _prologue():
        x_hi, x_lo = _split(x_ref[...])
        xb_ref[...] = x_hi
        wqh, wql = wqh_ref[...], wql_ref[...]
        qt = _dot_nt(wqh, x_hi) + _dot_nt(wqh, x_lo) + _dot_nt(wql, x_hi)
        b_id = lax.broadcasted_iota(jnp.int32, (PEER_TOPK, ts), 0)
        for h in range(PEER_HEADS):
            base = h * PEER_DKEY
            s1 = _dot3(k1h_ref[h], k1l_ref[h], qt[base:base + HALF_KEY, :])
            s2 = _dot3(k2h_ref[h], k2l_ref[h], qt[base + HALF_KEY:base + PEER_DKEY, :])
            v1 = _top_values(s1, PEER_TOPK)
            v2 = _top_values(s2, PEER_TOPK)
            for r in range(PEER_TOPK):
                v2_ref[r:r + 1, :] = v2[r]
            v2_all = v2_ref[...]
            cands = [jnp.where(b_id < PEER_TOPK // (a + 1), v1[a] + v2_all, NEG_INF)
                     for a in range(PEER_TOPK)]
            sc = _top_values(jnp.concatenate(cands, axis=0), PEER_TOPK)
            tau = sc[PEER_TOPK - 1]
            z = jnp.zeros_like(tau)
            for r in range(PEER_TOPK):
                z = z + jnp.exp(sc[r] - sc[0])
            theta = jnp.full(s1.shape, jnp.inf, F32)
            for b in range(PEER_TOPK):
                theta = jnp.minimum(theta, jnp.where(s1 + v2[b] >= tau, v2[b], jnp.inf))
            th_ref[h] = theta
            cw_ref[h] = jnp.exp(s1 - v1[0]) / z
            s2_ref[h] = s2
            p2_ref[h] = jnp.exp(s2 - v2[0])
        acc_ref[...] = jnp.zeros_like(acc_ref)

    st_ref[...] = _dot_nt(u_ref[...], xb_ref[...])

    i1_base = pl.multiple_of(c * PEER_ROWS_PER_STEP, PEER_ROWS_PER_STEP)

    def lane_tile(lt, _):
        cols = pl.ds(pl.multiple_of(lt * LANES, LANES), LANES)
        ths = [th_ref[h, pl.ds(i1_base, PEER_ROWS_PER_STEP), cols] for h in range(PEER_HEADS)]
        cws = [cw_ref[h, pl.ds(i1_base, PEER_ROWS_PER_STEP), cols] for h in range(PEER_HEADS)]
        for j in range(PEER_ROWS_PER_STEP):
            rows = slice(j * PEER_NKEYS, (j + 1) * PEER_NKEYS)
            w = jnp.zeros((PEER_NKEYS, LANES), F32)
            for h in range(PEER_HEADS):
                keep = s2_ref[h, :, cols] >= ths[h][j:j + 1, :]
                w = w + jnp.where(keep, p2_ref[h, :, cols] * cws[h][j:j + 1, :], 0.0)
            ht_ref[rows, cols] = (w * _gelu(st_ref[rows, cols])).astype(BF16)
        return 0

    lax.fori_loop(0, ts // LANES, lane_tile, 0)
    acc_ref[...] += _dot(vt_ref[...], ht_ref[...])

    @pl.when(c == PEER_NCH - 1)
    def _epilogue():
        y = DN_ALPHA * x_ref[...] + acc_ref[...].T
        o_ref[...] = _layer_norm(y, lg_ref[...], lb_ref[...])


def _peer(x1, wq_hi, wq_lo, k1_hi, k1_lo, k2_hi, k2_lo, u_bf, vt_bf, lg_row, lb_row):
    n = x1.shape[0]
    ts = PEER_TS
    full = lambda a: pl.BlockSpec(a.shape, lambda i, c: (0,) * a.ndim)
    tok = pl.BlockSpec((ts, D_MODEL), lambda i, c: (i, 0))
    tables = pltpu.VMEM((PEER_HEADS, PEER_NKEYS, ts), F32)
    return pl.pallas_call(
        _peer_kernel,
        grid=(n // ts, PEER_NCH),
        in_specs=[tok, full(wq_hi), full(wq_lo), full(k1_hi), full(k1_lo), full(k2_hi), full(k2_lo),
                  pl.BlockSpec((PEER_EC, D_MODEL), lambda i, c: (c, 0)),
                  pl.BlockSpec((D_MODEL, PEER_EC), lambda i, c: (0, c)),
                  full(lg_row), full(lb_row)],
        out_specs=tok,
        out_shape=jax.ShapeDtypeStruct((n, D_MODEL), F32),
        scratch_shapes=[
            pltpu.VMEM((ts, D_MODEL), BF16),
            tables, tables, tables, tables,
            pltpu.VMEM((D_MODEL, ts), F32),
            pltpu.VMEM((PEER_EC, ts), F32),
            pltpu.VMEM((PEER_EC, ts), BF16),
            pltpu.VMEM((PEER_TOPK, ts), F32),
        ],
        compiler_params=_cparams("parallel", "arbitrary"),
        name="peer",
    )(x1, wq_hi, wq_lo, k1_hi, k1_lo, k2_hi, k2_lo, u_bf, vt_bf, lg_row, lb_row)


def kernel(x_prompt, x_sample, cache_k, cache_v, state_ssm_re, state_ssm_im, page_table, w_in, b_in, w_a, w_b, w_o, ln1_g, ln1_b, a_re, a_im, log_dt, b_re, b_im, c_re, c_im, d_skip, w_glu, b_glu, ln2_g, ln2_b, w_pq, sub_k1, sub_k2, peer_u, peer_v):
    bn, t, _ = x_prompt.shape
    db, dt_, _ = x_sample.shape
    row = lambda a: a.reshape(1, -1).astype(F32)
    slopes = jnp.exp2(-8.0 * (jnp.arange(N_HEADS, dtype=F32) + 1.0) / N_HEADS)

    w_in_bf = w_in.astype(BF16)
    wa_bf, wb_bf, wo_bf, wglu_bf = (w.astype(BF16) for w in (w_a, w_b, w_o, w_glu))
    eye = jnp.eye(SSM_GROUPS, dtype=F32)
    c_blk = lambda cm: jnp.einsum('gcp,gh->gphc', cm, eye).reshape(SSM_FLAT, SSM_W)
    ccat = jnp.concatenate([c_blk(c_re), -c_blk(c_im)], axis=0).astype(BF16)
    wq_hi, wq_lo = _split(w_pq.T)
    k1_hi, k1_lo = _split(sub_k1)
    k2_hi, k2_lo = _split(sub_k2)
    u_bf = peer_u.astype(BF16)
    vt_bf = peer_v.T.astype(BF16)

    bbar, apr, api = _s5_prep(a_re, a_im, log_dt, b_re, b_im)
    s5_w = (bbar, apr, api, ccat, row(d_skip), wglu_bf, row(b_glu))

    xp = x_prompt.reshape(bn * t, D_MODEL)
    q, k, v, u, gates = _proj(xp, w_in_bf, row(b_in))
    att = _moba_prompt(q.reshape(bn, t, ATT_W), k.reshape(bn, t, ATT_W), v.reshape(bn, t, ATT_W), slopes)
    ssm, hre_p, him_p = _s5_prompt(u.reshape(bn, t, SSM_W), *s5_w)
    x1 = _merge(xp, att.reshape(bn * t, ATT_W), ssm.reshape(bn * t, SSM_W), gates,
                wa_bf, wb_bf, wo_bf, row(ln1_g), row(ln1_b))
    y_prompt = _peer(x1, wq_hi, wq_lo, k1_hi, k1_lo, k2_hi, k2_lo, u_bf, vt_bf,
                     row(ln2_g), row(ln2_b)).reshape(bn, t, D_MODEL)
    k_prompt = k.reshape(bn, t, N_HEADS, HEAD_DIM)
    v_prompt = v.reshape(bn, t, N_HEADS, HEAD_DIM)

    xs = x_sample.reshape(db * dt_, D_MODEL)
    q, k, v, u, gates = _proj(xs, w_in_bf, row(b_in))
    att = _moba_sample(q.reshape(db, dt_, ATT_W), k.reshape(db, dt_, ATT_W), v.reshape(db, dt_, ATT_W),
                       cache_k, cache_v, page_table, slopes)
    ssm, hre_s, him_s = _s5_sample(u.reshape(db, dt_, SSM_W), state_ssm_re, state_ssm_im, *s5_w)
    x1 = _merge(xs, att.reshape(db * dt_, ATT_W), ssm.reshape(db * dt_, SSM_W), gates,
                wa_bf, wb_bf, wo_bf, row(ln1_g), row(ln1_b))
    y_sample = _peer(x1, wq_hi, wq_lo, k1_hi, k1_lo, k2_hi, k2_lo, u_bf, vt_bf,
                     row(ln2_g), row(ln2_b)).reshape(db, dt_, D_MODEL)
    k_sample = k.reshape(db, dt_, N_HEADS, HEAD_DIM)
    v_sample = v.reshape(db, dt_, N_HEADS, HEAD_DIM)

    return (y_prompt, y_sample, k_prompt, v_prompt, k_sample, v_sample,
            hre_p, him_p, hre_s, him_s)
```

```python
import functools
import math

import jax
import jax.numpy as jnp
from jax import lax
from jax.experimental import pallas as pl
from jax.experimental.pallas import tpu as pltpu

F32 = jnp.float32
BF16 = jnp.bfloat16

D_MODEL = 1024
ATT_W = 512
HEAD_DIM = 64
N_HEADS = 8
MOBA_BLOCK = 256
MOBA_TOPK = 3
SSM_W = 512
SSM_GROUP = 16
SSM_GROUPS = 32
SSM_STATE = 64
SSM_FLAT = SSM_GROUPS * SSM_STATE
PEER_HEADS = 8
PEER_NKEYS = 128
PEER_EXPERTS = PEER_NKEYS * PEER_NKEYS
PEER_DKEY = 256
PEER_TOPK = 16
PROJ_W = 3 * ATT_W + SSM_W + 2 * D_MODEL
DEPTH = 1
DN_ALPHA = (2.0 * DEPTH) ** 0.25
LN_EPS = 1e-5
PAGE_SIZE = 128

LANES = 128
NEG_INF = float("-inf")
VMEM_LIMIT = 56 * 1024 * 1024

NT_DIMS = (((1,), (1,)), ((), ()))


def _cparams(*sem):
    return pltpu.CompilerParams(dimension_semantics=sem, vmem_limit_bytes=VMEM_LIMIT)


def _dot(a, b):
    return jnp.dot(a, b, preferred_element_type=F32)


def _dot_nt(a, b):
    return lax.dot_general(a, b, NT_DIMS, preferred_element_type=F32)


def _split(x):
    hi = x.astype(BF16)
    lo = (x - hi.astype(F32)).astype(BF16)
    return hi, lo


def _gelu(x):
    c = math.sqrt(2.0 / math.pi)
    return 0.5 * x * (1.0 + jnp.tanh(c * (x + 0.044715 * (x * x * x))))


def _layer_norm(y, g, b):
    mu = jnp.mean(y, axis=-1, keepdims=True)
    yc = y - mu
    var = jnp.mean(yc * yc, axis=-1, keepdims=True)
    return yc * lax.rsqrt(var + LN_EPS) * g + b


PROJ_TM = 512


def _proj_kernel(x_ref, w_ref, b_ref, q_ref, k_ref, v_ref, u_ref, g_ref):
    xb = x_ref[...].astype(BF16)

    def seg(lo, hi):
        return _dot(xb, w_ref[:, lo:hi]) + b_ref[:, lo:hi]

    q_ref[...] = seg(0, ATT_W)
    k_ref[...] = seg(ATT_W, 2 * ATT_W)
    v_ref[...] = seg(2 * ATT_W, 3 * ATT_W)
    u_ref[...] = seg(3 * ATT_W, 3 * ATT_W + SSM_W)
    g_ref[...] = jax.nn.sigmoid(seg(3 * ATT_W + SSM_W, PROJ_W))


def _proj(x, w_bf, b_row):
    n = x.shape[0]
    tm = PROJ_TM
    row = lambda w: pl.BlockSpec((tm, w), lambda i: (i, 0))
    full = lambda a: pl.BlockSpec(a.shape, lambda i: (0,) * a.ndim)
    return pl.pallas_call(
        _proj_kernel,
        grid=(n // tm,),
        in_specs=[row(D_MODEL), full(w_bf), full(b_row)],
        out_specs=[row(ATT_W), row(ATT_W), row(ATT_W), row(SSM_W), row(2 * D_MODEL)],
        out_shape=[jax.ShapeDtypeStruct((n, w), F32)
                   for w in (ATT_W, ATT_W, ATT_W, SSM_W, 2 * D_MODEL)],
        compiler_params=_cparams("parallel"),
        name="proj",
    )(x, w_bf, b_row)


def _moba_select_bias(gate, cur):
    nb = gate.shape[0]
    n_id = lax.broadcasted_iota(jnp.int32, gate.shape, 0)
    past = n_id < cur
    g = jnp.where(past, gate, NEG_INF)
    rank = jnp.zeros(gate.shape, jnp.int32)
    for m in range(nb):
        gm = g[m:m + 1, :]
        ahead = (gm > g) | ((gm == g) & (m < n_id))
        rank = rank + ahead.astype(jnp.int32)
    sel = past & (rank < MOBA_TOPK)
    return jnp.where(sel, 0.0, NEG_INF)


def _moba_prompt_kernel(slopes_ref, q_ref, k_ref, v_ref, o_ref,
                        kb_ref, vt_ref, ot_ref):
    t = q_ref.shape[0]
    nb = t // MOBA_BLOCK
    blk = MOBA_BLOCK
    hp = pl.program_id(1)
    lane = lax.broadcasted_iota(jnp.int32, (1, LANES), 1)

    k = k_ref[...]
    kb_ref[...] = k.astype(BF16)
    vt_ref[...] = v_ref[...].T.astype(BF16)
    kmean = jnp.mean(k.reshape(nb, blk, LANES), axis=1)
    q_all = q_ref[...]
    q_hi, q_lo = _split(q_all)
    cur = lax.broadcasted_iota(jnp.int32, (1, t), 1) // blk
    d0 = (lax.broadcasted_iota(jnp.int32, (blk, blk), 1)
          - lax.broadcasted_iota(jnp.int32, (blk, blk), 0)).astype(F32)

    for hh in range(2):
        hmask = (lane >= HEAD_DIM * hh) & (lane < HEAD_DIM * (hh + 1))
        slope = slopes_ref[2 * hp + hh]
        km_hi, km_lo = _split(jnp.where(hmask, kmean, 0.0))
        gate = _dot_nt(km_hi, q_hi) + _dot_nt(km_hi, q_lo) + _dot_nt(km_lo, q_hi)
        selb = _moba_select_bias(gate, cur)
        slope_d0 = slope * d0
        rows = slice(HEAD_DIM * hh, HEAD_DIM * (hh + 1))

        for i in range(nb):
            qcols = slice(i * blk, (i + 1) * blk)
            qi = jnp.where(hmask, q_ref[qcols, :] * (HEAD_DIM ** -0.5), 0.0).astype(BF16)
            n_keys = (i + 1) * blk
            s_all = _dot_nt(kb_ref[0:n_keys, :], qi)
            tiles = []
            for j in range(i + 1):
                s = s_all[j * blk:(j + 1) * blk, :] - (slope_d0 + slope * float((i - j) * blk))
                if j < i:
                    s = s + selb[j:j + 1, qcols]
                else:
                    s = jnp.where(d0 >= 0.0, s, NEG_INF)
                tiles.append(s)
            m = jnp.max(tiles[0], axis=0, keepdims=True)
            for s in tiles[1:]:
                m = jnp.maximum(m, jnp.max(s, axis=0, keepdims=True))
            probs = [jnp.exp(s - m) for s in tiles]
            l = jnp.sum(probs[0], axis=0, keepdims=True)
            for p in probs[1:]:
                l = l + jnp.sum(p, axis=0, keepdims=True)
            p_all = jnp.concatenate([p.astype(BF16) for p in probs], axis=0)
            res = _dot(vt_ref[:, 0:n_keys], p_all) / l
            ot_ref[rows, qcols] = res[rows, :]

    o_ref[...] = ot_ref[...].T


def _moba_prompt(q, k, v, slopes):
    b, t, _ = q.shape
    spec = pl.BlockSpec((None, t, LANES), lambda bi, hp, *_: (bi, 0, hp))
    grid_spec = pltpu.PrefetchScalarGridSpec(
        num_scalar_prefetch=1,
        grid=(b, ATT_W // LANES),
        in_specs=[spec, spec, spec],
        out_specs=spec,
        scratch_shapes=[
            pltpu.VMEM((t, LANES), BF16),
            pltpu.VMEM((LANES, t), BF16),
            pltpu.VMEM((LANES, t), F32),
        ],
    )
    return pl.pallas_call(
        _moba_prompt_kernel,
        grid_spec=grid_spec,
        out_shape=jax.ShapeDtypeStruct((b, t, ATT_W), F32),
        compiler_params=_cparams("parallel", "parallel"),
        name="moba_prompt",
    )(slopes, q, k, v)


def _moba_sample_kernel(pt_ref, slopes_ref, q_ref, kn_ref, vn_ref, *rest, n_pages):
    kp_refs = rest[:n_pages]
    vp_refs = rest[n_pages:2 * n_pages]
    o_ref, kb_ref, vb_ref = rest[2 * n_pages:]
    nq = q_ref.shape[0]
    n_past = n_pages * PAGE_SIZE
    nb = n_past // MOBA_BLOCK
    pages_per_block = MOBA_BLOCK // PAGE_SIZE
    rows = LANES
    assert N_HEADS * nq <= rows

    q = q_ref[...]
    r_id = lax.broadcasted_iota(jnp.int32, (rows, ATT_W), 0)
    c_id = lax.broadcasted_iota(jnp.int32, (rows, ATT_W), 1)
    q_rep = jnp.concatenate([q] * (rows // nq), axis=0)
    own = (c_id // HEAD_DIM == r_id // nq) & (r_id < N_HEADS * nq)
    qm = jnp.where(own, q_rep, 0.0)
    qm_hi, qm_lo = _split(qm)
    qs = (qm * (HEAD_DIM ** -0.5)).astype(BF16)

    col = lax.broadcasted_iota(jnp.int32, (1, rows), 1)
    tq = (col % nq).astype(F32)
    slope = jnp.zeros((1, rows), F32)
    for h in range(N_HEADS):
        slope = jnp.where(col // nq == h, slopes_ref[h], slope)

    def page_rows(ref):
        return jnp.concatenate(
            [ref[pl.ds(h, PAGE_SIZE, stride=N_HEADS), :] for h in range(N_HEADS)], axis=1)

    means = []
    for n in range(nb):
        acc = jnp.zeros((1, ATT_W), F32)
        for pg in range(n * pages_per_block, (n + 1) * pages_per_block):
            kf = page_rows(kp_refs[pg])
            acc = acc + jnp.sum(kf, axis=0, keepdims=True)
            kb_ref[pg * PAGE_SIZE:(pg + 1) * PAGE_SIZE, :] = kf.astype(BF16)
            vb_ref[pg * PAGE_SIZE:(pg + 1) * PAGE_SIZE, :] = page_rows(vp_refs[pg]).astype(BF16)
        means.append(acc * (1.0 / MOBA_BLOCK))
    kmean = jnp.concatenate(means, axis=0)
    km_hi, km_lo = _split(kmean)
    gate = _dot_nt(km_hi, qm_hi) + _dot_nt(km_hi, qm_lo) + _dot_nt(km_lo, qm_hi)
    cur = jnp.full((1, rows), nb, jnp.int32)
    selb = _moba_select_bias(gate, cur)

    kpos0 = lax.broadcasted_iota(jnp.int32, (PAGE_SIZE, rows), 0).astype(F32)
    pieces = []
    for pg in range(n_pages):
        s = _dot_nt(kb_ref[pg * PAGE_SIZE:(pg + 1) * PAGE_SIZE, :], qs)
        dist = (n_past - pg * PAGE_SIZE) + tq - kpos0
        s = s - slope * dist + selb[pg // pages_per_block:pg // pages_per_block + 1, :]
        pieces.append(s)
    s_new = _dot_nt(kn_ref[...].astype(BF16), qs)
    dist_new = tq - lax.broadcasted_iota(jnp.int32, (nq, rows), 0).astype(F32)
    s_new = jnp.where(dist_new >= 0.0, s_new - slope * dist_new, NEG_INF)

    m = jnp.max(s_new, axis=0, keepdims=True)
    for s in pieces:
        m = jnp.maximum(m, jnp.max(s, axis=0, keepdims=True))
    p_new = jnp.exp(s_new - m)
    probs = [jnp.exp(s - m) for s in pieces]
    l = jnp.sum(p_new, axis=0, keepdims=True)
    for p in probs:
        l = l + jnp.sum(p, axis=0, keepdims=True)
    inv_l = 1.0 / l
    p_new_t = jnp.concatenate([p_new * inv_l, jnp.zeros((LANES - nq, rows), F32)], axis=0).T
    v_new = jnp.concatenate([vn_ref[...], jnp.zeros((LANES - nq, ATT_W), F32)], axis=0)
    out = _dot(p_new_t.astype(BF16), v_new.astype(BF16))
    for pg in range(n_pages):
        out = out + _dot((probs[pg] * inv_l).T.astype(BF16),
                         vb_ref[pg * PAGE_SIZE:(pg + 1) * PAGE_SIZE, :])

    res = jnp.zeros((nq, ATT_W), F32)
    c_q = lax.broadcasted_iota(jnp.int32, (nq, ATT_W), 1)
    for h in range(N_HEADS):
        res = jnp.where(c_q // HEAD_DIM == h, out[h * nq:(h + 1) * nq, :], res)
    o_ref[...] = res


def _moba_sample(q, k_new, v_new, cache_k, cache_v, page_table, slopes):
    db, nq, _ = q.shape
    n_pages = page_table.shape[1]
    new_spec = pl.BlockSpec((None, nq, ATT_W), lambda s, *_: (s, 0, 0))

    def page_spec(pg):
        return pl.BlockSpec((None, PAGE_SIZE * N_HEADS, HEAD_DIM), lambda s, pt, sl: (pt[s, pg], 0, 0))

    grid_spec = pltpu.PrefetchScalarGridSpec(
        num_scalar_prefetch=2,
        grid=(db,),
        in_specs=([new_spec] * 3 + [page_spec(pg) for pg in range(n_pages)] * 2),
        out_specs=new_spec,
        scratch_shapes=[pltpu.VMEM((n_pages * PAGE_SIZE, ATT_W), BF16)] * 2,
    )
    ck = cache_k.reshape(cache_k.shape[0], PAGE_SIZE * N_HEADS, HEAD_DIM)
    cv = cache_v.reshape(cache_v.shape[0], PAGE_SIZE * N_HEADS, HEAD_DIM)
    return pl.pallas_call(
        functools.partial(_moba_sample_kernel, n_pages=n_pages),
        grid_spec=grid_spec,
        out_shape=jax.ShapeDtypeStruct((db, nq, ATT_W), F32),
        compiler_params=_cparams("parallel"),
        name="moba_sample",
    )(page_table, slopes, q, k_new, v_new, *([ck] * n_pages), *([cv] * n_pages))


S5_CHUNK = 256


def _s5_prep_kernel(are_ref, aim_ref, ldt_ref, bre_ref, bim_ref,
                    bbar_ref, apr_ref, api_ref):
    ar, ai = are_ref[...], aim_ref[...]
    dt = jnp.exp(ldt_ref[...])
    mag = jnp.exp(dt * ar)
    ang = dt * ai
    abr, abi = mag * jnp.cos(ang), mag * jnp.sin(ang)
    den = ar * ar + ai * ai
    nr, ni = abr - 1.0, abi
    f_re = (nr * ar + ni * ai) / den
    f_im = (ni * ar - nr * ai) / den
    bre, bim = bre_ref[...], bim_ref[...]
    bbar_ref[:, :SSM_FLAT] = (f_re * bre - f_im * bim).astype(BF16)
    bbar_ref[:, SSM_FLAT:] = (f_re * bim + f_im * bre).astype(BF16)
    apr_ref[0:1, :] = abr
    api_ref[0:1, :] = abi
    d = 1
    while d < S5_CHUNK:
        pr, pi = apr_ref[0:d, :], api_ref[0:d, :]
        er, ei = apr_ref[d - 1:d, :], api_ref[d - 1:d, :]
        apr_ref[d:2 * d, :] = pr * er - pi * ei
        api_ref[d:2 * d, :] = pr * ei + pi * er
        d *= 2


def _s5_prep(a_re, a_im, log_dt, b_re, b_im):
    eye = jnp.eye(SSM_GROUPS, dtype=F32)
    blockdiag = lambda b: jnp.einsum('gpc,gh->gchp', b, eye).reshape(SSM_W, SSM_FLAT)
    flat = lambda a: a.reshape(1, SSM_FLAT)
    ldt = jnp.broadcast_to(log_dt[:, None], (SSM_GROUPS, SSM_STATE))
    return pl.pallas_call(
        _s5_prep_kernel,
        out_shape=[jax.ShapeDtypeStruct((SSM_W, 2 * SSM_FLAT), BF16),
                   jax.ShapeDtypeStruct((S5_CHUNK, SSM_FLAT), F32),
                   jax.ShapeDtypeStruct((S5_CHUNK, SSM_FLAT), F32)],
        compiler_params=pltpu.CompilerParams(vmem_limit_bytes=VMEM_LIMIT),
        name="s5_prep",
    )(flat(a_re), flat(a_im), flat(ldt), blockdiag(b_re), blockdiag(b_im))


def _s5_readout(h_re, h_im, u, ccat_ref, d_ref, wglu_ref, bglu_ref):
    hcat = jnp.concatenate([h_re.astype(BF16), h_im.astype(BF16)], axis=1)
    y = _dot(hcat, ccat_ref[...]) + d_ref[...] * u
    z = _gelu(y)
    return z * jax.nn.sigmoid(_dot(z.astype(BF16), wglu_ref[...]) + bglu_ref[...])


def _s5_prompt_kernel(u_ref, bbar_ref, apr_ref, api_ref, ccat_ref, d_ref, wglu_ref, bglu_ref,
                      y_ref, hre_ref, him_ref, cr_ref, ci_ref):
    c = pl.program_id(1)
    n = u_ref.shape[0]

    @pl.when(c == 0)
    def _():
        cr_ref[...] = jnp.zeros_like(cr_ref)
        ci_ref[...] = jnp.zeros_like(ci_ref)

    u = u_ref[...]
    bu = _dot(u.astype(BF16), bbar_ref[...])
    hr, hi = bu[:, :SSM_FLAT], bu[:, SSM_FLAT:]
    row = lax.broadcasted_iota(jnp.int32, (n, SSM_FLAT), 0)
    d = 1
    while d < n:
        er, ei = apr_ref[d - 1:d, :], api_ref[d - 1:d, :]
        keep = row >= d
        sr = jnp.where(keep, pltpu.roll(hr, d, axis=0), 0.0)
        si = jnp.where(keep, pltpu.roll(hi, d, axis=0), 0.0)
        hr, hi = hr + er * sr - ei * si, hi + er * si + ei * sr
        d *= 2
    cr, ci = cr_ref[...], ci_ref[...]
    pr, pi = apr_ref[...], api_ref[...]
    hr = hr + pr * cr - pi * ci
    hi = hi + pr * ci + pi * cr
    cr_ref[...] = hr[n - 1:n, :]
    ci_ref[...] = hi[n - 1:n, :]
    y_ref[...] = _s5_readout(hr, hi, u, ccat_ref, d_ref, wglu_ref, bglu_ref)
    hre_ref[...] = hr[n - 1:n, :]
    him_ref[...] = hi[n - 1:n, :]


def _s5_prompt(u, bbar, apr, api, ccat, d_row, wglu_bf, bglu_row):
    b, t, _ = u.shape
    n = S5_CHUNK
    full = lambda a: pl.BlockSpec(a.shape, lambda bi, c: (0,) * a.ndim)
    tok = pl.BlockSpec((None, n, SSM_W), lambda bi, c: (bi, c, 0))
    st = pl.BlockSpec((None, 1, SSM_FLAT), lambda bi, c: (bi, 0, 0))
    y, hre, him = pl.pallas_call(
        _s5_prompt_kernel,
        grid=(b, t // n),
        in_specs=[tok, full(bbar), full(apr), full(api), full(ccat), full(d_row),
                  full(wglu_bf), full(bglu_row)],
        out_specs=[tok, st, st],
        out_shape=[jax.ShapeDtypeStruct((b, t, SSM_W), F32),
                   jax.ShapeDtypeStruct((b, 1, SSM_FLAT), F32),
                   jax.ShapeDtypeStruct((b, 1, SSM_FLAT), F32)],
        scratch_shapes=[pltpu.VMEM((1, SSM_FLAT), F32), pltpu.VMEM((1, SSM_FLAT), F32)],
        compiler_params=_cparams("parallel", "arbitrary"),
        name="s5_prompt",
    )(u, bbar, apr, api, ccat, d_row, wglu_bf, bglu_row)
    return y, hre.reshape(b, SSM_GROUPS, SSM_STATE), him.reshape(b, SSM_GROUPS, SSM_STATE)


def _s5_sample_kernel(u_ref, h0r_ref, h0i_ref, bbar_ref, apr_ref, api_ref, ccat_ref, d_ref,
                      wglu_ref, bglu_ref, y_ref, hre_ref, him_ref, *, n_steps):
    hr, hi = h0r_ref[...], h0i_ref[...]
    ar, ai = apr_ref[0:1, :], api_ref[0:1, :]
    for t in range(n_steps):
        u = u_ref[:, t * SSM_W:(t + 1) * SSM_W]
        bu = _dot(u.astype(BF16), bbar_ref[...])
        hr, hi = (ar * hr - ai * hi + bu[:, :SSM_FLAT],
                  ar * hi + ai * hr + bu[:, SSM_FLAT:])
        y_ref[:, t * SSM_W:(t + 1) * SSM_W] = _s5_readout(
            hr, hi, u, ccat_ref, d_ref, wglu_ref, bglu_ref)
    hre_ref[...] = hr
    him_ref[...] = hi


def _s5_sample(u, h0_re, h0_im, bbar, apr, api, ccat, d_row, wglu_bf, bglu_row):
    db, t, _ = u.shape
    y, hre, him = pl.pallas_call(
        functools.partial(_s5_sample_kernel, n_steps=t),
        out_shape=[jax.ShapeDtypeStruct((db, t * SSM_W), F32),
                   jax.ShapeDtypeStruct((db, SSM_FLAT), F32),
                   jax.ShapeDtypeStruct((db, SSM_FLAT), F32)],
        compiler_params=pltpu.CompilerParams(vmem_limit_bytes=VMEM_LIMIT),
        name="s5_sample",
    )(u.reshape(db, t * SSM_W), h0_re.reshape(db, SSM_FLAT), h0_im.reshape(db, SSM_FLAT),
      bbar, apr, api, ccat, d_row, wglu_bf, bglu_row)
    return (y.reshape(db, t, SSM_W), hre.reshape(db, SSM_GROUPS, SSM_STATE),
            him.reshape(db, SSM_GROUPS, SSM_STATE))


MERGE_TM = 512


def _merge_kernel(x_ref, att_ref, ssm_ref, g_ref, wa_ref, wb_ref, wo_ref, lg_ref, lb_ref, o_ref):
    ya = _dot(att_ref[...].astype(BF16), wa_ref[...])
    yb = _dot(ssm_ref[...].astype(BF16), wb_ref[...])
    merged = g_ref[:, :D_MODEL] * ya + g_ref[:, D_MODEL:] * yb
    y = DN_ALPHA * x_ref[...] + _dot(merged.astype(BF16), wo_ref[...])
    o_ref[...] = _layer_norm(y, lg_ref[...], lb_ref[...])


def _merge(x, att, ssm, gates, wa_bf, wb_bf, wo_bf, lg_row, lb_row):
    n = x.shape[0]
    tm = MERGE_TM
    row = lambda w: pl.BlockSpec((tm, w), lambda i: (i, 0))
    full = lambda a: pl.BlockSpec(a.shape, lambda i: (0,) * a.ndim)
    return pl.pallas_call(
        _merge_kernel,
        grid=(n // tm,),
        in_specs=[row(D_MODEL), row(ATT_W), row(SSM_W), row(2 * D_MODEL),
                  full(wa_bf), full(wb_bf), full(wo_bf), full(lg_row), full(lb_row)],
        out_specs=row(D_MODEL),
        out_shape=jax.ShapeDtypeStruct((n, D_MODEL), F32),
        compiler_params=_cparams("parallel"),
        name="merge",
    )(x, att, ssm, gates, wa_bf, wb_bf, wo_bf, lg_row, lb_row)


PEER_TS = 512
PEER_EC = 1024
PEER_NCH = PEER_EXPERTS // PEER_EC
PEER_ROWS_PER_STEP = PEER_EC // PEER_NKEYS
HALF_KEY = PEER_DKEY // 2
BF16_ROWS = 16
SUBLANES = 8


def _top_values(s, k, want_rank=False):
    vals = []
    rank = jnp.full(s.shape, float(k), F32) if want_rank else None
    for r in range(k):
        m = jnp.max(s, axis=0, keepdims=True)
        vals.append(m)
        hit = s == m
        if want_rank:
            rank = jnp.where(hit, float(r), rank)
        if r + 1 < k:
            s = jnp.where(hit, NEG_INF, s)
    return (vals, rank) if want_rank else vals


def _peer_tables(s1, s2, v1_ref, v2_ref):
    ts = s1.shape[1]
    k = PEER_TOPK
    v1 = _top_values(s1, k)
    v2, rank2 = _top_values(s2, k, want_rank=True)
    for r in range(k):
        v1_ref[r:r + 1, :] = v1[r]
        v2_ref[r:r + 1, :] = v2[r]
    v1_all, v2_all = v1_ref[...], v2_ref[...]
    row = lax.broadcasted_iota(jnp.int32, (SUBLANES, ts), 0)
    groups = [v1[0] + v2_all[:SUBLANES], v1[0] + v2_all[SUBLANES:], v1[1] + v2_all[:SUBLANES]]
    for a in range(2, SUBLANES):
        groups.append(jnp.where(row < k // (a + 1), v1[a] + v2_all[:SUBLANES], NEG_INF))
    groups.append(v1_all[SUBLANES:] + v2[0])
    sc = _top_values(jnp.concatenate(groups, axis=0), k)
    tau = sc[k - 1]
    z = jnp.zeros_like(tau)
    for r in range(k):
        z = z + jnp.exp(sc[r] - sc[0])
    count = jnp.zeros(s1.shape, F32)
    for b in range(k):
        count = count + jnp.where(s1 + v2[b] >= tau, 1.0, 0.0)
    row_weight = jnp.exp(s1 - v1[0]) / z
    return rank2, jnp.exp(s2 - v2[0]), count, row_weight


def _peer_kernel(x_ref, wq_ref, k1_ref, k2_ref,
                 u0_ref, un_ref, vp_ref, vl_ref, lg_ref, lb_ref, o_ref,
                 xb_ref, s_ref, r2_ref, p2_ref, cnt_ref, cw_ref, acc_ref, st_ref, ht_ref,
                 v1_ref, v2_ref):
    c = pl.program_id(1)
    ts = x_ref.shape[0]
    groups = PEER_NKEYS // BF16_ROWS
    cur = c % 2
    nxt = 1 - cur

    @pl.when(c == 0)
    def _prologue():
        xb = x_ref[...].astype(BF16)
        xb_ref[...] = xb
        st_ref[0] = _dot_nt(u0_ref[...], xb)
        qt = _dot_nt(wq_ref[...], xb).astype(BF16)
        for h in range(PEER_HEADS):
            base = h * PEER_DKEY
            s_ref[0] = _dot(k1_ref[h], qt[base:base + HALF_KEY, :])
            s_ref[1] = _dot(k2_ref[h], qt[base + HALF_KEY:base + PEER_DKEY, :])

            def lane_tile(lt, _):
                cols = pl.ds(pl.multiple_of(lt * LANES, LANES), LANES)
                rank2, p2, count, row_weight = _peer_tables(
                    s_ref[0, :, cols], s_ref[1, :, cols], v1_ref, v2_ref)
                r2_ref[h, :, :, cols] = rank2.astype(BF16).reshape(groups, BF16_ROWS, LANES)
                p2_ref[h, :, :, cols] = p2.astype(BF16).reshape(groups, BF16_ROWS, LANES)
                cnt_ref[h, :, cols] = count
                cw_ref[h, :, cols] = row_weight
                return 0

            lax.fori_loop(0, ts // LANES, lane_tile, 0)
        acc_ref[...] = jnp.zeros_like(acc_ref)
        ht_ref[1] = jnp.zeros((PEER_EC, ts), BF16)

    st_ref[nxt] = _dot_nt(un_ref[...], xb_ref[...])
    acc_ref[...] += _dot(vp_ref[...], ht_ref[nxt])
    i1_base = pl.multiple_of(c * PEER_ROWS_PER_STEP, PEER_ROWS_PER_STEP)
    cnts = [cnt_ref[h, pl.ds(i1_base, PEER_ROWS_PER_STEP), :] for h in range(PEER_HEADS)]
    cws = [cw_ref[h, pl.ds(i1_base, PEER_ROWS_PER_STEP), :] for h in range(PEER_HEADS)]
    for j in range(PEER_ROWS_PER_STEP):
        rows = slice(j * PEER_NKEYS, (j + 1) * PEER_NKEYS)
        w = jnp.zeros((groups, BF16_ROWS, ts), BF16)
        for h in range(PEER_HEADS):
            cnt = jnp.broadcast_to(cnts[h][j:j + 1, :], (BF16_ROWS, ts)).astype(BF16)
            cw = jnp.broadcast_to(cws[h][j:j + 1, :], (BF16_ROWS, ts)).astype(BF16)
            w = w + jnp.where(r2_ref[h] < cnt[None], cw[None], 0.0) * p2_ref[h]
        a = _gelu(st_ref[cur, rows, :].astype(BF16))
        ht_ref[cur, rows, :] = w.reshape(PEER_NKEYS, ts) * a

    @pl.when(c == PEER_NCH - 1)
    def _epilogue():
        acc = acc_ref[...] + _dot(vl_ref[...], ht_ref[cur])
        y = DN_ALPHA * x_ref[...] + acc.T
        o_ref[...] = _layer_norm(y, lg_ref[...], lb_ref[...])


def _peer(x1, wq_bf, k1_bf, k2_bf, u_bf, vt_bf, lg_row, lb_row):
    n = x1.shape[0]
    ts = PEER_TS
    last = PEER_NCH - 1
    full = lambda a: pl.BlockSpec(a.shape, lambda i, c: (0,) * a.ndim)
    tok = pl.BlockSpec((ts, D_MODEL), lambda i, c: (i, 0))
    tables = pltpu.VMEM((PEER_HEADS, PEER_NKEYS, ts), F32)
    packed = pltpu.VMEM((PEER_HEADS, PEER_NKEYS // BF16_ROWS, BF16_ROWS, ts), BF16)
    return pl.pallas_call(
        _peer_kernel,
        grid=(n // ts, PEER_NCH),
        in_specs=[tok, full(wq_bf), full(k1_bf), full(k2_bf),
                  pl.BlockSpec((PEER_EC, D_MODEL), lambda i, c: (0, 0)),
                  pl.BlockSpec((PEER_EC, D_MODEL), lambda i, c: (jnp.minimum(c + 1, last), 0)),
                  pl.BlockSpec((D_MODEL, PEER_EC), lambda i, c: (0, jnp.maximum(c - 1, 0))),
                  pl.BlockSpec((D_MODEL, PEER_EC), lambda i, c: (0, last)),
                  full(lg_row), full(lb_row)],
        out_specs=tok,
        out_shape=jax.ShapeDtypeStruct((n, D_MODEL), F32),
        scratch_shapes=[
            pltpu.VMEM((ts, D_MODEL), BF16),
            pltpu.VMEM((2, PEER_NKEYS, ts), F32),
            packed, packed,
            tables, tables,
            pltpu.VMEM((D_MODEL, ts), F32),
            pltpu.VMEM((2, PEER_EC, ts), F32),
            pltpu.VMEM((2, PEER_EC, ts), BF16),
            pltpu.VMEM((PEER_TOPK, LANES), F32),
            pltpu.VMEM((PEER_TOPK, LANES), F32),
        ],
        compiler_params=_cparams("parallel", "arbitrary"),
        name="peer",
    )(x1, wq_bf, k1_bf, k2_bf, u_bf, u_bf, vt_bf, vt_bf, lg_row, lb_row)


def kernel(x_prompt, x_sample, cache_k, cache_v, state_ssm_re, state_ssm_im, page_table, w_in, b_in, w_a, w_b, w_o, ln1_g, ln1_b, a_re, a_im, log_dt, b_re, b_im, c_re, c_im, d_skip, w_glu, b_glu, ln2_g, ln2_b, w_pq, sub_k1, sub_k2, peer_u, peer_v):
    bn, t, _ = x_prompt.shape
    db, dt_, _ = x_sample.shape
    row = lambda a: a.reshape(1, -1).astype(F32)
    slopes = jnp.exp2(-8.0 * (jnp.arange(N_HEADS, dtype=F32) + 1.0) / N_HEADS)

    w_in_bf = w_in.astype(BF16)
    wa_bf, wb_bf, wo_bf, wglu_bf = (w.astype(BF16) for w in (w_a, w_b, w_o, w_glu))
    eye = jnp.eye(SSM_GROUPS, dtype=F32)
    c_blk = lambda cm: jnp.einsum('gcp,gh->gphc', cm, eye).reshape(SSM_FLAT, SSM_W)
    ccat = jnp.concatenate([c_blk(c_re), -c_blk(c_im)], axis=0).astype(BF16)
    peer_w = (w_pq.T.astype(BF16), sub_k1.astype(BF16), sub_k2.astype(BF16))
    u_bf = peer_u.astype(BF16)
    vt_bf = peer_v.T.astype(BF16)

    bbar, apr, api = _s5_prep(a_re, a_im, log_dt, b_re, b_im)
    s5_w = (bbar, apr, api, ccat, row(d_skip), wglu_bf, row(b_glu))

    xp = x_prompt.reshape(bn * t, D_MODEL)
    q, k, v, u, gates = _proj(xp, w_in_bf, row(b_in))
    att = _moba_prompt(q.reshape(bn, t, ATT_W), k.reshape(bn, t, ATT_W), v.reshape(bn, t, ATT_W), slopes)
    ssm, hre_p, him_p = _s5_prompt(u.reshape(bn, t, SSM_W), *s5_w)
    x1 = _merge(xp, att.reshape(bn * t, ATT_W), ssm.reshape(bn * t, SSM_W), gates,
                wa_bf, wb_bf, wo_bf, row(ln1_g), row(ln1_b))
    y_prompt = _peer(x1, *peer_w, u_bf, vt_bf, row(ln2_g), row(ln2_b)).reshape(bn, t, D_MODEL)
    k_prompt = k.reshape(bn, t, N_HEADS, HEAD_DIM)
    v_prompt = v.reshape(bn, t, N_HEADS, HEAD_DIM)

    xs = x_sample.reshape(db * dt_, D_MODEL)
    q, k, v, u, gates = _proj(xs, w_in_bf, row(b_in))
    att = _moba_sample(q.reshape(db, dt_, ATT_W), k.reshape(db, dt_, ATT_W), v.reshape(db, dt_, ATT_W),
                       cache_k, cache_v, page_table, slopes)
    ssm, hre_s, him_s = _s5_sample(u.reshape(db, dt_, SSM_W), state_ssm_re, state_ssm_im, *s5_w)
    x1 = _merge(xs, att.reshape(db * dt_, ATT_W), ssm.reshape(db * dt_, SSM_W), gates,
                wa_bf, wb_bf, wo_bf, row(ln1_g), row(ln1_b))
    y_sample = _peer(x1, *peer_w, u_bf, vt_bf, row(ln2_g), row(ln2_b)).reshape(db, dt_, D_MODEL)
    k_sample = k.reshape(db, dt_, N_HEADS, HEAD_DIM)
    v_sample = v.reshape(db, dt_, N_HEADS, HEAD_DIM)

    return (y_prompt, y_sample, k_prompt, v_prompt, k_sample, v_sample,
            hre_p, him_p, hre_s, him_s)
```

```python
import functools
import math

import jax
import jax.numpy as jnp
from jax import lax
from jax.experimental import pallas as pl
from jax.experimental.pallas import tpu as pltpu

F32 = jnp.float32
BF16 = jnp.bfloat16

D_MODEL = 1024
ATT_W = 512
HEAD_DIM = 64
N_HEADS = 8
MOBA_BLOCK = 256
MOBA_TOPK = 3
SSM_W = 512
SSM_GROUP = 16
SSM_GROUPS = 32
SSM_STATE = 64
SSM_FLAT = SSM_GROUPS * SSM_STATE
PEER_HEADS = 8
PEER_NKEYS = 128
PEER_EXPERTS = PEER_NKEYS * PEER_NKEYS
PEER_DKEY = 256
PEER_TOPK = 16
PROJ_W = 3 * ATT_W + SSM_W + 2 * D_MODEL
DEPTH = 1
DN_ALPHA = (2.0 * DEPTH) ** 0.25
LN_EPS = 1e-5
PAGE_SIZE = 128

LANES = 128
NEG_INF = float("-inf")
VMEM_LIMIT = 56 * 1024 * 1024

NT_DIMS = (((1,), (1,)), ((), ()))


def _cparams(*sem):
    return pltpu.CompilerParams(dimension_semantics=sem, vmem_limit_bytes=VMEM_LIMIT)


def _dot(a, b):
    return jnp.dot(a, b, preferred_element_type=F32)


def _dot_nt(a, b):
    return lax.dot_general(a, b, NT_DIMS, preferred_element_type=F32)


def _split(x):
    hi = x.astype(BF16)
    lo = (x - hi.astype(F32)).astype(BF16)
    return hi, lo


def _gelu(x):
    c = math.sqrt(2.0 / math.pi)
    return 0.5 * x * (1.0 + jnp.tanh(c * (x + 0.044715 * (x * x * x))))


def _layer_norm(y, g, b):
    mu = jnp.mean(y, axis=-1, keepdims=True)
    yc = y - mu
    var = jnp.mean(yc * yc, axis=-1, keepdims=True)
    return yc * lax.rsqrt(var + LN_EPS) * g + b


PROJ_TM = 512


def _proj_kernel(x_ref, w_ref, b_ref, q_ref, k_ref, v_ref, u_ref, g_ref):
    xb = x_ref[...].astype(BF16)

    def seg(lo, hi):
        return _dot(xb, w_ref[:, lo:hi]) + b_ref[:, lo:hi]

    q_ref[...] = seg(0, ATT_W)
    k_ref[...] = seg(ATT_W, 2 * ATT_W)
    v_ref[...] = seg(2 * ATT_W, 3 * ATT_W)
    u_ref[...] = seg(3 * ATT_W, 3 * ATT_W + SSM_W)
    g_ref[...] = jax.nn.sigmoid(seg(3 * ATT_W + SSM_W, PROJ_W))


def _proj(x, w_bf, b_row):
    n = x.shape[0]
    tm = PROJ_TM
    row = lambda w: pl.BlockSpec((tm, w), lambda i: (i, 0))
    full = lambda a: pl.BlockSpec(a.shape, lambda i: (0,) * a.ndim)
    return pl.pallas_call(
        _proj_kernel,
        grid=(n // tm,),
        in_specs=[row(D_MODEL), full(w_bf), full(b_row)],
        out_specs=[row(ATT_W), row(ATT_W), row(ATT_W), row(SSM_W), row(2 * D_MODEL)],
        out_shape=[jax.ShapeDtypeStruct((n, w), F32)
                   for w in (ATT_W, ATT_W, ATT_W, SSM_W, 2 * D_MODEL)],
        compiler_params=_cparams("parallel"),
        name="proj",
    )(x, w_bf, b_row)


def _moba_select_bias(gates, cur):
    nb = len(gates)
    g = [jnp.where(n < cur, gates[n], NEG_INF) for n in range(nb)]
    bias = []
    for n in range(nb):
        rank = jnp.zeros(gates[n].shape, jnp.int32)
        for m in range(nb):
            if m != n:
                ahead = (g[m] >= g[n]) if m < n else (g[m] > g[n])
                rank = rank + ahead.astype(jnp.int32)
        sel = (n < cur) & (rank < MOBA_TOPK)
        bias.append(jnp.where(sel, 0.0, NEG_INF))
    return bias


def _moba_prompt_kernel(slopes_ref, q_ref, k_ref, v_ref, o_ref,
                        kb_ref, vt_ref, ot_ref):
    t = q_ref.shape[0]
    nb = t // MOBA_BLOCK
    blk = MOBA_BLOCK
    hp = pl.program_id(1)
    lane = lax.broadcasted_iota(jnp.int32, (1, LANES), 1)

    k = k_ref[...]
    kb_ref[...] = k.astype(BF16)
    vt_ref[...] = v_ref[...].T.astype(BF16)
    kmean = jnp.mean(k.reshape(nb, blk, LANES), axis=1)
    q_all = q_ref[...]
    q_hi, q_lo = _split(q_all)
    cur = lax.broadcasted_iota(jnp.int32, (1, t), 1) // blk
    d0 = (lax.broadcasted_iota(jnp.int32, (blk, blk), 1)
          - lax.broadcasted_iota(jnp.int32, (blk, blk), 0)).astype(F32)

    for hh in range(2):
        hmask = (lane >= HEAD_DIM * hh) & (lane < HEAD_DIM * (hh + 1))
        slope = slopes_ref[2 * hp + hh]
        km_hi, km_lo = _split(jnp.where(hmask, kmean, 0.0))
        gate = _dot_nt(km_hi, q_hi) + _dot_nt(km_hi, q_lo) + _dot_nt(km_lo, q_hi)
        selb = _moba_select_bias([gate[n:n + 1, :] for n in range(nb)], cur)
        slope_d0 = slope * d0
        rows = slice(HEAD_DIM * hh, HEAD_DIM * (hh + 1))

        for i in range(nb):
            qcols = slice(i * blk, (i + 1) * blk)
            qi = jnp.where(hmask, q_ref[qcols, :] * (HEAD_DIM ** -0.5), 0.0).astype(BF16)
            n_keys = (i + 1) * blk
            s_all = _dot_nt(kb_ref[0:n_keys, :], qi)
            tiles = []
            for j in range(i + 1):
                s = s_all[j * blk:(j + 1) * blk, :] - (slope_d0 + slope * float((i - j) * blk))
                if j < i:
                    s = s + selb[j][:, qcols]
                else:
                    s = jnp.where(d0 >= 0.0, s, NEG_INF)
                tiles.append(s)
            m = jnp.max(tiles[0], axis=0, keepdims=True)
            for s in tiles[1:]:
                m = jnp.maximum(m, jnp.max(s, axis=0, keepdims=True))
            probs = [jnp.exp(s - m) for s in tiles]
            l = jnp.sum(probs[0], axis=0, keepdims=True)
            for p in probs[1:]:
                l = l + jnp.sum(p, axis=0, keepdims=True)
            p_all = jnp.concatenate([p.astype(BF16) for p in probs], axis=0)
            res = _dot(vt_ref[:, 0:n_keys], p_all) / l
            ot_ref[rows, qcols] = res[rows, :]

    o_ref[...] = ot_ref[...].T


def _moba_prompt(q, k, v, slopes):
    b, t, _ = q.shape
    spec = pl.BlockSpec((None, t, LANES), lambda bi, hp, *_: (bi, 0, hp))
    grid_spec = pltpu.PrefetchScalarGridSpec(
        num_scalar_prefetch=1,
        grid=(b, ATT_W // LANES),
        in_specs=[spec, spec, spec],
        out_specs=spec,
        scratch_shapes=[
            pltpu.VMEM((t, LANES), BF16),
            pltpu.VMEM((LANES, t), BF16),
            pltpu.VMEM((LANES, t), F32),
        ],
    )
    return pl.pallas_call(
        _moba_prompt_kernel,
        grid_spec=grid_spec,
        out_shape=jax.ShapeDtypeStruct((b, t, ATT_W), F32),
        compiler_params=_cparams("parallel", "parallel"),
        name="moba_prompt",
    )(slopes, q, k, v)


def _moba_sample_kernel(pt_ref, slopes_ref, q_ref, kn_ref, vn_ref, *rest, n_pages):
    kt_refs = rest[:n_pages]
    vt_refs = rest[n_pages:2 * n_pages]
    o_ref = rest[2 * n_pages]
    nq = q_ref.shape[0]
    n_past = n_pages * PAGE_SIZE
    nb = n_past // MOBA_BLOCK
    pages_per_block = MOBA_BLOCK // PAGE_SIZE
    rows = LANES
    scale = HEAD_DIM ** -0.5
    assert N_HEADS * nq <= rows and nq <= PAGE_SIZE

    q = q_ref[...]
    r_id = lax.broadcasted_iota(jnp.int32, (rows, ATT_W), 0)
    c_id = lax.broadcasted_iota(jnp.int32, (rows, ATT_W), 1)
    q_rep = jnp.concatenate([q] * (rows // nq), axis=0)
    own = (c_id // HEAD_DIM == r_id // nq) & (r_id < N_HEADS * nq)
    qs = jnp.where(own, q_rep * scale, 0.0).astype(BF16)

    r_col = lax.broadcasted_iota(jnp.int32, (rows, 1), 0)
    tq = (r_col % nq).astype(F32)
    slope = jnp.zeros((rows, 1), F32)
    for h in range(N_HEADS):
        slope = jnp.where(r_col // nq == h, slopes_ref[h], slope)
    lane = lax.broadcasted_iota(jnp.int32, (1, PAGE_SIZE), 1).astype(F32)

    raw = [_dot(qs, kt_refs[pg][...].astype(BF16)) for pg in range(n_pages)]

    gates = []
    for n in range(nb):
        total = raw[n * pages_per_block]
        for pg in range(n * pages_per_block + 1, (n + 1) * pages_per_block):
            total = total + raw[pg]
        gates.append(jnp.sum(total, axis=1, keepdims=True) * (1.0 / (MOBA_BLOCK * scale)))
    bias = _moba_select_bias(gates, nb)

    tiles = []
    for pg in range(n_pages):
        dist = (tq + float(n_past - pg * PAGE_SIZE)) - lane
        tiles.append(raw[pg] - slope * dist + bias[pg // pages_per_block])
    pad_rows = lambda a: jnp.concatenate([a, jnp.zeros((PAGE_SIZE - nq, ATT_W), F32)], axis=0)
    dist_new = tq - lane
    s_new = _dot_nt(qs, pad_rows(kn_ref[...]).astype(BF16))
    tiles.append(jnp.where(dist_new >= 0.0, s_new - slope * dist_new, NEG_INF))

    top = tiles[0]
    for s in tiles[1:]:
        top = jnp.maximum(top, s)
    m = jnp.max(top, axis=1, keepdims=True)
    probs = [jnp.exp(s - m) for s in tiles]
    total = probs[0]
    for p in probs[1:]:
        total = total + p
    inv_l = 1.0 / jnp.sum(total, axis=1, keepdims=True)
    out = _dot((probs[n_pages] * inv_l).astype(BF16), pad_rows(vn_ref[...]).astype(BF16))
    for pg in range(n_pages):
        out = out + _dot_nt((probs[pg] * inv_l).astype(BF16), vt_refs[pg][...].astype(BF16))

    res = jnp.zeros((nq, ATT_W), F32)
    c_q = lax.broadcasted_iota(jnp.int32, (nq, ATT_W), 1)
    for h in range(N_HEADS):
        res = jnp.where(c_q // HEAD_DIM == h, out[h * nq:(h + 1) * nq, :], res)
    o_ref[...] = res


def _moba_sample(q, k_new, v_new, cache_k, cache_v, page_table, slopes):
    db, nq, _ = q.shape
    n_pages = page_table.shape[1]
    new_spec = pl.BlockSpec((None, nq, ATT_W), lambda s, *_: (s, 0, 0))

    def page_spec(pg):
        return pl.BlockSpec((None, ATT_W, PAGE_SIZE), lambda s, pt, sl: (pt[s, pg], 0, 0))

    grid_spec = pltpu.PrefetchScalarGridSpec(
        num_scalar_prefetch=2,
        grid=(db,),
        in_specs=([new_spec] * 3 + [page_spec(pg) for pg in range(n_pages)] * 2),
        out_specs=new_spec,
    )
    by_token = lambda c: jnp.transpose(c, (0, 2, 3, 1)).reshape(c.shape[0], ATT_W, PAGE_SIZE)
    ck, cv = by_token(cache_k), by_token(cache_v)
    return pl.pallas_call(
        functools.partial(_moba_sample_kernel, n_pages=n_pages),
        grid_spec=grid_spec,
        out_shape=jax.ShapeDtypeStruct((db, nq, ATT_W), F32),
        compiler_params=_cparams("parallel"),
        name="moba_sample",
    )(page_table, slopes, q, k_new, v_new, *([ck] * n_pages), *([cv] * n_pages))


S5_CHUNK = 256


def _s5_prep_kernel(are_ref, aim_ref, ldt_ref, bre_ref, bim_ref,
                    bbar_ref, apr_ref, api_ref):
    ar, ai = are_ref[...], aim_ref[...]
    dt = jnp.exp(ldt_ref[...])
    mag = jnp.exp(dt * ar)
    ang = dt * ai
    abr, abi = mag * jnp.cos(ang), mag * jnp.sin(ang)
    den = ar * ar + ai * ai
    nr, ni = abr - 1.0, abi
    f_re = (nr * ar + ni * ai) / den
    f_im = (ni * ar - nr * ai) / den
    bre, bim = bre_ref[...], bim_ref[...]
    bbar_ref[:, :SSM_FLAT] = (f_re * bre - f_im * bim).astype(BF16)
    bbar_ref[:, SSM_FLAT:] = (f_re * bim + f_im * bre).astype(BF16)
    apr_ref[0:1, :] = abr
    api_ref[0:1, :] = abi
    d = 1
    while d < S5_CHUNK:
        pr, pi = apr_ref[0:d, :], api_ref[0:d, :]
        er, ei = apr_ref[d - 1:d, :], api_ref[d - 1:d, :]
        apr_ref[d:2 * d, :] = pr * er - pi * ei
        api_ref[d:2 * d, :] = pr * ei + pi * er
        d *= 2


def _s5_prep(a_re, a_im, log_dt, b_re, b_im):
    eye = jnp.eye(SSM_GROUPS, dtype=F32)
    blockdiag = lambda b: jnp.einsum('gpc,gh->gchp', b, eye).reshape(SSM_W, SSM_FLAT)
    flat = lambda a: a.reshape(1, SSM_FLAT)
    ldt = jnp.broadcast_to(log_dt[:, None], (SSM_GROUPS, SSM_STATE))
    return pl.pallas_call(
        _s5_prep_kernel,
        out_shape=[jax.ShapeDtypeStruct((SSM_W, 2 * SSM_FLAT), BF16),
                   jax.ShapeDtypeStruct((S5_CHUNK, SSM_FLAT), F32),
                   jax.ShapeDtypeStruct((S5_CHUNK, SSM_FLAT), F32)],
        compiler_params=pltpu.CompilerParams(vmem_limit_bytes=VMEM_LIMIT),
        name="s5_prep",
    )(flat(a_re), flat(a_im), flat(ldt), blockdiag(b_re), blockdiag(b_im))


def _s5_readout(h_re, h_im, u, ccat_ref, d_ref, wglu_ref, bglu_ref):
    hcat = jnp.concatenate([h_re.astype(BF16), h_im.astype(BF16)], axis=1)
    y = _dot(hcat, ccat_ref[...]) + d_ref[...] * u
    z = _gelu(y)
    return z * jax.nn.sigmoid(_dot(z.astype(BF16), wglu_ref[...]) + bglu_ref[...])


def _s5_prompt_kernel(u_ref, bbar_ref, apr_ref, api_ref, a8r_ref, a8i_ref, ccat_ref, d_ref,
                      wglu_ref, bglu_ref,
                      y_ref, hre_ref, him_ref, cr_ref, ci_ref, hr_ref, hi_ref, pr_ref, pi_ref):
    c = pl.program_id(1)
    n = u_ref.shape[0]
    n_groups = n // SUBLANES

    @pl.when(c == 0)
    def _():
        cr_ref[...] = jnp.zeros_like(cr_ref)
        ci_ref[...] = jnp.zeros_like(ci_ref)

    def scan_level(hr, hi, pos, d, power):
        er, ei = apr_ref[power - 1:power, :], api_ref[power - 1:power, :]
        keep = pos >= d
        sr = jnp.where(keep, pltpu.roll(hr, d, axis=0), 0.0)
        si = jnp.where(keep, pltpu.roll(hi, d, axis=0), 0.0)
        return hr + er * sr - ei * si, hi + er * si + ei * sr

    u = u_ref[...]
    bu = _dot(u.astype(BF16), bbar_ref[...])
    hr, hi = bu[:, :SSM_FLAT], bu[:, SSM_FLAT:]
    in_group = lax.broadcasted_iota(jnp.int32, (n, SSM_FLAT), 0) % SUBLANES
    d = 1
    while d < SUBLANES:
        hr, hi = scan_level(hr, hi, in_group, d, d)
        d *= 2
    n_lt = SSM_FLAT // LANES
    lt_cols = [slice(l * LANES, (l + 1) * LANES) for l in range(n_lt)]
    for l in range(n_lt):
        hr_ref[l] = hr[:, lt_cols[l]]
        hi_ref[l] = hi[:, lt_cols[l]]

    ends = pl.ds(SUBLANES - 1, n_groups, stride=SUBLANES)
    gr = jnp.concatenate([hr_ref[l, ends, :] for l in range(n_lt)], axis=1)
    gi = jnp.concatenate([hi_ref[l, ends, :] for l in range(n_lt)], axis=1)
    group = lax.broadcasted_iota(jnp.int32, (n_groups, SSM_FLAT), 0)
    d = 1
    while d < n_groups:
        gr, gi = scan_level(gr, gi, group, d, d * SUBLANES)
        d *= 2
    cr, ci = cr_ref[...], ci_ref[...]
    a8r, a8i = a8r_ref[...], a8i_ref[...]
    gr = gr + a8r * cr - a8i * ci
    gi = gi + a8r * ci + a8i * cr
    cr_ref[...] = gr[n_groups - 1:n_groups, :]
    ci_ref[...] = gi[n_groups - 1:n_groups, :]
    hre_ref[...] = gr[n_groups - 1:n_groups, :]
    him_ref[...] = gi[n_groups - 1:n_groups, :]
    first = group == 0
    pr_ref[...] = jnp.where(first, cr, pltpu.roll(gr, 1, axis=0))
    pi_ref[...] = jnp.where(first, ci, pltpu.roll(gi, 1, axis=0))

    a1r, a1i = apr_ref[0:SUBLANES, :], api_ref[0:SUBLANES, :]
    for g in range(n_groups):
        rows = slice(g * SUBLANES, (g + 1) * SUBLANES)
        br = jnp.broadcast_to(pr_ref[g:g + 1, :], (SUBLANES, SSM_FLAT))
        bi = jnp.broadcast_to(pi_ref[g:g + 1, :], (SUBLANES, SSM_FLAT))
        add_r = a1r * br - a1i * bi
        add_i = a1r * bi + a1i * br
        for l in range(n_lt):
            hr_ref[l, rows, :] = hr_ref[l, rows, :] + add_r[:, lt_cols[l]]
            hi_ref[l, rows, :] = hi_ref[l, rows, :] + add_i[:, lt_cols[l]]
    hr = jnp.concatenate([hr_ref[l] for l in range(n_lt)], axis=1)
    hi = jnp.concatenate([hi_ref[l] for l in range(n_lt)], axis=1)
    y_ref[...] = _s5_readout(hr, hi, u, ccat_ref, d_ref, wglu_ref, bglu_ref)


def _s5_prompt(u, bbar, apr, api, ccat, d_row, wglu_bf, bglu_row):
    b, t, _ = u.shape
    n = S5_CHUNK
    full = lambda a: pl.BlockSpec(a.shape, lambda bi, c: (0,) * a.ndim)
    tok = pl.BlockSpec((None, n, SSM_W), lambda bi, c: (bi, c, 0))
    st = pl.BlockSpec((None, 1, SSM_FLAT), lambda bi, c: (bi, 0, 0))
    a8r, a8i = apr[SUBLANES - 1::SUBLANES], api[SUBLANES - 1::SUBLANES]
    lane_major = pltpu.VMEM((SSM_FLAT // LANES, n, LANES), F32)
    y, hre, him = pl.pallas_call(
        _s5_prompt_kernel,
        grid=(b, t // n),
        in_specs=[tok, full(bbar), full(apr), full(api), full(a8r), full(a8i), full(ccat),
                  full(d_row), full(wglu_bf), full(bglu_row)],
        out_specs=[tok, st, st],
        out_shape=[jax.ShapeDtypeStruct((b, t, SSM_W), F32),
                   jax.ShapeDtypeStruct((b, 1, SSM_FLAT), F32),
                   jax.ShapeDtypeStruct((b, 1, SSM_FLAT), F32)],
        scratch_shapes=[pltpu.VMEM((1, SSM_FLAT), F32), pltpu.VMEM((1, SSM_FLAT), F32),
                        lane_major, lane_major,
                        pltpu.VMEM((n // SUBLANES, SSM_FLAT), F32),
                        pltpu.VMEM((n // SUBLANES, SSM_FLAT), F32)],
        compiler_params=_cparams("parallel", "arbitrary"),
        name="s5_prompt",
    )(u, bbar, apr, api, a8r, a8i, ccat, d_row, wglu_bf, bglu_row)
    return y, hre.reshape(b, SSM_GROUPS, SSM_STATE), him.reshape(b, SSM_GROUPS, SSM_STATE)


def _s5_sample_kernel(u_ref, h0r_ref, h0i_ref, bbar_ref, apr_ref, api_ref, ccat_ref, d_ref,
                      wglu_ref, bglu_ref, y_ref, hre_ref, him_ref, *, n_steps):
    hr, hi = h0r_ref[...], h0i_ref[...]
    ar, ai = apr_ref[0:1, :], api_ref[0:1, :]
    for t in range(n_steps):
        u = u_ref[:, t * SSM_W:(t + 1) * SSM_W]
        bu = _dot(u.astype(BF16), bbar_ref[...])
        hr, hi = (ar * hr - ai * hi + bu[:, :SSM_FLAT],
                  ar * hi + ai * hr + bu[:, SSM_FLAT:])
        y_ref[:, t * SSM_W:(t + 1) * SSM_W] = _s5_readout(
            hr, hi, u, ccat_ref, d_ref, wglu_ref, bglu_ref)
    hre_ref[...] = hr
    him_ref[...] = hi


def _s5_sample(u, h0_re, h0_im, bbar, apr, api, ccat, d_row, wglu_bf, bglu_row):
    db, t, _ = u.shape
    y, hre, him = pl.pallas_call(
        functools.partial(_s5_sample_kernel, n_steps=t),
        out_shape=[jax.ShapeDtypeStruct((db, t * SSM_W), F32),
                   jax.ShapeDtypeStruct((db, SSM_FLAT), F32),
                   jax.ShapeDtypeStruct((db, SSM_FLAT), F32)],
        compiler_params=pltpu.CompilerParams(vmem_limit_bytes=VMEM_LIMIT),
        name="s5_sample",
    )(u.reshape(db, t * SSM_W), h0_re.reshape(db, SSM_FLAT), h0_im.reshape(db, SSM_FLAT),
      bbar, apr, api, ccat, d_row, wglu_bf, bglu_row)
    return (y.reshape(db, t, SSM_W), hre.reshape(db, SSM_GROUPS, SSM_STATE),
            him.reshape(db, SSM_GROUPS, SSM_STATE))


MERGE_TM = 512


def _merge_kernel(x_ref, att_ref, ssm_ref, g_ref, wa_ref, wb_ref, wo_ref, lg_ref, lb_ref, o_ref):
    ya = _dot(att_ref[...].astype(BF16), wa_ref[...])
    yb = _dot(ssm_ref[...].astype(BF16), wb_ref[...])
    merged = g_ref[:, :D_MODEL] * ya + g_ref[:, D_MODEL:] * yb
    y = DN_ALPHA * x_ref[...] + _dot(merged.astype(BF16), wo_ref[...])
    o_ref[...] = _layer_norm(y, lg_ref[...], lb_ref[...])


def _merge(x, att, ssm, gates, wa_bf, wb_bf, wo_bf, lg_row, lb_row):
    n = x.shape[0]
    tm = MERGE_TM
    row = lambda w: pl.BlockSpec((tm, w), lambda i: (i, 0))
    full = lambda a: pl.BlockSpec(a.shape, lambda i: (0,) * a.ndim)
    return pl.pallas_call(
        _merge_kernel,
        grid=(n // tm,),
        in_specs=[row(D_MODEL), row(ATT_W), row(SSM_W), row(2 * D_MODEL),
                  full(wa_bf), full(wb_bf), full(wo_bf), full(lg_row), full(lb_row)],
        out_specs=row(D_MODEL),
        out_shape=jax.ShapeDtypeStruct((n, D_MODEL), F32),
        compiler_params=_cparams("parallel"),
        name="merge",
    )(x, att, ssm, gates, wa_bf, wb_bf, wo_bf, lg_row, lb_row)


PEER_TS = 512
PEER_EC = 1024
PEER_NCH = PEER_EXPERTS // PEER_EC
PEER_ROWS_PER_STEP = PEER_EC // PEER_NKEYS
HALF_KEY = PEER_DKEY // 2
BF16_ROWS = 16
SUBLANES = 8


def _sorting_network(n):
    pairs = []
    t = (n - 1).bit_length()
    p = 1 << (t - 1)
    while p > 0:
        q, r, d = 1 << (t - 1), 0, p
        while d > 0:
            pairs.extend((i, i + d) for i in range(n - d) if (i & p) == r)
            d, q, r = q - p, q >> 1, p
        p >>= 1
    return pairs


def _top_sorted(s, k):
    assert s.shape[0] == k * SUBLANES and k & (k - 1) == 0
    v = [s[i * SUBLANES:(i + 1) * SUBLANES, :] for i in range(k)]
    for i, j in _sorting_network(k):
        v[i], v[j] = jnp.maximum(v[i], v[j]), jnp.minimum(v[i], v[j])
    shift = SUBLANES // 2
    while shift:
        other = [pltpu.roll(x, shift, axis=0) for x in v]
        v = [jnp.maximum(v[i], other[k - 1 - i]) for i in range(k)]
        stride = k // 2
        while stride:
            for i in range(k):
                if (i // stride) % 2 == 0:
                    lo, hi = v[i], v[i + stride]
                    v[i], v[i + stride] = jnp.maximum(lo, hi), jnp.minimum(lo, hi)
            stride //= 2
        shift //= 2
    return v


def _peer_tables(s1, s2):
    k = PEER_TOPK
    c = s1.shape[1]
    groups3 = (k, SUBLANES, c)
    v1 = _top_sorted(s1, k)
    v2 = _top_sorted(s2, k)
    s2g = s2.reshape(groups3)
    rank2 = jnp.zeros(groups3, F32)
    for b in range(k):
        rank2 = jnp.where(s2g < v2[b][None], float(b + 1), rank2)
    sub = lax.broadcasted_iota(jnp.int32, (SUBLANES, c), 0)

    def pack(rows):
        out = rows[0]
        for r in range(1, SUBLANES):
            out = jnp.where(sub == r, rows[r], out)
        return out

    v2_lo, v2_hi, v1_hi = pack(v2[:SUBLANES]), pack(v2[SUBLANES:]), pack(v1[SUBLANES:])
    cands = [v1[0] + v2_lo, v1[0] + v2_hi, v1[1] + v2_lo]
    for a in range(2, SUBLANES):
        cands.append(jnp.where(sub < k // (a + 1), v1[a] + v2_lo, NEG_INF))
    cands.append(v1_hi + v2[0])
    cands += [jnp.full((SUBLANES, c), NEG_INF, F32)] * (k - len(cands))
    sc = _top_sorted(jnp.concatenate(cands, axis=0), k)
    tau = sc[k - 1]
    z = jnp.zeros_like(tau)
    for r in range(k):
        z = z + jnp.exp(sc[r] - sc[0])
    s1g = s1.reshape(groups3)
    count = jnp.zeros(groups3, F32)
    for a in range(k):
        n_sel = jnp.zeros_like(tau)
        for b in range(k // (a + 1)):
            n_sel = n_sel + jnp.where(v1[a] + v2[b] >= tau, 1.0, 0.0)
        count = jnp.where(s1g == v1[a][None], n_sel[None], count)
    row_weight = jnp.exp(s1g - v1[0][None]) / z[None]
    p2 = jnp.exp(s2g - v2[0][None])
    flat = lambda a: a.reshape(k * SUBLANES, c)
    return flat(rank2), flat(p2), flat(count), flat(row_weight)


def _peer_kernel(x_ref, wq_ref, k1_ref, k2_ref,
                 u0_ref, un_ref, vp_ref, vl_ref, lg_ref, lb_ref, o_ref,
                 xb_ref, s_ref, r2_ref, p2_ref, cnt_ref, cw_ref, acc_ref, st_ref, ht_ref):
    c = pl.program_id(1)
    ts = x_ref.shape[0]
    groups = PEER_NKEYS // BF16_ROWS
    cur = c % 2
    nxt = 1 - cur

    @pl.when(c == 0)
    def _prologue():
        xb = x_ref[...].astype(BF16)
        xb_ref[...] = xb
        st_ref[0] = _dot_nt(u0_ref[...], xb)
        qt = _dot_nt(wq_ref[...], xb).astype(BF16)
        for h in range(PEER_HEADS):
            base = h * PEER_DKEY
            s_ref[0] = _dot(k1_ref[h], qt[base:base + HALF_KEY, :])
            s_ref[1] = _dot(k2_ref[h], qt[base + HALF_KEY:base + PEER_DKEY, :])

            def lane_tile(lt, _):
                cols = pl.ds(pl.multiple_of(lt * LANES, LANES), LANES)
                rank2, p2, count, row_weight = _peer_tables(s_ref[0, :, cols], s_ref[1, :, cols])
                r2_ref[h, :, :, cols] = rank2.astype(BF16).reshape(groups, BF16_ROWS, LANES)
                p2_ref[h, :, :, cols] = p2.astype(BF16).reshape(groups, BF16_ROWS, LANES)
                cnt_ref[h, :, cols] = count
                cw_ref[h, :, cols] = row_weight
                return 0

            lax.fori_loop(0, ts // LANES, lane_tile, 0)
        acc_ref[...] = jnp.zeros_like(acc_ref)
        ht_ref[1] = jnp.zeros((PEER_EC, ts), BF16)

    st_ref[nxt] = _dot_nt(un_ref[...], xb_ref[...])
    acc_ref[...] += _dot(vp_ref[...], ht_ref[nxt])
    i1_base = pl.multiple_of(c * PEER_ROWS_PER_STEP, PEER_ROWS_PER_STEP)
    cnts = [cnt_ref[h, pl.ds(i1_base, PEER_ROWS_PER_STEP), :] for h in range(PEER_HEADS)]
    cws = [cw_ref[h, pl.ds(i1_base, PEER_ROWS_PER_STEP), :] for h in range(PEER_HEADS)]
    for j in range(PEER_ROWS_PER_STEP):
        rows = slice(j * PEER_NKEYS, (j + 1) * PEER_NKEYS)
        w = jnp.zeros((groups, BF16_ROWS, ts), BF16)
        for h in range(PEER_HEADS):
            cnt = jnp.broadcast_to(cnts[h][j:j + 1, :], (BF16_ROWS, ts)).astype(BF16)
            cw = jnp.broadcast_to(cws[h][j:j + 1, :], (BF16_ROWS, ts)).astype(BF16)
            w = w + jnp.where(r2_ref[h] < cnt[None], cw[None], 0.0) * p2_ref[h]
        a = _gelu(st_ref[cur, rows, :].astype(BF16))
        ht_ref[cur, rows, :] = w.reshape(PEER_NKEYS, ts) * a

    @pl.when(c == PEER_NCH - 1)
    def _epilogue():
        acc = acc_ref[...] + _dot(vl_ref[...], ht_ref[cur])
        y = DN_ALPHA * x_ref[...] + acc.T
        o_ref[...] = _layer_norm(y, lg_ref[...], lb_ref[...])


def _peer(x1, wq_bf, k1_bf, k2_bf, u_bf, vt_bf, lg_row, lb_row):
    n = x1.shape[0]
    ts = PEER_TS
    last = PEER_NCH - 1
    full = lambda a: pl.BlockSpec(a.shape, lambda i, c: (0,) * a.ndim)
    tok = pl.BlockSpec((ts, D_MODEL), lambda i, c: (i, 0))
    tables = pltpu.VMEM((PEER_HEADS, PEER_NKEYS, ts), F32)
    packed = pltpu.VMEM((PEER_HEADS, PEER_NKEYS // BF16_ROWS, BF16_ROWS, ts), BF16)
    return pl.pallas_call(
        _peer_kernel,
        grid=(n // ts, PEER_NCH),
        in_specs=[tok, full(wq_bf), full(k1_bf), full(k2_bf),
                  pl.BlockSpec((PEER_EC, D_MODEL), lambda i, c: (0, 0)),
                  pl.BlockSpec((PEER_EC, D_MODEL), lambda i, c: (jnp.minimum(c + 1, last), 0)),
                  pl.BlockSpec((D_MODEL, PEER_EC), lambda i, c: (0, jnp.maximum(c - 1, 0))),
                  pl.BlockSpec((D_MODEL, PEER_EC), lambda i, c: (0, last)),
                  full(lg_row), full(lb_row)],
        out_specs=tok,
        out_shape=jax.ShapeDtypeStruct((n, D_MODEL), F32),
        scratch_shapes=[
            pltpu.VMEM((ts, D_MODEL), BF16),
            pltpu.VMEM((2, PEER_NKEYS, ts), F32),
            packed, packed,
            tables, tables,
            pltpu.VMEM((D_MODEL, ts), F32),
            pltpu.VMEM((2, PEER_EC, ts), F32),
            pltpu.VMEM((2, PEER_EC, ts), BF16),
        ],
        compiler_params=_cparams("parallel", "arbitrary"),
        name="peer",
    )(x1, wq_bf, k1_bf, k2_bf, u_bf, u_bf, vt_bf, vt_bf, lg_row, lb_row)


def kernel(x_prompt, x_sample, cache_k, cache_v, state_ssm_re, state_ssm_im, page_table, w_in, b_in, w_a, w_b, w_o, ln1_g, ln1_b, a_re, a_im, log_dt, b_re, b_im, c_re, c_im, d_skip, w_glu, b_glu, ln2_g, ln2_b, w_pq, sub_k1, sub_k2, peer_u, peer_v):
    bn, t, _ = x_prompt.shape
    db, dt_, _ = x_sample.shape
    row = lambda a: a.reshape(1, -1).astype(F32)
    slopes = jnp.exp2(-8.0 * (jnp.arange(N_HEADS, dtype=F32) + 1.0) / N_HEADS)

    w_in_bf = w_in.astype(BF16)
    wa_bf, wb_bf, wo_bf, wglu_bf = (w.astype(BF16) for w in (w_a, w_b, w_o, w_glu))
    eye = jnp.eye(SSM_GROUPS, dtype=F32)
    c_blk = lambda cm: jnp.einsum('gcp,gh->gphc', cm, eye).reshape(SSM_FLAT, SSM_W)
    ccat = jnp.concatenate([c_blk(c_re), -c_blk(c_im)], axis=0).astype(BF16)
    peer_w = (w_pq.T.astype(BF16), sub_k1.astype(BF16), sub_k2.astype(BF16))
    u_bf = peer_u.astype(BF16)
    vt_bf = peer_v.T.astype(BF16)

    bbar, apr, api = _s5_prep(a_re, a_im, log_dt, b_re, b_im)
    s5_w = (bbar, apr, api, ccat, row(d_skip), wglu_bf, row(b_glu))

    xp = x_prompt.reshape(bn * t, D_MODEL)
    q, k, v, u, gates = _proj(xp, w_in_bf, row(b_in))
    att = _moba_prompt(q.reshape(bn, t, ATT_W), k.reshape(bn, t, ATT_W), v.reshape(bn, t, ATT_W), slopes)
    ssm, hre_p, him_p = _s5_prompt(u.reshape(bn, t, SSM_W), *s5_w)
    x1 = _merge(xp, att.reshape(bn * t, ATT_W), ssm.reshape(bn * t, SSM_W), gates,
                wa_bf, wb_bf, wo_bf, row(ln1_g), row(ln1_b))
    y_prompt = _peer(x1, *peer_w, u_bf, vt_bf, row(ln2_g), row(ln2_b)).reshape(bn, t, D_MODEL)
    k_prompt = k.reshape(bn, t, N_HEADS, HEAD_DIM)
    v_prompt = v.reshape(bn, t, N_HEADS, HEAD_DIM)

    xs = x_sample.reshape(db * dt_, D_MODEL)
    q, k, v, u, gates = _proj(xs, w_in_bf, row(b_in))
    att = _moba_sample(q.reshape(db, dt_, ATT_W), k.reshape(db, dt_, ATT_W), v.reshape(db, dt_, ATT_W),
                       cache_k, cache_v, page_table, slopes)
    ssm, hre_s, him_s = _s5_sample(u.reshape(db, dt_, SSM_W), state_ssm_re, state_ssm_im, *s5_w)
    x1 = _merge(xs, att.reshape(db * dt_, ATT_W), ssm.reshape(db * dt_, SSM_W), gates,
                wa_bf, wb_bf, wo_bf, row(ln1_g), row(ln1_b))
    y_sample = _peer(x1, *peer_w, u_bf, vt_bf, row(ln2_g), row(ln2_b)).reshape(db, dt_, D_MODEL)
    k_sample = k.reshape(db, dt_, N_HEADS, HEAD_DIM)
    v_sample = v.reshape(db, dt_, N_HEADS, HEAD_DIM)

    return (y_prompt, y_sample, k_prompt, v_prompt, k_sample, v_sample,
            hre_p, him_p, hre_s, him_s)
```

```python
import functools
import math

import jax
import jax.numpy as jnp
from jax import lax
from jax.experimental import pallas as pl
from jax.experimental.pallas import tpu as pltpu

F32 = jnp.float32
BF16 = jnp.bfloat16

D_MODEL = 1024
ATT_W = 512
HEAD_DIM = 64
N_HEADS = 8
MOBA_BLOCK = 256
MOBA_TOPK = 3
SSM_W = 512
SSM_GROUP = 16
SSM_GROUPS = 32
SSM_STATE = 64
SSM_FLAT = SSM_GROUPS * SSM_STATE
PEER_HEADS = 8
PEER_NKEYS = 128
PEER_EXPERTS = PEER_NKEYS * PEER_NKEYS
PEER_DKEY = 256
PEER_TOPK = 16
PROJ_W = 3 * ATT_W + SSM_W + 2 * D_MODEL
DEPTH = 1
DN_ALPHA = (2.0 * DEPTH) ** 0.25
LN_EPS = 1e-5
PAGE_SIZE = 128

LANES = 128
SUBLANES = 8
BF16_ROWS = 16
NEG_INF = float("-inf")
VMEM_LIMIT = 56 * 1024 * 1024

NT_DIMS = (((1,), (1,)), ((), ()))


def _cparams(*sem):
    return pltpu.CompilerParams(dimension_semantics=sem, vmem_limit_bytes=VMEM_LIMIT)


def _dot(a, b):
    return jnp.dot(a, b, preferred_element_type=F32)


def _dot_nt(a, b):
    return lax.dot_general(a, b, NT_DIMS, preferred_element_type=F32)


def _split(x):
    hi = x.astype(BF16)
    lo = (x - hi.astype(F32)).astype(BF16)
    return hi, lo


def _gelu(x):
    c = math.sqrt(2.0 / math.pi)
    return 0.5 * x * (1.0 + jnp.tanh(c * (x + 0.044715 * (x * x * x))))


def _layer_norm(y, g, b):
    mu = jnp.mean(y, axis=-1, keepdims=True)
    yc = y - mu
    var = jnp.mean(yc * yc, axis=-1, keepdims=True)
    return yc * lax.rsqrt(var + LN_EPS) * g + b


PROJ_TM = 512


def _proj_kernel(x_ref, w_ref, b_ref, wkv_t_ref, bkv_col_ref, q_ref, k_ref, v_ref, u_ref, g_ref,
                 *, kv_transposed):
    xb = x_ref[...].astype(BF16)

    def seg(lo, hi):
        return _dot(xb, w_ref[:, lo:hi]) + b_ref[:, lo:hi]

    q_ref[...] = seg(0, ATT_W)
    if kv_transposed:
        kv_t = _dot_nt(wkv_t_ref[...], xb) + bkv_col_ref[...]
        k_ref[...] = kv_t[:ATT_W, :]
        v_ref[...] = kv_t[ATT_W:, :]
    else:
        k_ref[...] = seg(ATT_W, 2 * ATT_W)
        v_ref[...] = seg(2 * ATT_W, 3 * ATT_W)
    u_ref[...] = seg(3 * ATT_W, 3 * ATT_W + SSM_W)
    g_ref[...] = jax.nn.sigmoid(seg(3 * ATT_W + SSM_W, PROJ_W))


def _proj(x, w_bf, b_row, wkv_t, bkv_col, seq_len=None):
    n = x.shape[0]
    tm = PROJ_TM
    row = lambda w: pl.BlockSpec((tm, w), lambda i: (i, 0))
    full = lambda a: pl.BlockSpec(a.shape, lambda i: (0,) * a.ndim)
    if seq_len is None:
        kv_spec, kv_shape = row(ATT_W), jax.ShapeDtypeStruct((n, ATT_W), F32)
    else:
        per_seq = seq_len // tm
        kv_spec = pl.BlockSpec((None, ATT_W, tm), lambda i: (i // per_seq, 0, i % per_seq))
        kv_shape = jax.ShapeDtypeStruct((n // seq_len, ATT_W, seq_len), F32)
    rows = lambda w: jax.ShapeDtypeStruct((n, w), F32)
    return pl.pallas_call(
        functools.partial(_proj_kernel, kv_transposed=seq_len is not None),
        grid=(n // tm,),
        in_specs=[row(D_MODEL), full(w_bf), full(b_row), full(wkv_t), full(bkv_col)],
        out_specs=[row(ATT_W), kv_spec, kv_spec, row(SSM_W), row(2 * D_MODEL)],
        out_shape=[rows(ATT_W), kv_shape, kv_shape, rows(SSM_W), rows(2 * D_MODEL)],
        compiler_params=_cparams("parallel"),
        name="proj",
    )(x, w_bf, b_row, wkv_t, bkv_col)


def _moba_select_bias(gates, cur):
    nb = len(gates)
    g = [jnp.where(n < cur, gates[n], NEG_INF) for n in range(nb)]
    bias = []
    for n in range(nb):
        rank = jnp.zeros(gates[n].shape, jnp.int32)
        for m in range(nb):
            if m != n:
                ahead = (g[m] >= g[n]) if m < n else (g[m] > g[n])
                rank = rank + ahead.astype(jnp.int32)
        sel = (n < cur) & (rank < MOBA_TOPK)
        bias.append(jnp.where(sel, 0.0, NEG_INF))
    return bias


def _moba_prompt_kernel(slopes_ref, q_ref, kt_ref, vt_in_ref, o_ref,
                        kb_ref, vt_ref, ot_ref):
    t = q_ref.shape[0]
    nb = t // MOBA_BLOCK
    blk = MOBA_BLOCK
    hp = pl.program_id(1)
    lane = lax.broadcasted_iota(jnp.int32, (1, LANES), 1)

    k = kt_ref[...].T
    kb_ref[...] = k.astype(BF16)
    vt_ref[...] = vt_in_ref[...].astype(BF16)
    kmean = jnp.mean(k.reshape(nb, blk, LANES), axis=1)
    q_all = q_ref[...]
    q_hi, q_lo = _split(q_all)
    cur = lax.broadcasted_iota(jnp.int32, (1, t), 1) // blk
    d0 = (lax.broadcasted_iota(jnp.int32, (blk, blk), 1)
          - lax.broadcasted_iota(jnp.int32, (blk, blk), 0)).astype(F32)

    for hh in range(2):
        hmask = (lane >= HEAD_DIM * hh) & (lane < HEAD_DIM * (hh + 1))
        slope = slopes_ref[2 * hp + hh]
        km_hi, km_lo = _split(jnp.where(hmask, kmean, 0.0))
        gate = _dot_nt(km_hi, q_hi) + _dot_nt(km_hi, q_lo) + _dot_nt(km_lo, q_hi)
        selb = _moba_select_bias([gate[n:n + 1, :] for n in range(nb)], cur)
        slope_d0 = slope * d0
        rows = slice(HEAD_DIM * hh, HEAD_DIM * (hh + 1))

        for i in range(nb):
            qcols = slice(i * blk, (i + 1) * blk)
            qi = jnp.where(hmask, q_ref[qcols, :] * (HEAD_DIM ** -0.5), 0.0).astype(BF16)
            n_keys = (i + 1) * blk
            s_all = _dot_nt(kb_ref[0:n_keys, :], qi)
            tiles = []
            for j in range(i + 1):
                s = s_all[j * blk:(j + 1) * blk, :] - (slope_d0 + slope * float((i - j) * blk))
                if j < i:
                    s = s + selb[j][:, qcols]
                else:
                    s = jnp.where(d0 >= 0.0, s, NEG_INF)
                tiles.append(s)
            m = jnp.max(tiles[0], axis=0, keepdims=True)
            for s in tiles[1:]:
                m = jnp.maximum(m, jnp.max(s, axis=0, keepdims=True))
            probs = [jnp.exp(s - m) for s in tiles]
            l = jnp.sum(probs[0], axis=0, keepdims=True)
            for p in probs[1:]:
                l = l + jnp.sum(p, axis=0, keepdims=True)
            p_all = jnp.concatenate([p.astype(BF16) for p in probs], axis=0)
            res = _dot(vt_ref[:, 0:n_keys], p_all) / l
            ot_ref[rows, qcols] = res[rows, :]

    o_ref[...] = ot_ref[...].T


def _moba_prompt(q, k_t, v_t, slopes):
    b, t, _ = q.shape
    spec = pl.BlockSpec((None, t, LANES), lambda bi, hp, *_: (bi, 0, hp))
    spec_t = pl.BlockSpec((None, LANES, t), lambda bi, hp, *_: (bi, hp, 0))
    grid_spec = pltpu.PrefetchScalarGridSpec(
        num_scalar_prefetch=1,
        grid=(b, ATT_W // LANES),
        in_specs=[spec, spec_t, spec_t],
        out_specs=spec,
        scratch_shapes=[
            pltpu.VMEM((t, LANES), BF16),
            pltpu.VMEM((LANES, t), BF16),
            pltpu.VMEM((LANES, t), F32),
        ],
    )
    return pl.pallas_call(
        _moba_prompt_kernel,
        grid_spec=grid_spec,
        out_shape=jax.ShapeDtypeStruct((b, t, ATT_W), F32),
        compiler_params=_cparams("parallel", "parallel"),
        name="moba_prompt",
    )(slopes, q, k_t, v_t)


def _moba_sample_kernel(pt_ref, slopes_ref, q_ref, kn_ref, vn_ref, *rest, n_pages):
    kt_refs = rest[:n_pages]
    vt_refs = rest[n_pages:2 * n_pages]
    o_ref = rest[2 * n_pages]
    nq = q_ref.shape[0]
    n_past = n_pages * PAGE_SIZE
    nb = n_past // MOBA_BLOCK
    pages_per_block = MOBA_BLOCK // PAGE_SIZE
    rows = N_HEADS * nq
    scale = HEAD_DIM ** -0.5
    assert rows % SUBLANES == 0 and nq <= PAGE_SIZE

    q = q_ref[...]
    r_id = lax.broadcasted_iota(jnp.int32, (rows, ATT_W), 0)
    c_id = lax.broadcasted_iota(jnp.int32, (rows, ATT_W), 1)
    q_rep = jnp.concatenate([q] * (rows // nq), axis=0)
    own = c_id // HEAD_DIM == r_id // nq
    qs = jnp.where(own, q_rep * scale, 0.0).astype(BF16)

    r_col = lax.broadcasted_iota(jnp.int32, (rows, 1), 0)
    tq = (r_col % nq).astype(F32)
    slope = jnp.zeros((rows, 1), F32)
    for h in range(N_HEADS):
        slope = jnp.where(r_col // nq == h, slopes_ref[h], slope)
    lane = lax.broadcasted_iota(jnp.int32, (1, PAGE_SIZE), 1).astype(F32)

    raw = [_dot(qs, kt_refs[pg][...].astype(BF16)) for pg in range(n_pages)]

    gates = []
    for n in range(nb):
        total = raw[n * pages_per_block]
        for pg in range(n * pages_per_block + 1, (n + 1) * pages_per_block):
            total = total + raw[pg]
        gates.append(jnp.sum(total, axis=1, keepdims=True) * (1.0 / (MOBA_BLOCK * scale)))
    bias = _moba_select_bias(gates, nb)

    tiles = []
    for pg in range(n_pages):
        dist = (tq + float(n_past - pg * PAGE_SIZE)) - lane
        tiles.append(raw[pg] - slope * dist + bias[pg // pages_per_block])
    pad_rows = lambda a: jnp.concatenate([a, jnp.zeros((PAGE_SIZE - nq, ATT_W), F32)], axis=0)
    dist_new = tq - lane
    s_new = _dot_nt(qs, pad_rows(kn_ref[...]).astype(BF16))
    tiles.append(jnp.where(dist_new >= 0.0, s_new - slope * dist_new, NEG_INF))

    top = tiles[0]
    for s in tiles[1:]:
        top = jnp.maximum(top, s)
    m = jnp.max(top, axis=1, keepdims=True)
    probs = [jnp.exp(s - m) for s in tiles]
    total = probs[0]
    for p in probs[1:]:
        total = total + p
    inv_l = 1.0 / jnp.sum(total, axis=1, keepdims=True)
    out = _dot((probs[n_pages] * inv_l).astype(BF16), pad_rows(vn_ref[...]).astype(BF16))
    for pg in range(n_pages):
        out = out + _dot_nt((probs[pg] * inv_l).astype(BF16), vt_refs[pg][...].astype(BF16))

    res = jnp.zeros((nq, ATT_W), F32)
    c_q = lax.broadcasted_iota(jnp.int32, (nq, ATT_W), 1)
    for h in range(N_HEADS):
        res = jnp.where(c_q // HEAD_DIM == h, out[h * nq:(h + 1) * nq, :], res)
    o_ref[...] = res


def _moba_sample(q, k_new, v_new, cache_k, cache_v, page_table, slopes):
    db, nq, _ = q.shape
    n_pages = page_table.shape[1]
    new_spec = pl.BlockSpec((None, nq, ATT_W), lambda s, *_: (s, 0, 0))

    def page_spec(pg):
        return pl.BlockSpec((None, ATT_W, PAGE_SIZE), lambda s, pt, sl: (pt[s, pg], 0, 0))

    grid_spec = pltpu.PrefetchScalarGridSpec(
        num_scalar_prefetch=2,
        grid=(db,),
        in_specs=([new_spec] * 3 + [page_spec(pg) for pg in range(n_pages)] * 2),
        out_specs=new_spec,
    )
    by_token = lambda c: jnp.transpose(c, (0, 2, 3, 1)).reshape(c.shape[0], ATT_W, PAGE_SIZE)
    ck, cv = by_token(cache_k), by_token(cache_v)
    return pl.pallas_call(
        functools.partial(_moba_sample_kernel, n_pages=n_pages),
        grid_spec=grid_spec,
        out_shape=jax.ShapeDtypeStruct((db, nq, ATT_W), F32),
        compiler_params=_cparams("parallel"),
        name="moba_sample",
    )(page_table, slopes, q, k_new, v_new, *([ck] * n_pages), *([cv] * n_pages))


S5_CHUNK = 256


def _s5_prep_kernel(are_ref, aim_ref, ldt_ref, bre_ref, bim_ref,
                    bbar_ref, apr_ref, api_ref):
    ar, ai = are_ref[...], aim_ref[...]
    dt = jnp.exp(ldt_ref[...])
    mag = jnp.exp(dt * ar)
    ang = dt * ai
    abr, abi = mag * jnp.cos(ang), mag * jnp.sin(ang)
    den = ar * ar + ai * ai
    nr, ni = abr - 1.0, abi
    f_re = (nr * ar + ni * ai) / den
    f_im = (ni * ar - nr * ai) / den
    bre, bim = bre_ref[...], bim_ref[...]
    bbar_ref[:, :SSM_FLAT] = (f_re * bre - f_im * bim).astype(BF16)
    bbar_ref[:, SSM_FLAT:] = (f_re * bim + f_im * bre).astype(BF16)
    apr_ref[0:1, :] = abr
    api_ref[0:1, :] = abi
    d = 1
    while d < S5_CHUNK:
        pr, pi = apr_ref[0:d, :], api_ref[0:d, :]
        er, ei = apr_ref[d - 1:d, :], api_ref[d - 1:d, :]
        apr_ref[d:2 * d, :] = pr * er - pi * ei
        api_ref[d:2 * d, :] = pr * ei + pi * er
        d *= 2


def _s5_prep(a_re, a_im, log_dt, b_re, b_im):
    eye = jnp.eye(SSM_GROUPS, dtype=F32)
    blockdiag = lambda b: jnp.einsum('gpc,gh->gchp', b, eye).reshape(SSM_W, SSM_FLAT)
    flat = lambda a: a.reshape(1, SSM_FLAT)
    ldt = jnp.broadcast_to(log_dt[:, None], (SSM_GROUPS, SSM_STATE))
    return pl.pallas_call(
        _s5_prep_kernel,
        out_shape=[jax.ShapeDtypeStruct((SSM_W, 2 * SSM_FLAT), BF16),
                   jax.ShapeDtypeStruct((S5_CHUNK, SSM_FLAT), F32),
                   jax.ShapeDtypeStruct((S5_CHUNK, SSM_FLAT), F32)],
        compiler_params=pltpu.CompilerParams(vmem_limit_bytes=VMEM_LIMIT),
        name="s5_prep",
    )(flat(a_re), flat(a_im), flat(ldt), blockdiag(b_re), blockdiag(b_im))


def _s5_readout(h_re, h_im, u, ccat_ref, d_ref, wglu_ref, bglu_ref):
    hcat = jnp.concatenate([h_re.astype(BF16), h_im.astype(BF16)], axis=1)
    y = _dot(hcat, ccat_ref[...]) + d_ref[...] * u
    z = _gelu(y)
    return z * jax.nn.sigmoid(_dot(z.astype(BF16), wglu_ref[...]) + bglu_ref[...])


def _s5_prompt_kernel(u_ref, bbar_ref, apr_ref, api_ref, a8r_ref, a8i_ref, ccat_ref, d_ref,
                      wglu_ref, bglu_ref,
                      y_ref, hre_ref, him_ref, cr_ref, ci_ref, hr_ref, hi_ref, pr_ref, pi_ref):
    c = pl.program_id(1)
    n = u_ref.shape[0]
    n_groups = n // SUBLANES

    @pl.when(c == 0)
    def _():
        cr_ref[...] = jnp.zeros_like(cr_ref)
        ci_ref[...] = jnp.zeros_like(ci_ref)

    def scan_level(hr, hi, pos, d, power):
        er, ei = apr_ref[power - 1:power, :], api_ref[power - 1:power, :]
        keep = pos >= d
        sr = jnp.where(keep, pltpu.roll(hr, d, axis=0), 0.0)
        si = jnp.where(keep, pltpu.roll(hi, d, axis=0), 0.0)
        return hr + er * sr - ei * si, hi + er * si + ei * sr

    u = u_ref[...]
    bu = _dot(u.astype(BF16), bbar_ref[...])
    hr, hi = bu[:, :SSM_FLAT], bu[:, SSM_FLAT:]
    in_group = lax.broadcasted_iota(jnp.int32, (n, SSM_FLAT), 0) % SUBLANES
    d = 1
    while d < SUBLANES:
        hr, hi = scan_level(hr, hi, in_group, d, d)
        d *= 2
    n_lt = SSM_FLAT // LANES
    lt_cols = [slice(l * LANES, (l + 1) * LANES) for l in range(n_lt)]
    for l in range(n_lt):
        hr_ref[l] = hr[:, lt_cols[l]]
        hi_ref[l] = hi[:, lt_cols[l]]

    ends = pl.ds(SUBLANES - 1, n_groups, stride=SUBLANES)
    gr = jnp.concatenate([hr_ref[l, ends, :] for l in range(n_lt)], axis=1)
    gi = jnp.concatenate([hi_ref[l, ends, :] for l in range(n_lt)], axis=1)
    group = lax.broadcasted_iota(jnp.int32, (n_groups, SSM_FLAT), 0)
    d = 1
    while d < n_groups:
        gr, gi = scan_level(gr, gi, group, d, d * SUBLANES)
        d *= 2
    cr, ci = cr_ref[...], ci_ref[...]
    a8r, a8i = a8r_ref[...], a8i_ref[...]
    gr = gr + a8r * cr - a8i * ci
    gi = gi + a8r * ci + a8i * cr
    cr_ref[...] = gr[n_groups - 1:n_groups, :]
    ci_ref[...] = gi[n_groups - 1:n_groups, :]
    hre_ref[...] = gr[n_groups - 1:n_groups, :]
    him_ref[...] = gi[n_groups - 1:n_groups, :]
    first = group == 0
    pr_ref[...] = jnp.where(first, cr, pltpu.roll(gr, 1, axis=0))
    pi_ref[...] = jnp.where(first, ci, pltpu.roll(gi, 1, axis=0))

    a1r, a1i = apr_ref[0:SUBLANES, :], api_ref[0:SUBLANES, :]
    for g in range(n_groups):
        rows = slice(g * SUBLANES, (g + 1) * SUBLANES)
        br = jnp.broadcast_to(pr_ref[g:g + 1, :], (SUBLANES, SSM_FLAT))
        bi = jnp.broadcast_to(pi_ref[g:g + 1, :], (SUBLANES, SSM_FLAT))
        add_r = a1r * br - a1i * bi
        add_i = a1r * bi + a1i * br
        for l in range(n_lt):
            hr_ref[l, rows, :] = hr_ref[l, rows, :] + add_r[:, lt_cols[l]]
            hi_ref[l, rows, :] = hi_ref[l, rows, :] + add_i[:, lt_cols[l]]
    hr = jnp.concatenate([hr_ref[l] for l in range(n_lt)], axis=1)
    hi = jnp.concatenate([hi_ref[l] for l in range(n_lt)], axis=1)
    y_ref[...] = _s5_readout(hr, hi, u, ccat_ref, d_ref, wglu_ref, bglu_ref)


def _s5_prompt(u, bbar, apr, api, ccat, d_row, wglu_bf, bglu_row):
    b, t, _ = u.shape
    n = S5_CHUNK
    full = lambda a: pl.BlockSpec(a.shape, lambda bi, c: (0,) * a.ndim)
    tok = pl.BlockSpec((None, n, SSM_W), lambda bi, c: (bi, c, 0))
    st = pl.BlockSpec((None, 1, SSM_FLAT), lambda bi, c: (bi, 0, 0))
    a8r, a8i = apr[SUBLANES - 1::SUBLANES], api[SUBLANES - 1::SUBLANES]
    lane_major = pltpu.VMEM((SSM_FLAT // LANES, n, LANES), F32)
    y, hre, him = pl.pallas_call(
        _s5_prompt_kernel,
        grid=(b, t // n),
        in_specs=[tok, full(bbar), full(apr), full(api), full(a8r), full(a8i), full(ccat),
                  full(d_row), full(wglu_bf), full(bglu_row)],
        out_specs=[tok, st, st],
        out_shape=[jax.ShapeDtypeStruct((b, t, SSM_W), F32),
                   jax.ShapeDtypeStruct((b, 1, SSM_FLAT), F32),
                   jax.ShapeDtypeStruct((b, 1, SSM_FLAT), F32)],
        scratch_shapes=[pltpu.VMEM((1, SSM_FLAT), F32), pltpu.VMEM((1, SSM_FLAT), F32),
                        lane_major, lane_major,
                        pltpu.VMEM((n // SUBLANES, SSM_FLAT), F32),
                        pltpu.VMEM((n // SUBLANES, SSM_FLAT), F32)],
        compiler_params=_cparams("parallel", "arbitrary"),
        name="s5_prompt",
    )(u, bbar, apr, api, a8r, a8i, ccat, d_row, wglu_bf, bglu_row)
    return y, hre.reshape(b, SSM_GROUPS, SSM_STATE), him.reshape(b, SSM_GROUPS, SSM_STATE)


def _s5_sample_kernel(u_ref, h0r_ref, h0i_ref, bbar_ref, apr_ref, api_ref, ccat_ref, d_ref,
                      wglu_ref, bglu_ref, y_ref, hre_ref, him_ref, *, n_steps):
    hr, hi = h0r_ref[...], h0i_ref[...]
    ar, ai = apr_ref[0:1, :], api_ref[0:1, :]
    for t in range(n_steps):
        u = u_ref[:, t * SSM_W:(t + 1) * SSM_W]
        bu = _dot(u.astype(BF16), bbar_ref[...])
        hr, hi = (ar * hr - ai * hi + bu[:, :SSM_FLAT],
                  ar * hi + ai * hr + bu[:, SSM_FLAT:])
        y_ref[:, t * SSM_W:(t + 1) * SSM_W] = _s5_readout(
            hr, hi, u, ccat_ref, d_ref, wglu_ref, bglu_ref)
    hre_ref[...] = hr
    him_ref[...] = hi


def _s5_sample(u, h0_re, h0_im, bbar, apr, api, ccat, d_row, wglu_bf, bglu_row):
    db, t, _ = u.shape
    y, hre, him = pl.pallas_call(
        functools.partial(_s5_sample_kernel, n_steps=t),
        out_shape=[jax.ShapeDtypeStruct((db, t * SSM_W), F32),
                   jax.ShapeDtypeStruct((db, SSM_FLAT), F32),
                   jax.ShapeDtypeStruct((db, SSM_FLAT), F32)],
        compiler_params=pltpu.CompilerParams(vmem_limit_bytes=VMEM_LIMIT),
        name="s5_sample",
    )(u.reshape(db, t * SSM_W), h0_re.reshape(db, SSM_FLAT), h0_im.reshape(db, SSM_FLAT),
      bbar, apr, api, ccat, d_row, wglu_bf, bglu_row)
    return (y.reshape(db, t, SSM_W), hre.reshape(db, SSM_GROUPS, SSM_STATE),
            him.reshape(db, SSM_GROUPS, SSM_STATE))


MERGE_TM = 512


def _merge_kernel(x_ref, att_ref, ssm_ref, g_ref, wa_ref, wb_ref, wo_ref, lg_ref, lb_ref, o_ref):
    ya = _dot(att_ref[...].astype(BF16), wa_ref[...])
    yb = _dot(ssm_ref[...].astype(BF16), wb_ref[...])
    merged = g_ref[:, :D_MODEL] * ya + g_ref[:, D_MODEL:] * yb
    y = DN_ALPHA * x_ref[...] + _dot(merged.astype(BF16), wo_ref[...])
    o_ref[...] = _layer_norm(y, lg_ref[...], lb_ref[...])


def _merge(x, att, ssm, gates, wa_bf, wb_bf, wo_bf, lg_row, lb_row):
    n = x.shape[0]
    tm = MERGE_TM
    row = lambda w: pl.BlockSpec((tm, w), lambda i: (i, 0))
    full = lambda a: pl.BlockSpec(a.shape, lambda i: (0,) * a.ndim)
    return pl.pallas_call(
        _merge_kernel,
        grid=(n // tm,),
        in_specs=[row(D_MODEL), row(ATT_W), row(SSM_W), row(2 * D_MODEL),
                  full(wa_bf), full(wb_bf), full(wo_bf), full(lg_row), full(lb_row)],
        out_specs=row(D_MODEL),
        out_shape=jax.ShapeDtypeStruct((n, D_MODEL), F32),
        compiler_params=_cparams("parallel"),
        name="merge",
    )(x, att, ssm, gates, wa_bf, wb_bf, wo_bf, lg_row, lb_row)


PEER_TS = 512
PEER_EC = 1024
PEER_NCH = PEER_EXPERTS // PEER_EC
PEER_ROWS_PER_STEP = PEER_EC // PEER_NKEYS
HALF_KEY = PEER_DKEY // 2


def _sorting_network(n):
    pairs = []
    t = (n - 1).bit_length()
    p = 1 << (t - 1)
    while p > 0:
        q, r, d = 1 << (t - 1), 0, p
        while d > 0:
            pairs.extend((i, i + d) for i in range(n - d) if (i & p) == r)
            d, q, r = q - p, q >> 1, p
        p >>= 1
    return pairs


def _top_sorted(s, k):
    assert s.shape[0] == k * SUBLANES and k & (k - 1) == 0
    v = [s[i * SUBLANES:(i + 1) * SUBLANES, :] for i in range(k)]
    for i, j in _sorting_network(k):
        v[i], v[j] = jnp.maximum(v[i], v[j]), jnp.minimum(v[i], v[j])
    shift = SUBLANES // 2
    while shift:
        other = [pltpu.roll(x, shift, axis=0) for x in v]
        v = [jnp.maximum(v[i], other[k - 1 - i]) for i in range(k)]
        stride = k // 2
        while stride:
            for i in range(k):
                if (i // stride) % 2 == 0:
                    lo, hi = v[i], v[i + stride]
                    v[i], v[i + stride] = jnp.maximum(lo, hi), jnp.minimum(lo, hi)
            stride //= 2
        shift //= 2
    return v


def _peer_tables(s1, s2):
    k = PEER_TOPK
    c = s1.shape[1]
    groups3 = (k, SUBLANES, c)
    v1 = _top_sorted(s1, k)
    v2 = _top_sorted(s2, k)
    s2g = s2.reshape(groups3)
    rank2 = jnp.zeros(groups3, F32)
    for b in range(k):
        rank2 = jnp.where(s2g < v2[b][None], float(b + 1), rank2)
    sub = lax.broadcasted_iota(jnp.int32, (SUBLANES, c), 0)

    def pack(rows):
        out = rows[0]
        for r in range(1, SUBLANES):
            out = jnp.where(sub == r, rows[r], out)
        return out

    v2_lo, v2_hi, v1_hi = pack(v2[:SUBLANES]), pack(v2[SUBLANES:]), pack(v1[SUBLANES:])
    cands = [v1[0] + v2_lo, v1[0] + v2_hi, v1[1] + v2_lo]
    for a in range(2, SUBLANES):
        cands.append(jnp.where(sub < k // (a + 1), v1[a] + v2_lo, NEG_INF))
    cands.append(v1_hi + v2[0])
    cands += [jnp.full((SUBLANES, c), NEG_INF, F32)] * (k - len(cands))
    sc = _top_sorted(jnp.concatenate(cands, axis=0), k)
    tau = sc[k - 1]
    z = jnp.zeros_like(tau)
    for r in range(k):
        z = z + jnp.exp(sc[r] - sc[0])
    s1g = s1.reshape(groups3)
    count = jnp.zeros(groups3, F32)
    for a in range(k):
        n_sel = jnp.zeros_like(tau)
        for b in range(k // (a + 1)):
            n_sel = n_sel + jnp.where(v1[a] + v2[b] >= tau, 1.0, 0.0)
        count = jnp.where(s1g == v1[a][None], n_sel[None], count)
    row_weight = jnp.exp(s1g - v1[0][None]) / z[None]
    p2 = jnp.exp(s2g - v2[0][None])
    flat = lambda a: a.reshape(k * SUBLANES, c)
    return flat(rank2), flat(p2), flat(count), flat(row_weight)


def _peer_kernel(x_ref, wq_ref, k1_ref, k2_ref,
                 u0_ref, un_ref, vp_ref, vl_ref, lg_ref, lb_ref, o_ref,
                 xb_ref, s_ref, r2_ref, p2_ref, cnt_ref, cw_ref, acc_ref, st_ref, ht_ref):
    c = pl.program_id(1)
    ts = x_ref.shape[0]
    groups = PEER_NKEYS // BF16_ROWS
    cur = c % 2
    nxt = 1 - cur

    @pl.when(c == 0)
    def _prologue():
        xb = x_ref[...].astype(BF16)
        xb_ref[...] = xb
        st_ref[0] = _dot_nt(u0_ref[...], xb)
        qt = _dot_nt(wq_ref[...], xb).astype(BF16)
        for h in range(PEER_HEADS):
            base = h * PEER_DKEY
            s_ref[0] = _dot(k1_ref[h], qt[base:base + HALF_KEY, :])
            s_ref[1] = _dot(k2_ref[h], qt[base + HALF_KEY:base + PEER_DKEY, :])

            def lane_tile(lt, _):
                cols = pl.ds(pl.multiple_of(lt * LANES, LANES), LANES)
                rank2, p2, count, row_weight = _peer_tables(s_ref[0, :, cols], s_ref[1, :, cols])
                r2_ref[h, :, :, cols] = rank2.astype(BF16).reshape(groups, BF16_ROWS, LANES)
                p2_ref[h, :, :, cols] = p2.astype(BF16).reshape(groups, BF16_ROWS, LANES)
                cnt_ref[h, :, cols] = count
                cw_ref[h, :, cols] = row_weight
                return 0

            lax.fori_loop(0, ts // LANES, lane_tile, 0)
        acc_ref[...] = jnp.zeros_like(acc_ref)
        ht_ref[1] = jnp.zeros((PEER_EC, ts), BF16)

    st_ref[nxt] = _dot_nt(un_ref[...], xb_ref[...])
    acc_ref[...] += _dot(vp_ref[...], ht_ref[nxt])
    i1_base = pl.multiple_of(c * PEER_ROWS_PER_STEP, PEER_ROWS_PER_STEP)
    cnts = [cnt_ref[h, pl.ds(i1_base, PEER_ROWS_PER_STEP), :] for h in range(PEER_HEADS)]
    cws = [cw_ref[h, pl.ds(i1_base, PEER_ROWS_PER_STEP), :] for h in range(PEER_HEADS)]
    for j in range(PEER_ROWS_PER_STEP):
        rows = slice(j * PEER_NKEYS, (j + 1) * PEER_NKEYS)
        w = jnp.zeros((groups, BF16_ROWS, ts), BF16)
        for h in range(PEER_HEADS):
            cnt = jnp.broadcast_to(cnts[h][j:j + 1, :], (BF16_ROWS, ts)).astype(BF16)
            cw = jnp.broadcast_to(cws[h][j:j + 1, :], (BF16_ROWS, ts)).astype(BF16)
            w = w + jnp.where(r2_ref[h] < cnt[None], cw[None], 0.0) * p2_ref[h]
        a = _gelu(st_ref[cur, rows, :].astype(BF16))
        ht_ref[cur, rows, :] = w.reshape(PEER_NKEYS, ts) * a

    @pl.when(c == PEER_NCH - 1)
    def _epilogue():
        acc = acc_ref[...] + _dot(vl_ref[...], ht_ref[cur])
        y = DN_ALPHA * x_ref[...] + acc.T
        o_ref[...] = _layer_norm(y, lg_ref[...], lb_ref[...])


def _peer(x1, wq_bf, k1_bf, k2_bf, u_bf, vt_bf, lg_row, lb_row):
    n = x1.shape[0]
    ts = PEER_TS
    last = PEER_NCH - 1
    full = lambda a: pl.BlockSpec(a.shape, lambda i, c: (0,) * a.ndim)
    tok = pl.BlockSpec((ts, D_MODEL), lambda i, c: (i, 0))
    tables = pltpu.VMEM((PEER_HEADS, PEER_NKEYS, ts), F32)
    packed = pltpu.VMEM((PEER_HEADS, PEER_NKEYS // BF16_ROWS, BF16_ROWS, ts), BF16)
    return pl.pallas_call(
        _peer_kernel,
        grid=(n // ts, PEER_NCH),
        in_specs=[tok, full(wq_bf), full(k1_bf), full(k2_bf),
                  pl.BlockSpec((PEER_EC, D_MODEL), lambda i, c: (0, 0)),
                  pl.BlockSpec((PEER_EC, D_MODEL), lambda i, c: (jnp.minimum(c + 1, last), 0)),
                  pl.BlockSpec((D_MODEL, PEER_EC), lambda i, c: (0, jnp.maximum(c - 1, 0))),
                  pl.BlockSpec((D_MODEL, PEER_EC), lambda i, c: (0, last)),
                  full(lg_row), full(lb_row)],
        out_specs=tok,
        out_shape=jax.ShapeDtypeStruct((n, D_MODEL), F32),
        scratch_shapes=[
            pltpu.VMEM((ts, D_MODEL), BF16),
            pltpu.VMEM((2, PEER_NKEYS, ts), F32),
            packed, packed,
            tables, tables,
            pltpu.VMEM((D_MODEL, ts), F32),
            pltpu.VMEM((2, PEER_EC, ts), F32),
            pltpu.VMEM((2, PEER_EC, ts), BF16),
        ],
        compiler_params=_cparams("parallel", "arbitrary"),
        name="peer",
    )(x1, wq_bf, k1_bf, k2_bf, u_bf, u_bf, vt_bf, vt_bf, lg_row, lb_row)


def kernel(x_prompt, x_sample, cache_k, cache_v, state_ssm_re, state_ssm_im, page_table, w_in, b_in, w_a, w_b, w_o, ln1_g, ln1_b, a_re, a_im, log_dt, b_re, b_im, c_re, c_im, d_skip, w_glu, b_glu, ln2_g, ln2_b, w_pq, sub_k1, sub_k2, peer_u, peer_v):
    bn, t, _ = x_prompt.shape
    db, dt_, _ = x_sample.shape
    row = lambda a: a.reshape(1, -1).astype(F32)
    slopes = jnp.exp2(-8.0 * (jnp.arange(N_HEADS, dtype=F32) + 1.0) / N_HEADS)

    w_in_bf = w_in.astype(BF16)
    proj_w = (w_in_bf, row(b_in), w_in_bf[:, ATT_W:3 * ATT_W].T,
              b_in[ATT_W:3 * ATT_W].reshape(2 * ATT_W, 1).astype(F32))
    wa_bf, wb_bf, wo_bf, wglu_bf = (w.astype(BF16) for w in (w_a, w_b, w_o, w_glu))
    eye = jnp.eye(SSM_GROUPS, dtype=F32)
    c_blk = lambda cm: jnp.einsum('gcp,gh->gphc', cm, eye).reshape(SSM_FLAT, SSM_W)
    ccat = jnp.concatenate([c_blk(c_re), -c_blk(c_im)], axis=0).astype(BF16)
    peer_w = (w_pq.T.astype(BF16), sub_k1.astype(BF16), sub_k2.astype(BF16))
    u_bf = peer_u.astype(BF16)
    vt_bf = peer_v.T.astype(BF16)

    bbar, apr, api = _s5_prep(a_re, a_im, log_dt, b_re, b_im)
    s5_w = (bbar, apr, api, ccat, row(d_skip), wglu_bf, row(b_glu))

    xp = x_prompt.reshape(bn * t, D_MODEL)
    q, k_t, v_t, u, gates = _proj(xp, *proj_w, seq_len=t)
    att = _moba_prompt(q.reshape(bn, t, ATT_W), k_t, v_t, slopes)
    ssm, hre_p, him_p = _s5_prompt(u.reshape(bn, t, SSM_W), *s5_w)
    x1 = _merge(xp, att.reshape(bn * t, ATT_W), ssm.reshape(bn * t, SSM_W), gates,
                wa_bf, wb_bf, wo_bf, row(ln1_g), row(ln1_b))
    y_prompt = _peer(x1, *peer_w, u_bf, vt_bf, row(ln2_g), row(ln2_b)).reshape(bn, t, D_MODEL)
    per_token = lambda a: a.reshape(bn, N_HEADS, HEAD_DIM, t).transpose(0, 3, 1, 2)
    k_prompt, v_prompt = per_token(k_t), per_token(v_t)

    xs = x_sample.reshape(db * dt_, D_MODEL)
    q, k, v, u, gates = _proj(xs, *proj_w)
    att = _moba_sample(q.reshape(db, dt_, ATT_W), k.reshape(db, dt_, ATT_W), v.reshape(db, dt_, ATT_W),
                       cache_k, cache_v, page_table, slopes)
    ssm, hre_s, him_s = _s5_sample(u.reshape(db, dt_, SSM_W), state_ssm_re, state_ssm_im, *s5_w)
    x1 = _merge(xs, att.reshape(db * dt_, ATT_W), ssm.reshape(db * dt_, SSM_W), gates,
                wa_bf, wb_bf, wo_bf, row(ln1_g), row(ln1_b))
    y_sample = _peer(x1, *peer_w, u_bf, vt_bf, row(ln2_g), row(ln2_b)).reshape(db, dt_, D_MODEL)
    k_sample = k.reshape(db, dt_, N_HEADS, HEAD_DIM)
    v_sample = v.reshape(db, dt_, N_HEADS, HEAD_DIM)

    return (y_prompt, y_sample, k_prompt, v_prompt, k_sample, v_sample,
            hre_p, him_p, hre_s, him_s)
```

```python
import functools
import math

import jax
import jax.numpy as jnp
from jax import lax
from jax.experimental import pallas as pl
from jax.experimental.pallas import tpu as pltpu

F32 = jnp.float32
BF16 = jnp.bfloat16

D_MODEL = 1024
ATT_W = 512
HEAD_DIM = 64
N_HEADS = 8
MOBA_BLOCK = 256
MOBA_TOPK = 3
SSM_W = 512
SSM_GROUP = 16
SSM_GROUPS = 32
SSM_STATE = 64
SSM_FLAT = SSM_GROUPS * SSM_STATE
PEER_HEADS = 8
PEER_NKEYS = 128
PEER_EXPERTS = PEER_NKEYS * PEER_NKEYS
PEER_DKEY = 256
PEER_TOPK = 16
PROJ_W = 3 * ATT_W + SSM_W + 2 * D_MODEL
DEPTH = 1
DN_ALPHA = (2.0 * DEPTH) ** 0.25
LN_EPS = 1e-5
PAGE_SIZE = 128

LANES = 128
SUBLANES = 8
BF16_ROWS = 16
NEG_INF = float("-inf")
VMEM_LIMIT = 56 * 1024 * 1024

NT_DIMS = (((1,), (1,)), ((), ()))


def _cparams(*sem):
    return pltpu.CompilerParams(dimension_semantics=sem, vmem_limit_bytes=VMEM_LIMIT)


def _dot(a, b):
    return jnp.dot(a, b, preferred_element_type=F32)


def _dot_nt(a, b):
    return lax.dot_general(a, b, NT_DIMS, preferred_element_type=F32)


def _split(x):
    hi = x.astype(BF16)
    lo = (x - hi.astype(F32)).astype(BF16)
    return hi, lo


def _gelu(x):
    c = math.sqrt(2.0 / math.pi)
    return 0.5 * x * (1.0 + jnp.tanh(c * (x + 0.044715 * (x * x * x))))


def _layer_norm(y, g, b):
    mu = jnp.mean(y, axis=-1, keepdims=True)
    yc = y - mu
    var = jnp.mean(yc * yc, axis=-1, keepdims=True)
    return yc * lax.rsqrt(var + LN_EPS) * g + b


PROJ_TM = 512


def _proj_kernel(x_ref, w_ref, b_ref, wkv_t_ref, bkv_col_ref, q_ref, k_ref, v_ref, u_ref, g_ref,
                 *, kv_transposed):
    xb = x_ref[...].astype(BF16)

    def seg(lo, hi):
        return _dot(xb, w_ref[:, lo:hi]) + b_ref[:, lo:hi]

    q_ref[...] = seg(0, ATT_W)
    if kv_transposed:
        kv_t = _dot_nt(wkv_t_ref[...], xb) + bkv_col_ref[...]
        k_ref[...] = kv_t[:ATT_W, :]
        v_ref[...] = kv_t[ATT_W:, :]
    else:
        k_ref[...] = seg(ATT_W, 2 * ATT_W)
        v_ref[...] = seg(2 * ATT_W, 3 * ATT_W)
    u_ref[...] = seg(3 * ATT_W, 3 * ATT_W + SSM_W)
    g_ref[...] = jax.nn.sigmoid(seg(3 * ATT_W + SSM_W, PROJ_W))


def _proj(x, w_bf, b_row, wkv_t, bkv_col, seq_len=None):
    n = x.shape[0]
    tm = PROJ_TM
    row = lambda w: pl.BlockSpec((tm, w), lambda i: (i, 0))
    full = lambda a: pl.BlockSpec(a.shape, lambda i: (0,) * a.ndim)
    if seq_len is None:
        kv_spec, kv_shape = row(ATT_W), jax.ShapeDtypeStruct((n, ATT_W), F32)
    else:
        per_seq = seq_len // tm
        kv_spec = pl.BlockSpec((None, ATT_W, tm), lambda i: (i // per_seq, 0, i % per_seq))
        kv_shape = jax.ShapeDtypeStruct((n // seq_len, ATT_W, seq_len), F32)
    rows = lambda w: jax.ShapeDtypeStruct((n, w), F32)
    return pl.pallas_call(
        functools.partial(_proj_kernel, kv_transposed=seq_len is not None),
        grid=(n // tm,),
        in_specs=[row(D_MODEL), full(w_bf), full(b_row), full(wkv_t), full(bkv_col)],
        out_specs=[row(ATT_W), kv_spec, kv_spec, row(SSM_W), row(2 * D_MODEL)],
        out_shape=[rows(ATT_W), kv_shape, kv_shape, rows(SSM_W), rows(2 * D_MODEL)],
        compiler_params=_cparams("parallel"),
        name="proj",
    )(x, w_bf, b_row, wkv_t, bkv_col)


def _moba_select_bias(gates, cur):
    nb = len(gates)
    g = [jnp.where(n < cur, gates[n], NEG_INF) for n in range(nb)]
    bias = []
    for n in range(nb):
        rank = jnp.zeros(gates[n].shape, jnp.int32)
        for m in range(nb):
            if m != n:
                ahead = (g[m] >= g[n]) if m < n else (g[m] > g[n])
                rank = rank + ahead.astype(jnp.int32)
        sel = (n < cur) & (rank < MOBA_TOPK)
        bias.append(jnp.where(sel, 0.0, NEG_INF))
    return bias


def _moba_prompt_kernel(slopes_ref, q_ref, kt_ref, vt_in_ref, o_ref,
                        kb_ref, vt_ref, ot_ref):
    t = q_ref.shape[0]
    nb = t // MOBA_BLOCK
    blk = MOBA_BLOCK
    hp = pl.program_id(1)
    lane = lax.broadcasted_iota(jnp.int32, (1, LANES), 1)

    k = kt_ref[...].T
    kb_ref[...] = k.astype(BF16)
    vt_ref[...] = vt_in_ref[...].astype(BF16)
    kmean = jnp.mean(k.reshape(nb, blk, LANES), axis=1)
    q_all = q_ref[...]
    q_hi, q_lo = _split(q_all)
    cur = lax.broadcasted_iota(jnp.int32, (1, t), 1) // blk
    d0 = (lax.broadcasted_iota(jnp.int32, (blk, blk), 1)
          - lax.broadcasted_iota(jnp.int32, (blk, blk), 0)).astype(F32)

    heads = []
    for hh in range(2):
        hmask = (lane >= HEAD_DIM * hh) & (lane < HEAD_DIM * (hh + 1))
        slope = slopes_ref[2 * hp + hh]
        km_hi, km_lo = _split(jnp.where(hmask, kmean, 0.0))
        gate = _dot_nt(km_hi, q_hi) + _dot_nt(km_hi, q_lo) + _dot_nt(km_lo, q_hi)
        selb = _moba_select_bias([gate[n:n + 1, :] for n in range(nb)], cur)
        heads.append((hmask, slope, selb, slope * d0, slice(HEAD_DIM * hh, HEAD_DIM * (hh + 1))))

    for i in range(nb):
        qcols = slice(i * blk, (i + 1) * blk)
        n_keys = (i + 1) * blk
        q_blk = q_ref[qcols, :] * (HEAD_DIM ** -0.5)
        s_alls = [_dot_nt(kb_ref[0:n_keys, :], jnp.where(hmask, q_blk, 0.0).astype(BF16))
                  for hmask, *_ in heads]
        tiles = []
        for (hmask, slope, selb, slope_d0, rows), s_all in zip(heads, s_alls):
            per_head = []
            for j in range(i + 1):
                s = s_all[j * blk:(j + 1) * blk, :] - (slope_d0 + slope * float((i - j) * blk))
                if j < i:
                    s = s + selb[j][:, qcols]
                else:
                    s = jnp.where(d0 >= 0.0, s, NEG_INF)
                per_head.append(s)
            tiles.append(per_head)
        maxes = []
        for per_head in tiles:
            m = jnp.max(per_head[0], axis=0, keepdims=True)
            for s in per_head[1:]:
                m = jnp.maximum(m, jnp.max(s, axis=0, keepdims=True))
            maxes.append(m)
        probs = [[jnp.exp(s - m) for s in per_head] for per_head, m in zip(tiles, maxes)]
        sums = []
        for per_head in probs:
            l = jnp.sum(per_head[0], axis=0, keepdims=True)
            for p in per_head[1:]:
                l = l + jnp.sum(p, axis=0, keepdims=True)
            sums.append(l)
        for (hmask, slope, selb, slope_d0, rows), per_head, l in zip(heads, probs, sums):
            p_all = jnp.concatenate([p.astype(BF16) for p in per_head], axis=0)
            res = _dot(vt_ref[:, 0:n_keys], p_all) / l
            ot_ref[rows, qcols] = res[rows, :]

    o_ref[...] = ot_ref[...].T


def _moba_prompt(q, k_t, v_t, slopes):
    b, t, _ = q.shape
    spec = pl.BlockSpec((None, t, LANES), lambda bi, hp, *_: (bi, 0, hp))
    spec_t = pl.BlockSpec((None, LANES, t), lambda bi, hp, *_: (bi, hp, 0))
    grid_spec = pltpu.PrefetchScalarGridSpec(
        num_scalar_prefetch=1,
        grid=(b, ATT_W // LANES),
        in_specs=[spec, spec_t, spec_t],
        out_specs=spec,
        scratch_shapes=[
            pltpu.VMEM((t, LANES), BF16),
            pltpu.VMEM((LANES, t), BF16),
            pltpu.VMEM((LANES, t), F32),
        ],
    )
    return pl.pallas_call(
        _moba_prompt_kernel,
        grid_spec=grid_spec,
        out_shape=jax.ShapeDtypeStruct((b, t, ATT_W), F32),
        compiler_params=_cparams("parallel", "parallel"),
        name="moba_prompt",
    )(slopes, q, k_t, v_t)


def _moba_sample_kernel(pt_ref, slopes_ref, q_ref, kn_ref, vn_ref, *rest, n_pages):
    kt_refs = rest[:n_pages]
    vt_refs = rest[n_pages:2 * n_pages]
    o_ref = rest[2 * n_pages]
    nq = q_ref.shape[0]
    n_past = n_pages * PAGE_SIZE
    nb = n_past // MOBA_BLOCK
    pages_per_block = MOBA_BLOCK // PAGE_SIZE
    rows = N_HEADS * nq
    scale = HEAD_DIM ** -0.5
    assert rows % SUBLANES == 0 and nq <= PAGE_SIZE

    q = q_ref[...]
    r_id = lax.broadcasted_iota(jnp.int32, (rows, ATT_W), 0)
    c_id = lax.broadcasted_iota(jnp.int32, (rows, ATT_W), 1)
    q_rep = jnp.concatenate([q] * (rows // nq), axis=0)
    own = c_id // HEAD_DIM == r_id // nq
    qs = jnp.where(own, q_rep * scale, 0.0).astype(BF16)

    r_col = lax.broadcasted_iota(jnp.int32, (rows, 1), 0)
    tq = (r_col % nq).astype(F32)
    slope = jnp.zeros((rows, 1), F32)
    for h in range(N_HEADS):
        slope = jnp.where(r_col // nq == h, slopes_ref[h], slope)
    lane = lax.broadcasted_iota(jnp.int32, (1, PAGE_SIZE), 1).astype(F32)

    raw = [_dot(qs, kt_refs[pg][...].astype(BF16)) for pg in range(n_pages)]

    gates = []
    for n in range(nb):
        total = raw[n * pages_per_block]
        for pg in range(n * pages_per_block + 1, (n + 1) * pages_per_block):
            total = total + raw[pg]
        gates.append(jnp.sum(total, axis=1, keepdims=True) * (1.0 / (MOBA_BLOCK * scale)))
    bias = _moba_select_bias(gates, nb)

    tiles = []
    for pg in range(n_pages):
        dist = (tq + float(n_past - pg * PAGE_SIZE)) - lane
        tiles.append(raw[pg] - slope * dist + bias[pg // pages_per_block])
    pad_rows = lambda a: jnp.concatenate([a, jnp.zeros((PAGE_SIZE - nq, ATT_W), F32)], axis=0)
    dist_new = tq - lane
    s_new = _dot_nt(qs, pad_rows(kn_ref[...]).astype(BF16))
    tiles.append(jnp.where(dist_new >= 0.0, s_new - slope * dist_new, NEG_INF))

    top = tiles[0]
    for s in tiles[1:]:
        top = jnp.maximum(top, s)
    m = jnp.max(top, axis=1, keepdims=True)
    probs = [jnp.exp(s - m) for s in tiles]
    total = probs[0]
    for p in probs[1:]:
        total = total + p
    inv_l = 1.0 / jnp.sum(total, axis=1, keepdims=True)
    out = _dot((probs[n_pages] * inv_l).astype(BF16), pad_rows(vn_ref[...]).astype(BF16))
    for pg in range(n_pages):
        out = out + _dot_nt((probs[pg] * inv_l).astype(BF16), vt_refs[pg][...].astype(BF16))

    res = jnp.zeros((nq, ATT_W), F32)
    c_q = lax.broadcasted_iota(jnp.int32, (nq, ATT_W), 1)
    for h in range(N_HEADS):
        res = jnp.where(c_q // HEAD_DIM == h, out[h * nq:(h + 1) * nq, :], res)
    o_ref[...] = res


def _moba_sample(q, k_new, v_new, cache_k, cache_v, page_table, slopes):
    db, nq, _ = q.shape
    n_pages = page_table.shape[1]
    new_spec = pl.BlockSpec((None, nq, ATT_W), lambda s, *_: (s, 0, 0))

    def page_spec(pg):
        return pl.BlockSpec((None, ATT_W, PAGE_SIZE), lambda s, pt, sl: (pt[s, pg], 0, 0))

    grid_spec = pltpu.PrefetchScalarGridSpec(
        num_scalar_prefetch=2,
        grid=(db,),
        in_specs=([new_spec] * 3 + [page_spec(pg) for pg in range(n_pages)] * 2),
        out_specs=new_spec,
    )
    by_token = lambda c: jnp.transpose(c, (0, 2, 3, 1)).reshape(c.shape[0], ATT_W, PAGE_SIZE)
    ck, cv = by_token(cache_k), by_token(cache_v)
    return pl.pallas_call(
        functools.partial(_moba_sample_kernel, n_pages=n_pages),
        grid_spec=grid_spec,
        out_shape=jax.ShapeDtypeStruct((db, nq, ATT_W), F32),
        compiler_params=_cparams("parallel"),
        name="moba_sample",
    )(page_table, slopes, q, k_new, v_new, *([ck] * n_pages), *([cv] * n_pages))


S5_CHUNK = 256


def _s5_prep_kernel(are_ref, aim_ref, ldt_ref, bre_ref, bim_ref,
                    bbar_ref, apr_ref, api_ref):
    ar, ai = are_ref[...], aim_ref[...]
    dt = jnp.exp(ldt_ref[...])
    mag = jnp.exp(dt * ar)
    ang = dt * ai
    abr, abi = mag * jnp.cos(ang), mag * jnp.sin(ang)
    den = ar * ar + ai * ai
    nr, ni = abr - 1.0, abi
    f_re = (nr * ar + ni * ai) / den
    f_im = (ni * ar - nr * ai) / den
    bre, bim = bre_ref[...], bim_ref[...]
    bbar_ref[:, :SSM_FLAT] = (f_re * bre - f_im * bim).astype(BF16)
    bbar_ref[:, SSM_FLAT:] = (f_re * bim + f_im * bre).astype(BF16)
    apr_ref[0:1, :] = abr
    api_ref[0:1, :] = abi
    d = 1
    while d < S5_CHUNK:
        pr, pi = apr_ref[0:d, :], api_ref[0:d, :]
        er, ei = apr_ref[d - 1:d, :], api_ref[d - 1:d, :]
        apr_ref[d:2 * d, :] = pr * er - pi * ei
        api_ref[d:2 * d, :] = pr * ei + pi * er
        d *= 2


def _s5_prep(a_re, a_im, log_dt, b_re, b_im):
    eye = jnp.eye(SSM_GROUPS, dtype=F32)
    blockdiag = lambda b: jnp.einsum('gpc,gh->gchp', b, eye).reshape(SSM_W, SSM_FLAT)
    flat = lambda a: a.reshape(1, SSM_FLAT)
    ldt = jnp.broadcast_to(log_dt[:, None], (SSM_GROUPS, SSM_STATE))
    return pl.pallas_call(
        _s5_prep_kernel,
        out_shape=[jax.ShapeDtypeStruct((SSM_W, 2 * SSM_FLAT), BF16),
                   jax.ShapeDtypeStruct((S5_CHUNK, SSM_FLAT), F32),
                   jax.ShapeDtypeStruct((S5_CHUNK, SSM_FLAT), F32)],
        compiler_params=pltpu.CompilerParams(vmem_limit_bytes=VMEM_LIMIT),
        name="s5_prep",
    )(flat(a_re), flat(a_im), flat(ldt), blockdiag(b_re), blockdiag(b_im))


def _s5_readout(h_re, h_im, u, ccat_ref, d_ref, wglu_ref, bglu_ref):
    hcat = jnp.concatenate([h_re.astype(BF16), h_im.astype(BF16)], axis=1)
    y = _dot(hcat, ccat_ref[...]) + d_ref[...] * u
    z = _gelu(y)
    return z * jax.nn.sigmoid(_dot(z.astype(BF16), wglu_ref[...]) + bglu_ref[...])


def _s5_prompt_kernel(u_ref, bbar_ref, apr_ref, api_ref, a8r_ref, a8i_ref, ccat_ref, d_ref,
                      wglu_ref, bglu_ref,
                      y_ref, hre_ref, him_ref, cr_ref, ci_ref, hr_ref, hi_ref, pr_ref, pi_ref):
    c = pl.program_id(1)
    n = u_ref.shape[0]
    n_groups = n // SUBLANES

    @pl.when(c == 0)
    def _():
        cr_ref[...] = jnp.zeros_like(cr_ref)
        ci_ref[...] = jnp.zeros_like(ci_ref)

    def scan_level(hr, hi, pos, d, power):
        er, ei = apr_ref[power - 1:power, :], api_ref[power - 1:power, :]
        keep = pos >= d
        sr = jnp.where(keep, pltpu.roll(hr, d, axis=0), 0.0)
        si = jnp.where(keep, pltpu.roll(hi, d, axis=0), 0.0)
        return hr + er * sr - ei * si, hi + er * si + ei * sr

    u = u_ref[...]
    bu = _dot(u.astype(BF16), bbar_ref[...])
    hr, hi = bu[:, :SSM_FLAT], bu[:, SSM_FLAT:]
    in_group = lax.broadcasted_iota(jnp.int32, (n, SSM_FLAT), 0) % SUBLANES
    d = 1
    while d < SUBLANES:
        hr, hi = scan_level(hr, hi, in_group, d, d)
        d *= 2
    n_lt = SSM_FLAT // LANES
    lt_cols = [slice(l * LANES, (l + 1) * LANES) for l in range(n_lt)]
    for l in range(n_lt):
        hr_ref[l] = hr[:, lt_cols[l]]
        hi_ref[l] = hi[:, lt_cols[l]]

    ends = pl.ds(SUBLANES - 1, n_groups, stride=SUBLANES)
    gr = jnp.concatenate([hr_ref[l, ends, :] for l in range(n_lt)], axis=1)
    gi = jnp.concatenate([hi_ref[l, ends, :] for l in range(n_lt)], axis=1)
    group = lax.broadcasted_iota(jnp.int32, (n_groups, SSM_FLAT), 0)
    d = 1
    while d < n_groups:
        gr, gi = scan_level(gr, gi, group, d, d * SUBLANES)
        d *= 2
    cr, ci = cr_ref[...], ci_ref[...]
    a8r, a8i = a8r_ref[...], a8i_ref[...]
    gr = gr + a8r * cr - a8i * ci
    gi = gi + a8r * ci + a8i * cr
    cr_ref[...] = gr[n_groups - 1:n_groups, :]
    ci_ref[...] = gi[n_groups - 1:n_groups, :]
    hre_ref[...] = gr[n_groups - 1:n_groups, :]
    him_ref[...] = gi[n_groups - 1:n_groups, :]
    first = group == 0
    pr_ref[...] = jnp.where(first, cr, pltpu.roll(gr, 1, axis=0))
    pi_ref[...] = jnp.where(first, ci, pltpu.roll(gi, 1, axis=0))

    a1r, a1i = apr_ref[0:SUBLANES, :], api_ref[0:SUBLANES, :]
    for g in range(n_groups):
        rows = slice(g * SUBLANES, (g + 1) * SUBLANES)
        br = jnp.broadcast_to(pr_ref[g:g + 1, :], (SUBLANES, SSM_FLAT))
        bi = jnp.broadcast_to(pi_ref[g:g + 1, :], (SUBLANES, SSM_FLAT))
        add_r = a1r * br - a1i * bi
        add_i = a1r * bi + a1i * br
        for l in range(n_lt):
            hr_ref[l, rows, :] = hr_ref[l, rows, :] + add_r[:, lt_cols[l]]
            hi_ref[l, rows, :] = hi_ref[l, rows, :] + add_i[:, lt_cols[l]]
    hr = jnp.concatenate([hr_ref[l] for l in range(n_lt)], axis=1)
    hi = jnp.concatenate([hi_ref[l] for l in range(n_lt)], axis=1)
    y_ref[...] = _s5_readout(hr, hi, u, ccat_ref, d_ref, wglu_ref, bglu_ref)


def _s5_prompt(u, bbar, apr, api, ccat, d_row, wglu_bf, bglu_row):
    b, t, _ = u.shape
    n = S5_CHUNK
    full = lambda a: pl.BlockSpec(a.shape, lambda bi, c: (0,) * a.ndim)
    tok = pl.BlockSpec((None, n, SSM_W), lambda bi, c: (bi, c, 0))
    st = pl.BlockSpec((None, 1, SSM_FLAT), lambda bi, c: (bi, 0, 0))
    a8r, a8i = apr[SUBLANES - 1::SUBLANES], api[SUBLANES - 1::SUBLANES]
    lane_major = pltpu.VMEM((SSM_FLAT // LANES, n, LANES), F32)
    y, hre, him = pl.pallas_call(
        _s5_prompt_kernel,
        grid=(b, t // n),
        in_specs=[tok, full(bbar), full(apr), full(api), full(a8r), full(a8i), full(ccat),
                  full(d_row), full(wglu_bf), full(bglu_row)],
        out_specs=[tok, st, st],
        out_shape=[jax.ShapeDtypeStruct((b, t, SSM_W), F32),
                   jax.ShapeDtypeStruct((b, 1, SSM_FLAT), F32),
                   jax.ShapeDtypeStruct((b, 1, SSM_FLAT), F32)],
        scratch_shapes=[pltpu.VMEM((1, SSM_FLAT), F32), pltpu.VMEM((1, SSM_FLAT), F32),
                        lane_major, lane_major,
                        pltpu.VMEM((n // SUBLANES, SSM_FLAT), F32),
                        pltpu.VMEM((n // SUBLANES, SSM_FLAT), F32)],
        compiler_params=_cparams("parallel", "arbitrary"),
        name="s5_prompt",
    )(u, bbar, apr, api, a8r, a8i, ccat, d_row, wglu_bf, bglu_row)
    return y, hre.reshape(b, SSM_GROUPS, SSM_STATE), him.reshape(b, SSM_GROUPS, SSM_STATE)


def _s5_sample_kernel(u_ref, h0r_ref, h0i_ref, bbar_ref, apr_ref, api_ref, ccat_ref, d_ref,
                      wglu_ref, bglu_ref, y_ref, hre_ref, him_ref, *, n_steps):
    hr, hi = h0r_ref[...], h0i_ref[...]
    ar, ai = apr_ref[0:1, :], api_ref[0:1, :]
    for t in range(n_steps):
        u = u_ref[:, t * SSM_W:(t + 1) * SSM_W]
        bu = _dot(u.astype(BF16), bbar_ref[...])
        hr, hi = (ar * hr - ai * hi + bu[:, :SSM_FLAT],
                  ar * hi + ai * hr + bu[:, SSM_FLAT:])
        y_ref[:, t * SSM_W:(t + 1) * SSM_W] = _s5_readout(
            hr, hi, u, ccat_ref, d_ref, wglu_ref, bglu_ref)
    hre_ref[...] = hr
    him_ref[...] = hi


def _s5_sample(u, h0_re, h0_im, bbar, apr, api, ccat, d_row, wglu_bf, bglu_row):
    db, t, _ = u.shape
    y, hre, him = pl.pallas_call(
        functools.partial(_s5_sample_kernel, n_steps=t),
        out_shape=[jax.ShapeDtypeStruct((db, t * SSM_W), F32),
                   jax.ShapeDtypeStruct((db, SSM_FLAT), F32),
                   jax.ShapeDtypeStruct((db, SSM_FLAT), F32)],
        compiler_params=pltpu.CompilerParams(vmem_limit_bytes=VMEM_LIMIT),
        name="s5_sample",
    )(u.reshape(db, t * SSM_W), h0_re.reshape(db, SSM_FLAT), h0_im.reshape(db, SSM_FLAT),
      bbar, apr, api, ccat, d_row, wglu_bf, bglu_row)
    return (y.reshape(db, t, SSM_W), hre.reshape(db, SSM_GROUPS, SSM_STATE),
            him.reshape(db, SSM_GROUPS, SSM_STATE))


MERGE_TM = 512


def _merge_kernel(x_ref, att_ref, ssm_ref, g_ref, wa_ref, wb_ref, wo_ref, lg_ref, lb_ref, o_ref):
    ya = _dot(att_ref[...].astype(BF16), wa_ref[...])
    yb = _dot(ssm_ref[...].astype(BF16), wb_ref[...])
    merged = g_ref[:, :D_MODEL] * ya + g_ref[:, D_MODEL:] * yb
    y = DN_ALPHA * x_ref[...] + _dot(merged.astype(BF16), wo_ref[...])
    o_ref[...] = _layer_norm(y, lg_ref[...], lb_ref[...])


def _merge(x, att, ssm, gates, wa_bf, wb_bf, wo_bf, lg_row, lb_row):
    n = x.shape[0]
    tm = MERGE_TM
    row = lambda w: pl.BlockSpec((tm, w), lambda i: (i, 0))
    full = lambda a: pl.BlockSpec(a.shape, lambda i: (0,) * a.ndim)
    return pl.pallas_call(
        _merge_kernel,
        grid=(n // tm,),
        in_specs=[row(D_MODEL), row(ATT_W), row(SSM_W), row(2 * D_MODEL),
                  full(wa_bf), full(wb_bf), full(wo_bf), full(lg_row), full(lb_row)],
        out_specs=row(D_MODEL),
        out_shape=jax.ShapeDtypeStruct((n, D_MODEL), F32),
        compiler_params=_cparams("parallel"),
        name="merge",
    )(x, att, ssm, gates, wa_bf, wb_bf, wo_bf, lg_row, lb_row)


PEER_TS = 512
PEER_EC = 1024
PEER_NCH = PEER_EXPERTS // PEER_EC
PEER_ROWS_PER_STEP = PEER_EC // PEER_NKEYS
HALF_KEY = PEER_DKEY // 2


def _sorting_network(n):
    pairs = []
    t = (n - 1).bit_length()
    p = 1 << (t - 1)
    while p > 0:
        q, r, d = 1 << (t - 1), 0, p
        while d > 0:
            pairs.extend((i, i + d) for i in range(n - d) if (i & p) == r)
            d, q, r = q - p, q >> 1, p
        p >>= 1
    return pairs


def _top_sorted(s, k):
    assert s.shape[0] == k * SUBLANES and k & (k - 1) == 0
    v = [s[i * SUBLANES:(i + 1) * SUBLANES, :] for i in range(k)]
    for i, j in _sorting_network(k):
        v[i], v[j] = jnp.maximum(v[i], v[j]), jnp.minimum(v[i], v[j])
    shift = SUBLANES // 2
    while shift:
        other = [pltpu.roll(x, shift, axis=0) for x in v]
        v = [jnp.maximum(v[i], other[k - 1 - i]) for i in range(k)]
        stride = k // 2
        while stride:
            for i in range(k):
                if (i // stride) % 2 == 0:
                    lo, hi = v[i], v[i + stride]
                    v[i], v[i + stride] = jnp.maximum(lo, hi), jnp.minimum(lo, hi)
            stride //= 2
        shift //= 2
    return v


def _peer_tables(s1, s2):
    k = PEER_TOPK
    c = s1.shape[1]
    groups3 = (k, SUBLANES, c)
    v1 = _top_sorted(s1, k)
    v2 = _top_sorted(s2, k)
    s2g = s2.reshape(groups3)
    rank2 = jnp.zeros(groups3, F32)
    for b in range(k):
        rank2 = jnp.where(s2g < v2[b][None], float(b + 1), rank2)
    sub = lax.broadcasted_iota(jnp.int32, (SUBLANES, c), 0)

    def pack(rows):
        out = rows[0]
        for r in range(1, SUBLANES):
            out = jnp.where(sub == r, rows[r], out)
        return out

    v2_lo, v2_hi, v1_hi = pack(v2[:SUBLANES]), pack(v2[SUBLANES:]), pack(v1[SUBLANES:])
    cands = [v1[0] + v2_lo, v1[0] + v2_hi, v1[1] + v2_lo]
    for a in range(2, SUBLANES):
        cands.append(jnp.where(sub < k // (a + 1), v1[a] + v2_lo, NEG_INF))
    cands.append(v1_hi + v2[0])
    cands += [jnp.full((SUBLANES, c), NEG_INF, F32)] * (k - len(cands))
    sc = _top_sorted(jnp.concatenate(cands, axis=0), k)
    tau = sc[k - 1]
    z = jnp.zeros_like(tau)
    for r in range(k):
        z = z + jnp.exp(sc[r] - sc[0])
    s1g = s1.reshape(groups3)
    count = jnp.zeros(groups3, F32)
    for a in range(k):
        n_sel = jnp.zeros_like(tau)
        for b in range(k // (a + 1)):
            n_sel = n_sel + jnp.where(v1[a] + v2[b] >= tau, 1.0, 0.0)
        count = jnp.where(s1g == v1[a][None], n_sel[None], count)
    row_weight = jnp.exp(s1g - v1[0][None]) / z[None]
    p2 = jnp.exp(s2g - v2[0][None])
    flat = lambda a: a.reshape(k * SUBLANES, c)
    return flat(rank2), flat(p2), flat(count), flat(row_weight)


def _peer_kernel(x_ref, wq_ref, k1_ref, k2_ref,
                 u0_ref, un_ref, vp_ref, vl_ref, lg_ref, lb_ref, o_ref,
                 xb_ref, s_ref, r2_ref, p2_ref, cnt_ref, cw_ref, acc_ref, st_ref, ht_ref):
    c = pl.program_id(1)
    ts = x_ref.shape[0]
    groups = PEER_NKEYS // BF16_ROWS
    cur = c % 2
    nxt = 1 - cur

    @pl.when(c == 0)
    def _prologue():
        xb = x_ref[...].astype(BF16)
        xb_ref[...] = xb
        st_ref[0] = _dot_nt(u0_ref[...], xb)
        qt = _dot_nt(wq_ref[...], xb).astype(BF16)
        for h in range(PEER_HEADS):
            base = h * PEER_DKEY
            s_ref[0] = _dot(k1_ref[h], qt[base:base + HALF_KEY, :])
            s_ref[1] = _dot(k2_ref[h], qt[base + HALF_KEY:base + PEER_DKEY, :])

            def lane_tile(lt, _):
                cols = pl.ds(pl.multiple_of(lt * LANES, LANES), LANES)
                rank2, p2, count, row_weight = _peer_tables(s_ref[0, :, cols], s_ref[1, :, cols])
                r2_ref[h, :, :, cols] = rank2.astype(BF16).reshape(groups, BF16_ROWS, LANES)
                p2_ref[h, :, :, cols] = p2.astype(BF16).reshape(groups, BF16_ROWS, LANES)
                cnt_ref[h, :, cols] = count
                cw_ref[h, :, cols] = row_weight
                return 0

            lax.fori_loop(0, ts // LANES, lane_tile, 0)
        acc_ref[...] = jnp.zeros_like(acc_ref)
        ht_ref[1] = jnp.zeros((PEER_EC, ts), BF16)

    st_ref[nxt] = _dot_nt(un_ref[...], xb_ref[...])
    acc_ref[...] += _dot(vp_ref[...], ht_ref[nxt])
    i1_base = pl.multiple_of(c * PEER_ROWS_PER_STEP, PEER_ROWS_PER_STEP)
    cnts = [cnt_ref[h, pl.ds(i1_base, PEER_ROWS_PER_STEP), :] for h in range(PEER_HEADS)]
    cws = [cw_ref[h, pl.ds(i1_base, PEER_ROWS_PER_STEP), :] for h in range(PEER_HEADS)]
    for j in range(PEER_ROWS_PER_STEP):
        rows = slice(j * PEER_NKEYS, (j + 1) * PEER_NKEYS)
        w = jnp.zeros((groups, BF16_ROWS, ts), BF16)
        for h in range(PEER_HEADS):
            cnt = jnp.broadcast_to(cnts[h][j:j + 1, :], (BF16_ROWS, ts)).astype(BF16)
            cw = jnp.broadcast_to(cws[h][j:j + 1, :], (BF16_ROWS, ts)).astype(BF16)
            w = w + jnp.where(r2_ref[h] < cnt[None], cw[None], 0.0) * p2_ref[h]
        a = _gelu(st_ref[cur, rows, :].astype(BF16))
        ht_ref[cur, rows, :] = w.reshape(PEER_NKEYS, ts) * a

    @pl.when(c == PEER_NCH - 1)
    def _epilogue():
        acc = acc_ref[...] + _dot(vl_ref[...], ht_ref[cur])
        y = DN_ALPHA * x_ref[...] + acc.T
        o_ref[...] = _layer_norm(y, lg_ref[...], lb_ref[...])


def _peer(x1, wq_bf, k1_bf, k2_bf, u_bf, vt_bf, lg_row, lb_row):
    n = x1.shape[0]
    ts = PEER_TS
    last = PEER_NCH - 1
    full = lambda a: pl.BlockSpec(a.shape, lambda i, c: (0,) * a.ndim)
    tok = pl.BlockSpec((ts, D_MODEL), lambda i, c: (i, 0))
    tables = pltpu.VMEM((PEER_HEADS, PEER_NKEYS, ts), F32)
    packed = pltpu.VMEM((PEER_HEADS, PEER_NKEYS // BF16_ROWS, BF16_ROWS, ts), BF16)
    return pl.pallas_call(
        _peer_kernel,
        grid=(n // ts, PEER_NCH),
        in_specs=[tok, full(wq_bf), full(k1_bf), full(k2_bf),
                  pl.BlockSpec((PEER_EC, D_MODEL), lambda i, c: (0, 0)),
                  pl.BlockSpec((PEER_EC, D_MODEL), lambda i, c: (jnp.minimum(c + 1, last), 0)),
                  pl.BlockSpec((D_MODEL, PEER_EC), lambda i, c: (0, jnp.maximum(c - 1, 0))),
                  pl.BlockSpec((D_MODEL, PEER_EC), lambda i, c: (0, last)),
                  full(lg_row), full(lb_row)],
        out_specs=tok,
        out_shape=jax.ShapeDtypeStruct((n, D_MODEL), F32),
        scratch_shapes=[
            pltpu.VMEM((ts, D_MODEL), BF16),
            pltpu.VMEM((2, PEER_NKEYS, ts), F32),
            packed, packed,
            tables, tables,
            pltpu.VMEM((D_MODEL, ts), F32),
            pltpu.VMEM((2, PEER_EC, ts), F32),
            pltpu.VMEM((2, PEER_EC, ts), BF16),
        ],
        compiler_params=_cparams("parallel", "arbitrary"),
        name="peer",
    )(x1, wq_bf, k1_bf, k2_bf, u_bf, u_bf, vt_bf, vt_bf, lg_row, lb_row)


def kernel(x_prompt, x_sample, cache_k, cache_v, state_ssm_re, state_ssm_im, page_table, w_in, b_in, w_a, w_b, w_o, ln1_g, ln1_b, a_re, a_im, log_dt, b_re, b_im, c_re, c_im, d_skip, w_glu, b_glu, ln2_g, ln2_b, w_pq, sub_k1, sub_k2, peer_u, peer_v):
    bn, t, _ = x_prompt.shape
    db, dt_, _ = x_sample.shape
    row = lambda a: a.reshape(1, -1).astype(F32)
    slopes = jnp.exp2(-8.0 * (jnp.arange(N_HEADS, dtype=F32) + 1.0) / N_HEADS)

    w_in_bf = w_in.astype(BF16)
    proj_w = (w_in_bf, row(b_in), w_in_bf[:, ATT_W:3 * ATT_W].T,
              b_in[ATT_W:3 * ATT_W].reshape(2 * ATT_W, 1).astype(F32))
    wa_bf, wb_bf, wo_bf, wglu_bf = (w.astype(BF16) for w in (w_a, w_b, w_o, w_glu))
    eye = jnp.eye(SSM_GROUPS, dtype=F32)
    c_blk = lambda cm: jnp.einsum('gcp,gh->gphc', cm, eye).reshape(SSM_FLAT, SSM_W)
    ccat = jnp.concatenate([c_blk(c_re), -c_blk(c_im)], axis=0).astype(BF16)
    peer_w = (w_pq.T.astype(BF16), sub_k1.astype(BF16), sub_k2.astype(BF16))
    u_bf = peer_u.astype(BF16)
    vt_bf = peer_v.T.astype(BF16)

    bbar, apr, api = _s5_prep(a_re, a_im, log_dt, b_re, b_im)
    s5_w = (bbar, apr, api, ccat, row(d_skip), wglu_bf, row(b_glu))

    xp = x_prompt.reshape(bn * t, D_MODEL)
    q, k_t, v_t, u, gates = _proj(xp, *proj_w, seq_len=t)
    att = _moba_prompt(q.reshape(bn, t, ATT_W), k_t, v_t, slopes)
    ssm, hre_p, him_p = _s5_prompt(u.reshape(bn, t, SSM_W), *s5_w)
    x1 = _merge(xp, att.reshape(bn * t, ATT_W), ssm.reshape(bn * t, SSM_W), gates,
                wa_bf, wb_bf, wo_bf, row(ln1_g), row(ln1_b))
    y_prompt = _peer(x1, *peer_w, u_bf, vt_bf, row(ln2_g), row(ln2_b)).reshape(bn, t, D_MODEL)
    per_token = lambda a: a.reshape(bn, N_HEADS, HEAD_DIM, t).transpose(0, 3, 1, 2)
    k_prompt, v_prompt = per_token(k_t), per_token(v_t)

    xs = x_sample.reshape(db * dt_, D_MODEL)
    q, k, v, u, gates = _proj(xs, *proj_w)
    att = _moba_sample(q.reshape(db, dt_, ATT_W), k.reshape(db, dt_, ATT_W), v.reshape(db, dt_, ATT_W),
                       cache_k, cache_v, page_table, slopes)
    ssm, hre_s, him_s = _s5_sample(u.reshape(db, dt_, SSM_W), state_ssm_re, state_ssm_im, *s5_w)
    x1 = _merge(xs, att.reshape(db * dt_, ATT_W), ssm.reshape(db * dt_, SSM_W), gates,
                wa_bf, wb_bf, wo_bf, row(ln1_g), row(ln1_b))
    y_sample = _peer(x1, *peer_w, u_bf, vt_bf, row(ln2_g), row(ln2_b)).reshape(db, dt_, D_MODEL)
    k_sample = k.reshape(db, dt_, N_HEADS, HEAD_DIM)
    v_sample = v.reshape(db, dt_, N_HEADS, HEAD_DIM)

    return (y_prompt, y_sample, k_prompt, v_prompt, k_sample, v_sample,
            hre_p, him_p, hre_s, him_s)
```

```python
import functools
import math

import jax
import jax.numpy as jnp
from jax import lax
from jax.experimental import pallas as pl
from jax.experimental.pallas import tpu as pltpu

F32 = jnp.float32
BF16 = jnp.bfloat16

D_MODEL = 1024
ATT_W = 512
HEAD_DIM = 64
N_HEADS = 8
MOBA_BLOCK = 256
MOBA_TOPK = 3
SSM_W = 512
SSM_GROUP = 16
SSM_GROUPS = 32
SSM_STATE = 64
SSM_FLAT = SSM_GROUPS * SSM_STATE
PEER_HEADS = 8
PEER_NKEYS = 128
PEER_EXPERTS = PEER_NKEYS * PEER_NKEYS
PEER_DKEY = 256
PEER_TOPK = 16
PROJ_W = 3 * ATT_W + SSM_W + 2 * D_MODEL
DEPTH = 1
DN_ALPHA = (2.0 * DEPTH) ** 0.25
LN_EPS = 1e-5
PAGE_SIZE = 128

LANES = 128
SUBLANES = 8
BF16_ROWS = 16
NEG_INF = float("-inf")
VMEM_LIMIT = 56 * 1024 * 1024

NT_DIMS = (((1,), (1,)), ((), ()))


def _cparams(*sem):
    return pltpu.CompilerParams(dimension_semantics=sem, vmem_limit_bytes=VMEM_LIMIT)


def _dot(a, b):
    return jnp.dot(a, b, preferred_element_type=F32)


def _dot_nt(a, b):
    return lax.dot_general(a, b, NT_DIMS, preferred_element_type=F32)


def _split(x):
    hi = x.astype(BF16)
    lo = (x - hi.astype(F32)).astype(BF16)
    return hi, lo


def _gelu(x):
    c = math.sqrt(2.0 / math.pi)
    return 0.5 * x * (1.0 + jnp.tanh(c * (x + 0.044715 * (x * x * x))))


def _layer_norm(y, g, b):
    mu = jnp.mean(y, axis=-1, keepdims=True)
    yc = y - mu
    var = jnp.mean(yc * yc, axis=-1, keepdims=True)
    return yc * lax.rsqrt(var + LN_EPS) * g + b


PROJ_TM = 512


def _proj_kernel(x_ref, w_ref, b_ref, wkv_t_ref, bkv_col_ref, q_ref, k_ref, v_ref, u_ref, g_ref,
                 *, kv_transposed):
    xb = x_ref[...].astype(BF16)

    def seg(lo, hi):
        return _dot(xb, w_ref[:, lo:hi]) + b_ref[:, lo:hi]

    q_ref[...] = seg(0, ATT_W)
    if kv_transposed:
        kv_t = _dot_nt(wkv_t_ref[...], xb) + bkv_col_ref[...]
        k_ref[...] = kv_t[:ATT_W, :]
        v_ref[...] = kv_t[ATT_W:, :]
    else:
        k_ref[...] = seg(ATT_W, 2 * ATT_W)
        v_ref[...] = seg(2 * ATT_W, 3 * ATT_W)
    u_ref[...] = seg(3 * ATT_W, 3 * ATT_W + SSM_W)
    g_ref[...] = jax.nn.sigmoid(seg(3 * ATT_W + SSM_W, PROJ_W))


def _proj(x, w_bf, b_row, wkv_t, bkv_col, seq_len=None):
    n = x.shape[0]
    tm = PROJ_TM
    row = lambda w: pl.BlockSpec((tm, w), lambda i: (i, 0))
    full = lambda a: pl.BlockSpec(a.shape, lambda i: (0,) * a.ndim)
    if seq_len is None:
        kv_spec, kv_shape = row(ATT_W), jax.ShapeDtypeStruct((n, ATT_W), F32)
    else:
        per_seq = seq_len // tm
        kv_spec = pl.BlockSpec((None, ATT_W, tm), lambda i: (i // per_seq, 0, i % per_seq))
        kv_shape = jax.ShapeDtypeStruct((n // seq_len, ATT_W, seq_len), F32)
    rows = lambda w: jax.ShapeDtypeStruct((n, w), F32)
    return pl.pallas_call(
        functools.partial(_proj_kernel, kv_transposed=seq_len is not None),
        grid=(n // tm,),
        in_specs=[row(D_MODEL), full(w_bf), full(b_row), full(wkv_t), full(bkv_col)],
        out_specs=[row(ATT_W), kv_spec, kv_spec, row(SSM_W), row(2 * D_MODEL)],
        out_shape=[rows(ATT_W), kv_shape, kv_shape, rows(SSM_W), rows(2 * D_MODEL)],
        compiler_params=_cparams("parallel"),
        name="proj",
    )(x, w_bf, b_row, wkv_t, bkv_col)


def _moba_select_bias(gates, cur):
    nb = len(gates)
    g = [jnp.where(n < cur, gates[n], NEG_INF) for n in range(nb)]
    bias = []
    for n in range(nb):
        rank = jnp.zeros(gates[n].shape, jnp.int32)
        for m in range(nb):
            if m != n:
                ahead = (g[m] >= g[n]) if m < n else (g[m] > g[n])
                rank = rank + ahead.astype(jnp.int32)
        sel = (n < cur) & (rank < MOBA_TOPK)
        bias.append(jnp.where(sel, 0.0, NEG_INF))
    return bias


def _moba_prompt_kernel(slopes_ref, q_ref, kt_ref, vt_in_ref, o_ref,
                        kb_ref, vt_ref, ot_ref):
    t = q_ref.shape[0]
    nb = t // MOBA_BLOCK
    blk = MOBA_BLOCK
    hp = pl.program_id(1)
    lane = lax.broadcasted_iota(jnp.int32, (1, LANES), 1)

    k = kt_ref[...].T
    kb_ref[...] = k.astype(BF16)
    vt_ref[...] = vt_in_ref[...].astype(BF16)
    kmean = jnp.mean(k.reshape(nb, blk, LANES), axis=1)
    q_all = q_ref[...]
    q_hi, q_lo = _split(q_all)
    cur = lax.broadcasted_iota(jnp.int32, (1, t), 1) // blk
    d0 = (lax.broadcasted_iota(jnp.int32, (blk, blk), 1)
          - lax.broadcasted_iota(jnp.int32, (blk, blk), 0)).astype(F32)

    heads = []
    for hh in range(2):
        hmask = (lane >= HEAD_DIM * hh) & (lane < HEAD_DIM * (hh + 1))
        slope = slopes_ref[2 * hp + hh]
        km_hi, km_lo = _split(jnp.where(hmask, kmean, 0.0))
        gate = _dot_nt(km_hi, q_hi) + _dot_nt(km_hi, q_lo) + _dot_nt(km_lo, q_hi)
        selb = _moba_select_bias([gate[n:n + 1, :] for n in range(nb)], cur)
        heads.append((hmask, slope, selb, slope * d0, slice(HEAD_DIM * hh, HEAD_DIM * (hh + 1))))

    for i in range(nb):
        qcols = slice(i * blk, (i + 1) * blk)
        n_keys = (i + 1) * blk
        q_blk = q_ref[qcols, :] * (HEAD_DIM ** -0.5)
        s_alls = [_dot_nt(kb_ref[0:n_keys, :], jnp.where(hmask, q_blk, 0.0).astype(BF16))
                  for hmask, *_ in heads]
        tiles = []
        for (hmask, slope, selb, slope_d0, rows), s_all in zip(heads, s_alls):
            per_head = []
            for j in range(i + 1):
                s = s_all[j * blk:(j + 1) * blk, :] - (slope_d0 + slope * float((i - j) * blk))
                if j < i:
                    s = s + selb[j][:, qcols]
                else:
                    s = jnp.where(d0 >= 0.0, s, NEG_INF)
                per_head.append(s)
            tiles.append(per_head)
        maxes = []
        for per_head in tiles:
            m = jnp.max(per_head[0], axis=0, keepdims=True)
            for s in per_head[1:]:
                m = jnp.maximum(m, jnp.max(s, axis=0, keepdims=True))
            maxes.append(m)
        probs = [[jnp.exp(s - m) for s in per_head] for per_head, m in zip(tiles, maxes)]
        sums = []
        for per_head in probs:
            l = jnp.sum(per_head[0], axis=0, keepdims=True)
            for p in per_head[1:]:
                l = l + jnp.sum(p, axis=0, keepdims=True)
            sums.append(l)
        for (hmask, slope, selb, slope_d0, rows), per_head, l in zip(heads, probs, sums):
            p_all = jnp.concatenate([p.astype(BF16) for p in per_head], axis=0)
            res = _dot(vt_ref[:, 0:n_keys], p_all) / l
            ot_ref[rows, qcols] = res[rows, :]

    o_ref[...] = ot_ref[...].T


def _moba_prompt(q, k_t, v_t, slopes):
    b, t, _ = q.shape
    spec = pl.BlockSpec((None, t, LANES), lambda bi, hp, *_: (bi, 0, hp))
    spec_t = pl.BlockSpec((None, LANES, t), lambda bi, hp, *_: (bi, hp, 0))
    grid_spec = pltpu.PrefetchScalarGridSpec(
        num_scalar_prefetch=1,
        grid=(b, ATT_W // LANES),
        in_specs=[spec, spec_t, spec_t],
        out_specs=spec,
        scratch_shapes=[
            pltpu.VMEM((t, LANES), BF16),
            pltpu.VMEM((LANES, t), BF16),
            pltpu.VMEM((LANES, t), F32),
        ],
    )
    return pl.pallas_call(
        _moba_prompt_kernel,
        grid_spec=grid_spec,
        out_shape=jax.ShapeDtypeStruct((b, t, ATT_W), F32),
        compiler_params=_cparams("parallel", "parallel"),
        name="moba_prompt",
    )(slopes, q, k_t, v_t)


def _moba_sample_kernel(pt_ref, slopes_ref, q_ref, kn_ref, vn_ref, *rest, n_pages):
    kt_refs = rest[:n_pages]
    vt_refs = rest[n_pages:2 * n_pages]
    o_ref = rest[2 * n_pages]
    nq = q_ref.shape[0]
    n_past = n_pages * PAGE_SIZE
    nb = n_past // MOBA_BLOCK
    pages_per_block = MOBA_BLOCK // PAGE_SIZE
    rows = N_HEADS * nq
    scale = HEAD_DIM ** -0.5
    assert rows % SUBLANES == 0 and nq <= PAGE_SIZE

    q = q_ref[...]
    r_id = lax.broadcasted_iota(jnp.int32, (rows, ATT_W), 0)
    c_id = lax.broadcasted_iota(jnp.int32, (rows, ATT_W), 1)
    q_rep = jnp.concatenate([q] * (rows // nq), axis=0)
    own = c_id // HEAD_DIM == r_id // nq
    qs = jnp.where(own, q_rep * scale, 0.0).astype(BF16)

    r_col = lax.broadcasted_iota(jnp.int32, (rows, 1), 0)
    tq = (r_col % nq).astype(F32)
    slope = jnp.zeros((rows, 1), F32)
    for h in range(N_HEADS):
        slope = jnp.where(r_col // nq == h, slopes_ref[h], slope)
    lane = lax.broadcasted_iota(jnp.int32, (1, PAGE_SIZE), 1).astype(F32)

    raw = [_dot(qs, kt_refs[pg][...].astype(BF16)) for pg in range(n_pages)]

    gates = []
    for n in range(nb):
        total = raw[n * pages_per_block]
        for pg in range(n * pages_per_block + 1, (n + 1) * pages_per_block):
            total = total + raw[pg]
        gates.append(jnp.sum(total, axis=1, keepdims=True) * (1.0 / (MOBA_BLOCK * scale)))
    bias = _moba_select_bias(gates, nb)

    tiles = []
    for pg in range(n_pages):
        dist = (tq + float(n_past - pg * PAGE_SIZE)) - lane
        tiles.append(raw[pg] - slope * dist + bias[pg // pages_per_block])
    pad_rows = lambda a: jnp.concatenate([a, jnp.zeros((PAGE_SIZE - nq, ATT_W), F32)], axis=0)
    dist_new = tq - lane
    s_new = _dot_nt(qs, pad_rows(kn_ref[...]).astype(BF16))
    tiles.append(jnp.where(dist_new >= 0.0, s_new - slope * dist_new, NEG_INF))

    top = tiles[0]
    for s in tiles[1:]:
        top = jnp.maximum(top, s)
    m = jnp.max(top, axis=1, keepdims=True)
    probs = [jnp.exp(s - m) for s in tiles]
    total = probs[0]
    for p in probs[1:]:
        total = total + p
    inv_l = 1.0 / jnp.sum(total, axis=1, keepdims=True)
    out = _dot((probs[n_pages] * inv_l).astype(BF16), pad_rows(vn_ref[...]).astype(BF16))
    for pg in range(n_pages):
        out = out + _dot_nt((probs[pg] * inv_l).astype(BF16), vt_refs[pg][...].astype(BF16))

    res = jnp.zeros((nq, ATT_W), F32)
    c_q = lax.broadcasted_iota(jnp.int32, (nq, ATT_W), 1)
    for h in range(N_HEADS):
        res = jnp.where(c_q // HEAD_DIM == h, out[h * nq:(h + 1) * nq, :], res)
    o_ref[...] = res


def _moba_sample(q, k_new, v_new, cache_k, cache_v, page_table, slopes):
    db, nq, _ = q.shape
    n_pages = page_table.shape[1]
    new_spec = pl.BlockSpec((None, nq, ATT_W), lambda s, *_: (s, 0, 0))

    def page_spec(pg):
        return pl.BlockSpec((None, ATT_W, PAGE_SIZE), lambda s, pt, sl: (pt[s, pg], 0, 0))

    grid_spec = pltpu.PrefetchScalarGridSpec(
        num_scalar_prefetch=2,
        grid=(db,),
        in_specs=([new_spec] * 3 + [page_spec(pg) for pg in range(n_pages)] * 2),
        out_specs=new_spec,
    )
    by_token = lambda c: jnp.transpose(c, (0, 2, 3, 1)).reshape(c.shape[0], ATT_W, PAGE_SIZE)
    ck, cv = by_token(cache_k), by_token(cache_v)
    return pl.pallas_call(
        functools.partial(_moba_sample_kernel, n_pages=n_pages),
        grid_spec=grid_spec,
        out_shape=jax.ShapeDtypeStruct((db, nq, ATT_W), F32),
        compiler_params=_cparams("parallel"),
        name="moba_sample",
    )(page_table, slopes, q, k_new, v_new, *([ck] * n_pages), *([cv] * n_pages))


S5_CHUNK = 256


def _s5_prep_kernel(are_ref, aim_ref, ldt_ref, bre_ref, bim_ref,
                    bbar_ref, apr_ref, api_ref):
    ar, ai = are_ref[...], aim_ref[...]
    dt = jnp.exp(ldt_ref[...])
    mag = jnp.exp(dt * ar)
    ang = dt * ai
    abr, abi = mag * jnp.cos(ang), mag * jnp.sin(ang)
    den = ar * ar + ai * ai
    nr, ni = abr - 1.0, abi
    f_re = (nr * ar + ni * ai) / den
    f_im = (ni * ar - nr * ai) / den
    bre, bim = bre_ref[...], bim_ref[...]
    bbar_ref[:, :SSM_FLAT] = (f_re * bre - f_im * bim).astype(BF16)
    bbar_ref[:, SSM_FLAT:] = (f_re * bim + f_im * bre).astype(BF16)
    apr_ref[0:1, :] = abr
    api_ref[0:1, :] = abi
    d = 1
    while d < S5_CHUNK:
        pr, pi = apr_ref[0:d, :], api_ref[0:d, :]
        er, ei = apr_ref[d - 1:d, :], api_ref[d - 1:d, :]
        apr_ref[d:2 * d, :] = pr * er - pi * ei
        api_ref[d:2 * d, :] = pr * ei + pi * er
        d *= 2


def _s5_prep(a_re, a_im, log_dt, b_re, b_im):
    eye = jnp.eye(SSM_GROUPS, dtype=F32)
    blockdiag = lambda b: jnp.einsum('gpc,gh->gchp', b, eye).reshape(SSM_W, SSM_FLAT)
    flat = lambda a: a.reshape(1, SSM_FLAT)
    ldt = jnp.broadcast_to(log_dt[:, None], (SSM_GROUPS, SSM_STATE))
    return pl.pallas_call(
        _s5_prep_kernel,
        out_shape=[jax.ShapeDtypeStruct((SSM_W, 2 * SSM_FLAT), BF16),
                   jax.ShapeDtypeStruct((S5_CHUNK, SSM_FLAT), F32),
                   jax.ShapeDtypeStruct((S5_CHUNK, SSM_FLAT), F32)],
        compiler_params=pltpu.CompilerParams(vmem_limit_bytes=VMEM_LIMIT),
        name="s5_prep",
    )(flat(a_re), flat(a_im), flat(ldt), blockdiag(b_re), blockdiag(b_im))


def _s5_readout(h_re, h_im, u, ccat_ref, d_ref, wglu_ref, bglu_ref):
    hcat = jnp.concatenate([h_re.astype(BF16), h_im.astype(BF16)], axis=1)
    y = _dot(hcat, ccat_ref[...]) + d_ref[...] * u
    z = _gelu(y)
    return z * jax.nn.sigmoid(_dot(z.astype(BF16), wglu_ref[...]) + bglu_ref[...])


def _s5_prompt_kernel(u_ref, bbar_ref, apr_ref, api_ref, a8r_ref, a8i_ref, ccat_ref, d_ref,
                      wglu_ref, bglu_ref,
                      y_ref, hre_ref, him_ref, cr_ref, ci_ref, hr_ref, hi_ref, pr_ref, pi_ref):
    c = pl.program_id(1)
    n = u_ref.shape[0]
    n_groups = n // SUBLANES

    @pl.when(c == 0)
    def _():
        cr_ref[...] = jnp.zeros_like(cr_ref)
        ci_ref[...] = jnp.zeros_like(ci_ref)

    def scan_level(hr, hi, pos, d, power):
        er, ei = apr_ref[power - 1:power, :], api_ref[power - 1:power, :]
        keep = pos >= d
        sr = jnp.where(keep, pltpu.roll(hr, d, axis=0), 0.0)
        si = jnp.where(keep, pltpu.roll(hi, d, axis=0), 0.0)
        return hr + er * sr - ei * si, hi + er * si + ei * sr

    u = u_ref[...]
    bu = _dot(u.astype(BF16), bbar_ref[...])
    hr, hi = bu[:, :SSM_FLAT], bu[:, SSM_FLAT:]
    hr = hr.reshape(n_groups, SUBLANES, SSM_FLAT)
    hi = hi.reshape(n_groups, SUBLANES, SSM_FLAT)
    in_group = lax.broadcasted_iota(jnp.int32, (SUBLANES, SSM_FLAT), 0)
    d = 1
    while d < SUBLANES:
        er = jnp.where(in_group >= d, apr_ref[d - 1:d, :], 0.0)[None]
        ei = jnp.where(in_group >= d, api_ref[d - 1:d, :], 0.0)[None]
        sr, si = pltpu.roll(hr, d, axis=1), pltpu.roll(hi, d, axis=1)
        hr, hi = hr + er * sr - ei * si, hi + er * si + ei * sr
        d *= 2
    hr = hr.reshape(n, SSM_FLAT)
    hi = hi.reshape(n, SSM_FLAT)
    n_lt = SSM_FLAT // LANES
    lt_cols = [slice(l * LANES, (l + 1) * LANES) for l in range(n_lt)]
    for l in range(n_lt):
        hr_ref[l] = hr[:, lt_cols[l]]
        hi_ref[l] = hi[:, lt_cols[l]]

    ends = pl.ds(SUBLANES - 1, n_groups, stride=SUBLANES)
    gr = jnp.concatenate([hr_ref[l, ends, :] for l in range(n_lt)], axis=1)
    gi = jnp.concatenate([hi_ref[l, ends, :] for l in range(n_lt)], axis=1)
    group = lax.broadcasted_iota(jnp.int32, (n_groups, SSM_FLAT), 0)
    d = 1
    while d < n_groups:
        gr, gi = scan_level(gr, gi, group, d, d * SUBLANES)
        d *= 2
    cr, ci = cr_ref[...], ci_ref[...]
    a8r, a8i = a8r_ref[...], a8i_ref[...]
    gr = gr + a8r * cr - a8i * ci
    gi = gi + a8r * ci + a8i * cr
    cr_ref[...] = gr[n_groups - 1:n_groups, :]
    ci_ref[...] = gi[n_groups - 1:n_groups, :]
    hre_ref[...] = gr[n_groups - 1:n_groups, :]
    him_ref[...] = gi[n_groups - 1:n_groups, :]
    first = group == 0
    pr_ref[...] = jnp.where(first, cr, pltpu.roll(gr, 1, axis=0))
    pi_ref[...] = jnp.where(first, ci, pltpu.roll(gi, 1, axis=0))

    a1r, a1i = apr_ref[0:SUBLANES, :], api_ref[0:SUBLANES, :]
    for g in range(n_groups):
        rows = slice(g * SUBLANES, (g + 1) * SUBLANES)
        br = jnp.broadcast_to(pr_ref[g:g + 1, :], (SUBLANES, SSM_FLAT))
        bi = jnp.broadcast_to(pi_ref[g:g + 1, :], (SUBLANES, SSM_FLAT))
        add_r = a1r * br - a1i * bi
        add_i = a1r * bi + a1i * br
        for l in range(n_lt):
            hr_ref[l, rows, :] = hr_ref[l, rows, :] + add_r[:, lt_cols[l]]
            hi_ref[l, rows, :] = hi_ref[l, rows, :] + add_i[:, lt_cols[l]]
    hr = jnp.concatenate([hr_ref[l] for l in range(n_lt)], axis=1)
    hi = jnp.concatenate([hi_ref[l] for l in range(n_lt)], axis=1)
    y_ref[...] = _s5_readout(hr, hi, u, ccat_ref, d_ref, wglu_ref, bglu_ref)


def _s5_prompt(u, bbar, apr, api, ccat, d_row, wglu_bf, bglu_row):
    b, t, _ = u.shape
    n = S5_CHUNK
    full = lambda a: pl.BlockSpec(a.shape, lambda bi, c: (0,) * a.ndim)
    tok = pl.BlockSpec((None, n, SSM_W), lambda bi, c: (bi, c, 0))
    st = pl.BlockSpec((None, 1, SSM_FLAT), lambda bi, c: (bi, 0, 0))
    a8r, a8i = apr[SUBLANES - 1::SUBLANES], api[SUBLANES - 1::SUBLANES]
    lane_major = pltpu.VMEM((SSM_FLAT // LANES, n, LANES), F32)
    y, hre, him = pl.pallas_call(
        _s5_prompt_kernel,
        grid=(b, t // n),
        in_specs=[tok, full(bbar), full(apr), full(api), full(a8r), full(a8i), full(ccat),
                  full(d_row), full(wglu_bf), full(bglu_row)],
        out_specs=[tok, st, st],
        out_shape=[jax.ShapeDtypeStruct((b, t, SSM_W), F32),
                   jax.ShapeDtypeStruct((b, 1, SSM_FLAT), F32),
                   jax.ShapeDtypeStruct((b, 1, SSM_FLAT), F32)],
        scratch_shapes=[pltpu.VMEM((1, SSM_FLAT), F32), pltpu.VMEM((1, SSM_FLAT), F32),
                        lane_major, lane_major,
                        pltpu.VMEM((n // SUBLANES, SSM_FLAT), F32),
                        pltpu.VMEM((n // SUBLANES, SSM_FLAT), F32)],
        compiler_params=_cparams("parallel", "arbitrary"),
        name="s5_prompt",
    )(u, bbar, apr, api, a8r, a8i, ccat, d_row, wglu_bf, bglu_row)
    return y, hre.reshape(b, SSM_GROUPS, SSM_STATE), him.reshape(b, SSM_GROUPS, SSM_STATE)


def _s5_sample_kernel(u_ref, h0r_ref, h0i_ref, bbar_ref, apr_ref, api_ref, ccat_ref, d_ref,
                      wglu_ref, bglu_ref, y_ref, hre_ref, him_ref, *, n_steps):
    hr, hi = h0r_ref[...], h0i_ref[...]
    ar, ai = apr_ref[0:1, :], api_ref[0:1, :]
    for t in range(n_steps):
        u = u_ref[:, t * SSM_W:(t + 1) * SSM_W]
        bu = _dot(u.astype(BF16), bbar_ref[...])
        hr, hi = (ar * hr - ai * hi + bu[:, :SSM_FLAT],
                  ar * hi + ai * hr + bu[:, SSM_FLAT:])
        y_ref[:, t * SSM_W:(t + 1) * SSM_W] = _s5_readout(
            hr, hi, u, ccat_ref, d_ref, wglu_ref, bglu_ref)
    hre_ref[...] = hr
    him_ref[...] = hi


def _s5_sample(u, h0_re, h0_im, bbar, apr, api, ccat, d_row, wglu_bf, bglu_row):
    db, t, _ = u.shape
    y, hre, him = pl.pallas_call(
        functools.partial(_s5_sample_kernel, n_steps=t),
        out_shape=[jax.ShapeDtypeStruct((db, t * SSM_W), F32),
                   jax.ShapeDtypeStruct((db, SSM_FLAT), F32),
                   jax.ShapeDtypeStruct((db, SSM_FLAT), F32)],
        compiler_params=pltpu.CompilerParams(vmem_limit_bytes=VMEM_LIMIT),
        name="s5_sample",
    )(u.reshape(db, t * SSM_W), h0_re.reshape(db, SSM_FLAT), h0_im.reshape(db, SSM_FLAT),
      bbar, apr, api, ccat, d_row, wglu_bf, bglu_row)
    return (y.reshape(db, t, SSM_W), hre.reshape(db, SSM_GROUPS, SSM_STATE),
            him.reshape(db, SSM_GROUPS, SSM_STATE))


MERGE_TM = 512


def _merge_kernel(x_ref, att_ref, ssm_ref, g_ref, wa_ref, wb_ref, wo_ref, lg_ref, lb_ref, o_ref):
    ya = _dot(att_ref[...].astype(BF16), wa_ref[...])
    yb = _dot(ssm_ref[...].astype(BF16), wb_ref[...])
    merged = g_ref[:, :D_MODEL] * ya + g_ref[:, D_MODEL:] * yb
    y = DN_ALPHA * x_ref[...] + _dot(merged.astype(BF16), wo_ref[...])
    o_ref[...] = _layer_norm(y, lg_ref[...], lb_ref[...])


def _merge(x, att, ssm, gates, wa_bf, wb_bf, wo_bf, lg_row, lb_row):
    n = x.shape[0]
    tm = MERGE_TM
    row = lambda w: pl.BlockSpec((tm, w), lambda i: (i, 0))
    full = lambda a: pl.BlockSpec(a.shape, lambda i: (0,) * a.ndim)
    return pl.pallas_call(
        _merge_kernel,
        grid=(n // tm,),
        in_specs=[row(D_MODEL), row(ATT_W), row(SSM_W), row(2 * D_MODEL),
                  full(wa_bf), full(wb_bf), full(wo_bf), full(lg_row), full(lb_row)],
        out_specs=row(D_MODEL),
        out_shape=jax.ShapeDtypeStruct((n, D_MODEL), F32),
        compiler_params=_cparams("parallel"),
        name="merge",
    )(x, att, ssm, gates, wa_bf, wb_bf, wo_bf, lg_row, lb_row)


PEER_TS = 512
PEER_EC = 1024
PEER_NCH = PEER_EXPERTS // PEER_EC
PEER_ROWS_PER_STEP = PEER_EC // PEER_NKEYS
HALF_KEY = PEER_DKEY // 2


def _sorting_network(n):
    pairs = []
    t = (n - 1).bit_length()
    p = 1 << (t - 1)
    while p > 0:
        q, r, d = 1 << (t - 1), 0, p
        while d > 0:
            pairs.extend((i, i + d) for i in range(n - d) if (i & p) == r)
            d, q, r = q - p, q >> 1, p
        p >>= 1
    return pairs


def _top_sorted(s, k):
    assert s.shape[0] == k * SUBLANES and k & (k - 1) == 0
    v = [s[i * SUBLANES:(i + 1) * SUBLANES, :] for i in range(k)]
    for i, j in _sorting_network(k):
        v[i], v[j] = jnp.maximum(v[i], v[j]), jnp.minimum(v[i], v[j])
    shift = SUBLANES // 2
    while shift:
        other = [pltpu.roll(x, shift, axis=0) for x in v]
        v = [jnp.maximum(v[i], other[k - 1 - i]) for i in range(k)]
        stride = k // 2
        while stride:
            for i in range(k):
                if (i // stride) % 2 == 0:
                    lo, hi = v[i], v[i + stride]
                    v[i], v[i + stride] = jnp.maximum(lo, hi), jnp.minimum(lo, hi)
            stride //= 2
        shift //= 2
    return v


def _peer_tables(s1, s2):
    k = PEER_TOPK
    c = s1.shape[1]
    groups3 = (k, SUBLANES, c)
    v1 = _top_sorted(s1, k)
    v2 = _top_sorted(s2, k)
    s2g = s2.reshape(groups3)
    rank2 = jnp.zeros(groups3, F32)
    for b in range(k):
        rank2 = jnp.where(s2g < v2[b][None], float(b + 1), rank2)
    sub = lax.broadcasted_iota(jnp.int32, (SUBLANES, c), 0)

    def pack(rows):
        out = rows[0]
        for r in range(1, SUBLANES):
            out = jnp.where(sub == r, rows[r], out)
        return out

    v2_lo, v2_hi, v1_hi = pack(v2[:SUBLANES]), pack(v2[SUBLANES:]), pack(v1[SUBLANES:])
    cands = [v1[0] + v2_lo, v1[0] + v2_hi, v1[1] + v2_lo]
    for a in range(2, SUBLANES):
        cands.append(jnp.where(sub < k // (a + 1), v1[a] + v2_lo, NEG_INF))
    cands.append(v1_hi + v2[0])
    cands += [jnp.full((SUBLANES, c), NEG_INF, F32)] * (k - len(cands))
    sc = _top_sorted(jnp.concatenate(cands, axis=0), k)
    tau = sc[k - 1]
    z = jnp.zeros_like(tau)
    for r in range(k):
        z = z + jnp.exp(sc[r] - sc[0])
    s1g = s1.reshape(groups3)
    count = jnp.zeros(groups3, F32)
    for a in range(k):
        n_sel = jnp.zeros_like(tau)
        for b in range(k // (a + 1)):
            n_sel = n_sel + jnp.where(v1[a] + v2[b] >= tau, 1.0, 0.0)
        count = jnp.where(s1g == v1[a][None], n_sel[None], count)
    row_weight = jnp.exp(s1g - v1[0][None]) / z[None]
    p2 = jnp.exp(s2g - v2[0][None])
    flat = lambda a: a.reshape(k * SUBLANES, c)
    return flat(rank2), flat(p2), flat(count), flat(row_weight)


def _peer_kernel(x_ref, wq_ref, k1_ref, k2_ref,
                 u0_ref, un_ref, vp_ref, vl_ref, lg_ref, lb_ref, o_ref,
                 xb_ref, s_ref, r2_ref, p2_ref, cnt_ref, cw_ref, acc_ref, st_ref, ht_ref, w_ref):
    c = pl.program_id(1)
    ts = x_ref.shape[0]
    groups = PEER_NKEYS // BF16_ROWS
    cur = c % 2
    nxt = 1 - cur

    @pl.when(c == 0)
    def _prologue():
        xb = x_ref[...].astype(BF16)
        xb_ref[...] = xb
        st_ref[0] = _dot_nt(u0_ref[...], xb)
        qt = _dot_nt(wq_ref[...], xb).astype(BF16)
        for h in range(PEER_HEADS):
            base = h * PEER_DKEY
            s_ref[0] = _dot(k1_ref[h], qt[base:base + HALF_KEY, :])
            s_ref[1] = _dot(k2_ref[h], qt[base + HALF_KEY:base + PEER_DKEY, :])

            def lane_tile(lt, _):
                cols = pl.ds(pl.multiple_of(lt * LANES, LANES), LANES)
                rank2, p2, count, row_weight = _peer_tables(s_ref[0, :, cols], s_ref[1, :, cols])
                r2_ref[h, :, :, cols] = rank2.astype(BF16).reshape(groups, BF16_ROWS, LANES)
                p2_ref[h, :, :, cols] = p2.astype(BF16).reshape(groups, BF16_ROWS, LANES)
                cnt_ref[h, :, cols] = count
                cw_ref[h, :, cols] = row_weight
                return 0

            lax.fori_loop(0, ts // LANES, lane_tile, 0)
        acc_ref[...] = jnp.zeros_like(acc_ref)
        ht_ref[1] = jnp.zeros((PEER_EC, ts), BF16)

    i1_base = pl.multiple_of(c * PEER_ROWS_PER_STEP, PEER_ROWS_PER_STEP)
    cnts = [cnt_ref[h, pl.ds(i1_base, PEER_ROWS_PER_STEP), :] for h in range(PEER_HEADS)]
    cws = [cw_ref[h, pl.ds(i1_base, PEER_ROWS_PER_STEP), :] for h in range(PEER_HEADS)]
    mblk = 2 * PEER_NKEYS
    for j in range(PEER_ROWS_PER_STEP):
        rows = slice(j * PEER_NKEYS, (j + 1) * PEER_NKEYS)
        w = jnp.zeros((groups, BF16_ROWS, ts), BF16)
        for h in range(PEER_HEADS):
            cnt = jnp.broadcast_to(cnts[h][j:j + 1, :], (BF16_ROWS, ts)).astype(BF16)
            cw = jnp.broadcast_to(cws[h][j:j + 1, :], (BF16_ROWS, ts)).astype(BF16)
            w = w + jnp.where(r2_ref[h] < cnt[None], cw[None], 0.0) * p2_ref[h]
        w_ref[rows, :] = w.reshape(PEER_NKEYS, ts)
        if j % 2 == 1:
            er = slice((j // 2) * mblk, (j // 2 + 1) * mblk)
            st_ref[nxt, er, :] = _dot_nt(un_ref[er, :], xb_ref[...])
    acc_ref[...] += _dot(vp_ref[...], ht_ref[nxt])
    for j in range(PEER_ROWS_PER_STEP):
        rows = slice(j * PEER_NKEYS, (j + 1) * PEER_NKEYS)
        ht_ref[cur, rows, :] = w_ref[rows, :] * _gelu(st_ref[cur, rows, :].astype(BF16))

    @pl.when(c == PEER_NCH - 1)
    def _epilogue():
        acc = acc_ref[...] + _dot(vl_ref[...], ht_ref[cur])
        y = DN_ALPHA * x_ref[...] + acc.T
        o_ref[...] = _layer_norm(y, lg_ref[...], lb_ref[...])


def _peer(x1, wq_bf, k1_bf, k2_bf, u_bf, vt_bf, lg_row, lb_row):
    n = x1.shape[0]
    ts = PEER_TS
    last = PEER_NCH - 1
    full = lambda a: pl.BlockSpec(a.shape, lambda i, c: (0,) * a.ndim)
    tok = pl.BlockSpec((ts, D_MODEL), lambda i, c: (i, 0))
    tables = pltpu.VMEM((PEER_HEADS, PEER_NKEYS, ts), F32)
    packed = pltpu.VMEM((PEER_HEADS, PEER_NKEYS // BF16_ROWS, BF16_ROWS, ts), BF16)
    return pl.pallas_call(
        _peer_kernel,
        grid=(n // ts, PEER_NCH),
        in_specs=[tok, full(wq_bf), full(k1_bf), full(k2_bf),
                  pl.BlockSpec((PEER_EC, D_MODEL), lambda i, c: (0, 0)),
                  pl.BlockSpec((PEER_EC, D_MODEL), lambda i, c: (jnp.minimum(c + 1, last), 0)),
                  pl.BlockSpec((D_MODEL, PEER_EC), lambda i, c: (0, jnp.maximum(c - 1, 0))),
                  pl.BlockSpec((D_MODEL, PEER_EC), lambda i, c: (0, last)),
                  full(lg_row), full(lb_row)],
        out_specs=tok,
        out_shape=jax.ShapeDtypeStruct((n, D_MODEL), F32),
        scratch_shapes=[
            pltpu.VMEM((ts, D_MODEL), BF16),
            pltpu.VMEM((2, PEER_NKEYS, ts), F32),
            packed, packed,
            tables, tables,
            pltpu.VMEM((D_MODEL, ts), F32),
            pltpu.VMEM((2, PEER_EC, ts), F32),
            pltpu.VMEM((2, PEER_EC, ts), BF16),
            pltpu.VMEM((PEER_EC, ts), BF16),
        ],
        compiler_params=_cparams("parallel", "arbitrary"),
        name="peer",
    )(x1, wq_bf, k1_bf, k2_bf, u_bf, u_bf, vt_bf, vt_bf, lg_row, lb_row)


def kernel(x_prompt, x_sample, cache_k, cache_v, state_ssm_re, state_ssm_im, page_table, w_in, b_in, w_a, w_b, w_o, ln1_g, ln1_b, a_re, a_im, log_dt, b_re, b_im, c_re, c_im, d_skip, w_glu, b_glu, ln2_g, ln2_b, w_pq, sub_k1, sub_k2, peer_u, peer_v):
    bn, t, _ = x_prompt.shape
    db, dt_, _ = x_sample.shape
    row = lambda a: a.reshape(1, -1).astype(F32)
    slopes = jnp.exp2(-8.0 * (jnp.arange(N_HEADS, dtype=F32) + 1.0) / N_HEADS)

    w_in_bf = w_in.astype(BF16)
    proj_w = (w_in_bf, row(b_in), w_in_bf[:, ATT_W:3 * ATT_W].T,
              b_in[ATT_W:3 * ATT_W].reshape(2 * ATT_W, 1).astype(F32))
    wa_bf, wb_bf, wo_bf, wglu_bf = (w.astype(BF16) for w in (w_a, w_b, w_o, w_glu))
    eye = jnp.eye(SSM_GROUPS, dtype=F32)
    c_blk = lambda cm: jnp.einsum('gcp,gh->gphc', cm, eye).reshape(SSM_FLAT, SSM_W)
    ccat = jnp.concatenate([c_blk(c_re), -c_blk(c_im)], axis=0).astype(BF16)
    peer_w = (w_pq.T.astype(BF16), sub_k1.astype(BF16), sub_k2.astype(BF16))
    u_bf = peer_u.astype(BF16)
    vt_bf = peer_v.T.astype(BF16)

    bbar, apr, api = _s5_prep(a_re, a_im, log_dt, b_re, b_im)
    s5_w = (bbar, apr, api, ccat, row(d_skip), wglu_bf, row(b_glu))

    xp = x_prompt.reshape(bn * t, D_MODEL)
    q, k_t, v_t, u, gates = _proj(xp, *proj_w, seq_len=t)
    att = _moba_prompt(q.reshape(bn, t, ATT_W), k_t, v_t, slopes)
    ssm, hre_p, him_p = _s5_prompt(u.reshape(bn, t, SSM_W), *s5_w)
    x1 = _merge(xp, att.reshape(bn * t, ATT_W), ssm.reshape(bn * t, SSM_W), gates,
                wa_bf, wb_bf, wo_bf, row(ln1_g), row(ln1_b))
    y_prompt = _peer(x1, *peer_w, u_bf, vt_bf, row(ln2_g), row(ln2_b)).reshape(bn, t, D_MODEL)
    per_token = lambda a: a.reshape(bn, N_HEADS, HEAD_DIM, t).transpose(0, 3, 1, 2)
    k_prompt, v_prompt = per_token(k_t), per_token(v_t)

    xs = x_sample.reshape(db * dt_, D_MODEL)
    q, k, v, u, gates = _proj(xs, *proj_w)
    att = _moba_sample(q.reshape(db, dt_, ATT_W), k.reshape(db, dt_, ATT_W), v.reshape(db, dt_, ATT_W),
                       cache_k, cache_v, page_table, slopes)
    ssm, hre_s, him_s = _s5_sample(u.reshape(db, dt_, SSM_W), state_ssm_re, state_ssm_im, *s5_w)
    x1 = _merge(xs, att.reshape(db * dt_, ATT_W), ssm.reshape(db * dt_, SSM_W), gates,
                wa_bf, wb_bf, wo_bf, row(ln1_g), row(ln1_b))
    y_sample = _peer(x1, *peer_w, u_bf, vt_bf, row(ln2_g), row(ln2_b)).reshape(db, dt_, D_MODEL)
    k_sample = k.reshape(db, dt_, N_HEADS, HEAD_DIM)
    v_sample = v.reshape(db, dt_, N_HEADS, HEAD_DIM)

    return (y_prompt, y_sample, k_prompt, v_prompt, k_sample, v_sample,
            hre_p, him_p, hre_s, him_s)
```

```python
import functools
import math

import jax
import jax.numpy as jnp
from jax import lax
from jax.experimental import pallas as pl
from jax.experimental.pallas import tpu as pltpu

F32 = jnp.float32
BF16 = jnp.bfloat16

D_MODEL = 1024
ATT_W = 512
HEAD_DIM = 64
N_HEADS = 8
MOBA_BLOCK = 256
MOBA_TOPK = 3
SSM_W = 512
SSM_GROUP = 16
SSM_GROUPS = 32
SSM_STATE = 64
SSM_FLAT = SSM_GROUPS * SSM_STATE
PEER_HEADS = 8
PEER_NKEYS = 128
PEER_EXPERTS = PEER_NKEYS * PEER_NKEYS
PEER_DKEY = 256
PEER_TOPK = 16
PROJ_W = 3 * ATT_W + SSM_W + 2 * D_MODEL
DEPTH = 1
DN_ALPHA = (2.0 * DEPTH) ** 0.25
LN_EPS = 1e-5
PAGE_SIZE = 128

LANES = 128
SUBLANES = 8
BF16_ROWS = 16
NEG_INF = float("-inf")
VMEM_LIMIT = 56 * 1024 * 1024

NT_DIMS = (((1,), (1,)), ((), ()))


def _cparams(*sem):
    return pltpu.CompilerParams(dimension_semantics=sem, vmem_limit_bytes=VMEM_LIMIT)


def _dot(a, b):
    return jnp.dot(a, b, preferred_element_type=F32)


def _dot_nt(a, b):
    return lax.dot_general(a, b, NT_DIMS, preferred_element_type=F32)


def _split(x):
    hi = x.astype(BF16)
    lo = (x - hi.astype(F32)).astype(BF16)
    return hi, lo


def _gelu(x):
    c = math.sqrt(2.0 / math.pi)
    return 0.5 * x * (1.0 + jnp.tanh(c * (x + 0.044715 * (x * x * x))))


def _layer_norm(y, g, b):
    mu = jnp.mean(y, axis=-1, keepdims=True)
    yc = y - mu
    var = jnp.mean(yc * yc, axis=-1, keepdims=True)
    return yc * lax.rsqrt(var + LN_EPS) * g + b


PROJ_TM = 512


def _proj_kernel(x_ref, w_ref, b_ref, wkv_t_ref, bkv_col_ref, q_ref, k_ref, v_ref, u_ref, g_ref,
                 *, kv_transposed):
    xb = x_ref[...].astype(BF16)

    def seg(lo, hi):
        return _dot(xb, w_ref[:, lo:hi]) + b_ref[:, lo:hi]

    q_ref[...] = seg(0, ATT_W)
    if kv_transposed:
        kv_t = _dot_nt(wkv_t_ref[...], xb) + bkv_col_ref[...]
        k_ref[...] = kv_t[:ATT_W, :]
        v_ref[...] = kv_t[ATT_W:, :]
    else:
        k_ref[...] = seg(ATT_W, 2 * ATT_W)
        v_ref[...] = seg(2 * ATT_W, 3 * ATT_W)
    u_ref[...] = seg(3 * ATT_W, 3 * ATT_W + SSM_W)
    g_ref[...] = jax.nn.sigmoid(seg(3 * ATT_W + SSM_W, PROJ_W))


def _proj(x, w_bf, b_row, wkv_t, bkv_col, seq_len=None):
    n = x.shape[0]
    tm = PROJ_TM
    row = lambda w: pl.BlockSpec((tm, w), lambda i: (i, 0))
    full = lambda a: pl.BlockSpec(a.shape, lambda i: (0,) * a.ndim)
    if seq_len is None:
        kv_spec, kv_shape = row(ATT_W), jax.ShapeDtypeStruct((n, ATT_W), F32)
    else:
        per_seq = seq_len // tm
        kv_spec = pl.BlockSpec((None, ATT_W, tm), lambda i: (i // per_seq, 0, i % per_seq))
        kv_shape = jax.ShapeDtypeStruct((n // seq_len, ATT_W, seq_len), F32)
    rows = lambda w: jax.ShapeDtypeStruct((n, w), F32)
    return pl.pallas_call(
        functools.partial(_proj_kernel, kv_transposed=seq_len is not None),
        grid=(n // tm,),
        in_specs=[row(D_MODEL), full(w_bf), full(b_row), full(wkv_t), full(bkv_col)],
        out_specs=[row(ATT_W), kv_spec, kv_spec, row(SSM_W), row(2 * D_MODEL)],
        out_shape=[rows(ATT_W), kv_shape, kv_shape, rows(SSM_W), rows(2 * D_MODEL)],
        compiler_params=_cparams("parallel"),
        name="proj",
    )(x, w_bf, b_row, wkv_t, bkv_col)


def _moba_select_bias(gates, cur):
    nb = len(gates)
    g = [jnp.where(n < cur, gates[n], NEG_INF) for n in range(nb)]
    bias = []
    for n in range(nb):
        rank = jnp.zeros(gates[n].shape, jnp.int32)
        for m in range(nb):
            if m != n:
                ahead = (g[m] >= g[n]) if m < n else (g[m] > g[n])
                rank = rank + ahead.astype(jnp.int32)
        sel = (n < cur) & (rank < MOBA_TOPK)
        bias.append(jnp.where(sel, 0.0, NEG_INF))
    return bias


def _moba_prompt_kernel(slopes_ref, q_ref, kt_ref, vt_in_ref, o_ref,
                        kb_ref, vt_ref, ot_ref):
    t = q_ref.shape[0]
    nb = t // MOBA_BLOCK
    blk = MOBA_BLOCK
    hp = pl.program_id(1)
    lane = lax.broadcasted_iota(jnp.int32, (1, LANES), 1)

    k = kt_ref[...].T
    kb_ref[...] = k.astype(BF16)
    vt_ref[...] = vt_in_ref[...].astype(BF16)
    kmean = jnp.mean(k.reshape(nb, blk, LANES), axis=1)
    q_all = q_ref[...]
    q_hi, q_lo = _split(q_all)
    cur = lax.broadcasted_iota(jnp.int32, (1, t), 1) // blk
    d0 = (lax.broadcasted_iota(jnp.int32, (blk, blk), 1)
          - lax.broadcasted_iota(jnp.int32, (blk, blk), 0)).astype(F32)

    heads = []
    for hh in range(2):
        hmask = (lane >= HEAD_DIM * hh) & (lane < HEAD_DIM * (hh + 1))
        slope = slopes_ref[2 * hp + hh]
        km_hi, km_lo = _split(jnp.where(hmask, kmean, 0.0))
        gate = _dot_nt(km_hi, q_hi) + _dot_nt(km_hi, q_lo) + _dot_nt(km_lo, q_hi)
        selb = _moba_select_bias([gate[n:n + 1, :] for n in range(nb)], cur)
        heads.append((hmask, slope, selb, slope * d0, slice(HEAD_DIM * hh, HEAD_DIM * (hh + 1))))

    for i in range(nb):
        qcols = slice(i * blk, (i + 1) * blk)
        n_keys = (i + 1) * blk
        q_blk = q_ref[qcols, :] * (HEAD_DIM ** -0.5)
        s_alls = [_dot_nt(kb_ref[0:n_keys, :], jnp.where(hmask, q_blk, 0.0).astype(BF16))
                  for hmask, *_ in heads]
        tiles = []
        for (hmask, slope, selb, slope_d0, rows), s_all in zip(heads, s_alls):
            per_head = []
            for j in range(i + 1):
                s = s_all[j * blk:(j + 1) * blk, :] - (slope_d0 + slope * float((i - j) * blk))
                if j < i:
                    s = s + selb[j][:, qcols]
                else:
                    s = jnp.where(d0 >= 0.0, s, NEG_INF)
                per_head.append(s)
            tiles.append(per_head)
        maxes = []
        for per_head in tiles:
            m = jnp.max(per_head[0], axis=0, keepdims=True)
            for s in per_head[1:]:
                m = jnp.maximum(m, jnp.max(s, axis=0, keepdims=True))
            maxes.append(m)
        probs = [[jnp.exp(s - m) for s in per_head] for per_head, m in zip(tiles, maxes)]
        sums = []
        for per_head in probs:
            l = jnp.sum(per_head[0], axis=0, keepdims=True)
            for p in per_head[1:]:
                l = l + jnp.sum(p, axis=0, keepdims=True)
            sums.append(l)
        for (hmask, slope, selb, slope_d0, rows), per_head, l in zip(heads, probs, sums):
            p_all = jnp.concatenate([p.astype(BF16) for p in per_head], axis=0)
            res = _dot(vt_ref[:, 0:n_keys], p_all) / l
            ot_ref[rows, qcols] = res[rows, :]

    o_ref[...] = ot_ref[...].T.astype(o_ref.dtype)


def _moba_prompt(q, k_t, v_t, slopes):
    b, t, _ = q.shape
    spec = pl.BlockSpec((None, t, LANES), lambda bi, hp, *_: (bi, 0, hp))
    spec_t = pl.BlockSpec((None, LANES, t), lambda bi, hp, *_: (bi, hp, 0))
    grid_spec = pltpu.PrefetchScalarGridSpec(
        num_scalar_prefetch=1,
        grid=(b, ATT_W // LANES),
        in_specs=[spec, spec_t, spec_t],
        out_specs=spec,
        scratch_shapes=[
            pltpu.VMEM((t, LANES), BF16),
            pltpu.VMEM((LANES, t), BF16),
            pltpu.VMEM((LANES, t), F32),
        ],
    )
    return pl.pallas_call(
        _moba_prompt_kernel,
        grid_spec=grid_spec,
        out_shape=jax.ShapeDtypeStruct((b, t, ATT_W), BF16),
        compiler_params=_cparams("parallel", "parallel"),
        name="moba_prompt",
    )(slopes, q, k_t, v_t)


def _moba_sample_kernel(pt_ref, slopes_ref, q_ref, kn_ref, vn_ref, *rest, n_pages):
    kt_refs = rest[:n_pages]
    vt_refs = rest[n_pages:2 * n_pages]
    o_ref = rest[2 * n_pages]
    nq = q_ref.shape[0]
    n_past = n_pages * PAGE_SIZE
    nb = n_past // MOBA_BLOCK
    pages_per_block = MOBA_BLOCK // PAGE_SIZE
    rows = N_HEADS * nq
    scale = HEAD_DIM ** -0.5
    assert rows % SUBLANES == 0 and nq <= PAGE_SIZE

    q = q_ref[...]
    r_id = lax.broadcasted_iota(jnp.int32, (rows, ATT_W), 0)
    c_id = lax.broadcasted_iota(jnp.int32, (rows, ATT_W), 1)
    q_rep = jnp.concatenate([q] * (rows // nq), axis=0)
    own = c_id // HEAD_DIM == r_id // nq
    qs = jnp.where(own, q_rep * scale, 0.0).astype(BF16)

    r_col = lax.broadcasted_iota(jnp.int32, (rows, 1), 0)
    tq = (r_col % nq).astype(F32)
    slope = jnp.zeros((rows, 1), F32)
    for h in range(N_HEADS):
        slope = jnp.where(r_col // nq == h, slopes_ref[h], slope)
    lane = lax.broadcasted_iota(jnp.int32, (1, PAGE_SIZE), 1).astype(F32)

    raw = [_dot(qs, kt_refs[pg][...].astype(BF16)) for pg in range(n_pages)]

    gates = []
    for n in range(nb):
        total = raw[n * pages_per_block]
        for pg in range(n * pages_per_block + 1, (n + 1) * pages_per_block):
            total = total + raw[pg]
        gates.append(jnp.sum(total, axis=1, keepdims=True) * (1.0 / (MOBA_BLOCK * scale)))
    bias = _moba_select_bias(gates, nb)

    tiles = []
    for pg in range(n_pages):
        dist = (tq + float(n_past - pg * PAGE_SIZE)) - lane
        tiles.append(raw[pg] - slope * dist + bias[pg // pages_per_block])
    pad_rows = lambda a: jnp.concatenate([a, jnp.zeros((PAGE_SIZE - nq, ATT_W), F32)], axis=0)
    dist_new = tq - lane
    s_new = _dot_nt(qs, pad_rows(kn_ref[...]).astype(BF16))
    tiles.append(jnp.where(dist_new >= 0.0, s_new - slope * dist_new, NEG_INF))

    top = tiles[0]
    for s in tiles[1:]:
        top = jnp.maximum(top, s)
    m = jnp.max(top, axis=1, keepdims=True)
    probs = [jnp.exp(s - m) for s in tiles]
    total = probs[0]
    for p in probs[1:]:
        total = total + p
    inv_l = 1.0 / jnp.sum(total, axis=1, keepdims=True)
    out = _dot((probs[n_pages] * inv_l).astype(BF16), pad_rows(vn_ref[...]).astype(BF16))
    for pg in range(n_pages):
        out = out + _dot_nt((probs[pg] * inv_l).astype(BF16), vt_refs[pg][...].astype(BF16))

    res = jnp.zeros((nq, ATT_W), F32)
    c_q = lax.broadcasted_iota(jnp.int32, (nq, ATT_W), 1)
    for h in range(N_HEADS):
        res = jnp.where(c_q // HEAD_DIM == h, out[h * nq:(h + 1) * nq, :], res)
    o_ref[...] = res.astype(o_ref.dtype)


def _moba_sample(q, k_new, v_new, cache_k, cache_v, page_table, slopes):
    db, nq, _ = q.shape
    n_pages = page_table.shape[1]
    new_spec = pl.BlockSpec((None, nq, ATT_W), lambda s, *_: (s, 0, 0))

    def page_spec(pg):
        return pl.BlockSpec((None, ATT_W, PAGE_SIZE), lambda s, pt, sl: (pt[s, pg], 0, 0))

    grid_spec = pltpu.PrefetchScalarGridSpec(
        num_scalar_prefetch=2,
        grid=(db,),
        in_specs=([new_spec] * 3 + [page_spec(pg) for pg in range(n_pages)] * 2),
        out_specs=new_spec,
    )
    by_token = lambda c: jnp.transpose(c, (0, 2, 3, 1)).reshape(c.shape[0], ATT_W, PAGE_SIZE)
    ck, cv = by_token(cache_k), by_token(cache_v)
    return pl.pallas_call(
        functools.partial(_moba_sample_kernel, n_pages=n_pages),
        grid_spec=grid_spec,
        out_shape=jax.ShapeDtypeStruct((db, nq, ATT_W), BF16),
        compiler_params=_cparams("parallel"),
        name="moba_sample",
    )(page_table, slopes, q, k_new, v_new, *([ck] * n_pages), *([cv] * n_pages))


S5_CHUNK = 256


def _s5_prep_kernel(are_ref, aim_ref, ldt_ref, bre_ref, bim_ref,
                    bbar_ref, apr_ref, api_ref):
    ar, ai = are_ref[...], aim_ref[...]
    dt = jnp.exp(ldt_ref[...])
    mag = jnp.exp(dt * ar)
    ang = dt * ai
    abr, abi = mag * jnp.cos(ang), mag * jnp.sin(ang)
    den = ar * ar + ai * ai
    nr, ni = abr - 1.0, abi
    f_re = (nr * ar + ni * ai) / den
    f_im = (ni * ar - nr * ai) / den
    bre, bim = bre_ref[...], bim_ref[...]
    bbar_ref[:, :SSM_FLAT] = (f_re * bre - f_im * bim).astype(BF16)
    bbar_ref[:, SSM_FLAT:] = (f_re * bim + f_im * bre).astype(BF16)
    apr_ref[0:1, :] = abr
    api_ref[0:1, :] = abi
    d = 1
    while d < S5_CHUNK:
        pr, pi = apr_ref[0:d, :], api_ref[0:d, :]
        er, ei = apr_ref[d - 1:d, :], api_ref[d - 1:d, :]
        apr_ref[d:2 * d, :] = pr * er - pi * ei
        api_ref[d:2 * d, :] = pr * ei + pi * er
        d *= 2


def _s5_prep(a_re, a_im, log_dt, b_re, b_im):
    eye = jnp.eye(SSM_GROUPS, dtype=F32)
    blockdiag = lambda b: jnp.einsum('gpc,gh->gchp', b, eye).reshape(SSM_W, SSM_FLAT)
    flat = lambda a: a.reshape(1, SSM_FLAT)
    ldt = jnp.broadcast_to(log_dt[:, None], (SSM_GROUPS, SSM_STATE))
    return pl.pallas_call(
        _s5_prep_kernel,
        out_shape=[jax.ShapeDtypeStruct((SSM_W, 2 * SSM_FLAT), BF16),
                   jax.ShapeDtypeStruct((S5_CHUNK, SSM_FLAT), F32),
                   jax.ShapeDtypeStruct((S5_CHUNK, SSM_FLAT), F32)],
        compiler_params=pltpu.CompilerParams(vmem_limit_bytes=VMEM_LIMIT),
        name="s5_prep",
    )(flat(a_re), flat(a_im), flat(ldt), blockdiag(b_re), blockdiag(b_im))


def _s5_readout(h_re, h_im, u, ccat_ref, d_ref, wglu_ref, bglu_ref):
    hcat = jnp.concatenate([h_re.astype(BF16), h_im.astype(BF16)], axis=1)
    y = _dot(hcat, ccat_ref[...]) + d_ref[...] * u
    z = _gelu(y)
    return z * jax.nn.sigmoid(_dot(z.astype(BF16), wglu_ref[...]) + bglu_ref[...])


def _s5_prompt_kernel(u_ref, bbar_ref, apr_ref, api_ref, a8r_ref, a8i_ref, ccat_ref, d_ref,
                      wglu_ref, bglu_ref,
                      y_ref, hre_ref, him_ref, cr_ref, ci_ref, hr_ref, hi_ref, pr_ref, pi_ref):
    c = pl.program_id(1)
    n = u_ref.shape[0]
    n_groups = n // SUBLANES

    @pl.when(c == 0)
    def _():
        cr_ref[...] = jnp.zeros_like(cr_ref)
        ci_ref[...] = jnp.zeros_like(ci_ref)

    def scan_level(hr, hi, pos, d, power):
        er, ei = apr_ref[power - 1:power, :], api_ref[power - 1:power, :]
        keep = pos >= d
        sr = jnp.where(keep, pltpu.roll(hr, d, axis=0), 0.0)
        si = jnp.where(keep, pltpu.roll(hi, d, axis=0), 0.0)
        return hr + er * sr - ei * si, hi + er * si + ei * sr

    u = u_ref[...]
    bu = _dot(u.astype(BF16), bbar_ref[...])
    hr, hi = bu[:, :SSM_FLAT], bu[:, SSM_FLAT:]
    hr = hr.reshape(n_groups, SUBLANES, SSM_FLAT)
    hi = hi.reshape(n_groups, SUBLANES, SSM_FLAT)
    in_group = lax.broadcasted_iota(jnp.int32, (SUBLANES, SSM_FLAT), 0)
    d = 1
    while d < SUBLANES:
        er = jnp.where(in_group >= d, apr_ref[d - 1:d, :], 0.0)[None]
        ei = jnp.where(in_group >= d, api_ref[d - 1:d, :], 0.0)[None]
        sr, si = pltpu.roll(hr, d, axis=1), pltpu.roll(hi, d, axis=1)
        hr, hi = hr + er * sr - ei * si, hi + er * si + ei * sr
        d *= 2
    hr = hr.reshape(n, SSM_FLAT)
    hi = hi.reshape(n, SSM_FLAT)
    n_lt = SSM_FLAT // LANES
    lt_cols = [slice(l * LANES, (l + 1) * LANES) for l in range(n_lt)]
    for l in range(n_lt):
        hr_ref[l] = hr[:, lt_cols[l]]
        hi_ref[l] = hi[:, lt_cols[l]]

    ends = pl.ds(SUBLANES - 1, n_groups, stride=SUBLANES)
    gr = jnp.concatenate([hr_ref[l, ends, :] for l in range(n_lt)], axis=1)
    gi = jnp.concatenate([hi_ref[l, ends, :] for l in range(n_lt)], axis=1)
    group = lax.broadcasted_iota(jnp.int32, (n_groups, SSM_FLAT), 0)
    d = 1
    while d < n_groups:
        gr, gi = scan_level(gr, gi, group, d, d * SUBLANES)
        d *= 2
    cr, ci = cr_ref[...], ci_ref[...]
    a8r, a8i = a8r_ref[...], a8i_ref[...]
    gr = gr + a8r * cr - a8i * ci
    gi = gi + a8r * ci + a8i * cr
    cr_ref[...] = gr[n_groups - 1:n_groups, :]
    ci_ref[...] = gi[n_groups - 1:n_groups, :]
    hre_ref[...] = gr[n_groups - 1:n_groups, :]
    him_ref[...] = gi[n_groups - 1:n_groups, :]
    first = group == 0
    pr_ref[...] = jnp.where(first, cr, pltpu.roll(gr, 1, axis=0))
    pi_ref[...] = jnp.where(first, ci, pltpu.roll(gi, 1, axis=0))

    a1r, a1i = apr_ref[0:SUBLANES, :], api_ref[0:SUBLANES, :]
    for g in range(n_groups):
        rows = slice(g * SUBLANES, (g + 1) * SUBLANES)
        br = jnp.broadcast_to(pr_ref[g:g + 1, :], (SUBLANES, SSM_FLAT))
        bi = jnp.broadcast_to(pi_ref[g:g + 1, :], (SUBLANES, SSM_FLAT))
        add_r = a1r * br - a1i * bi
        add_i = a1r * bi + a1i * br
        for l in range(n_lt):
            hr_ref[l, rows, :] = hr_ref[l, rows, :] + add_r[:, lt_cols[l]]
            hi_ref[l, rows, :] = hi_ref[l, rows, :] + add_i[:, lt_cols[l]]
    hr = jnp.concatenate([hr_ref[l] for l in range(n_lt)], axis=1)
    hi = jnp.concatenate([hi_ref[l] for l in range(n_lt)], axis=1)
    y_ref[...] = _s5_readout(hr, hi, u, ccat_ref, d_ref, wglu_ref, bglu_ref).astype(y_ref.dtype)


def _s5_prompt(u, bbar, apr, api, ccat, d_row, wglu_bf, bglu_row):
    b, t, _ = u.shape
    n = S5_CHUNK
    full = lambda a: pl.BlockSpec(a.shape, lambda bi, c: (0,) * a.ndim)
    tok = pl.BlockSpec((None, n, SSM_W), lambda bi, c: (bi, c, 0))
    st = pl.BlockSpec((None, 1, SSM_FLAT), lambda bi, c: (bi, 0, 0))
    a8r, a8i = apr[SUBLANES - 1::SUBLANES], api[SUBLANES - 1::SUBLANES]
    lane_major = pltpu.VMEM((SSM_FLAT // LANES, n, LANES), F32)
    y, hre, him = pl.pallas_call(
        _s5_prompt_kernel,
        grid=(b, t // n),
        in_specs=[tok, full(bbar), full(apr), full(api), full(a8r), full(a8i), full(ccat),
                  full(d_row), full(wglu_bf), full(bglu_row)],
        out_specs=[tok, st, st],
        out_shape=[jax.ShapeDtypeStruct((b, t, SSM_W), BF16),
                   jax.ShapeDtypeStruct((b, 1, SSM_FLAT), F32),
                   jax.ShapeDtypeStruct((b, 1, SSM_FLAT), F32)],
        scratch_shapes=[pltpu.VMEM((1, SSM_FLAT), F32), pltpu.VMEM((1, SSM_FLAT), F32),
                        lane_major, lane_major,
                        pltpu.VMEM((n // SUBLANES, SSM_FLAT), F32),
                        pltpu.VMEM((n // SUBLANES, SSM_FLAT), F32)],
        compiler_params=_cparams("parallel", "arbitrary"),
        name="s5_prompt",
    )(u, bbar, apr, api, a8r, a8i, ccat, d_row, wglu_bf, bglu_row)
    return y, hre.reshape(b, SSM_GROUPS, SSM_STATE), him.reshape(b, SSM_GROUPS, SSM_STATE)


def _s5_sample_kernel(u_ref, h0r_ref, h0i_ref, bbar_ref, apr_ref, api_ref, ccat_ref, d_ref,
                      wglu_ref, bglu_ref, y_ref, hre_ref, him_ref, *, n_steps):
    hr, hi = h0r_ref[...], h0i_ref[...]
    ar, ai = apr_ref[0:1, :], api_ref[0:1, :]
    for t in range(n_steps):
        u = u_ref[:, t * SSM_W:(t + 1) * SSM_W]
        bu = _dot(u.astype(BF16), bbar_ref[...])
        hr, hi = (ar * hr - ai * hi + bu[:, :SSM_FLAT],
                  ar * hi + ai * hr + bu[:, SSM_FLAT:])
        y_ref[:, t * SSM_W:(t + 1) * SSM_W] = _s5_readout(
            hr, hi, u, ccat_ref, d_ref, wglu_ref, bglu_ref).astype(y_ref.dtype)
    hre_ref[...] = hr
    him_ref[...] = hi


def _s5_sample(u, h0_re, h0_im, bbar, apr, api, ccat, d_row, wglu_bf, bglu_row):
    db, t, _ = u.shape
    y, hre, him = pl.pallas_call(
        functools.partial(_s5_sample_kernel, n_steps=t),
        out_shape=[jax.ShapeDtypeStruct((db, t * SSM_W), BF16),
                   jax.ShapeDtypeStruct((db, SSM_FLAT), F32),
                   jax.ShapeDtypeStruct((db, SSM_FLAT), F32)],
        compiler_params=pltpu.CompilerParams(vmem_limit_bytes=VMEM_LIMIT),
        name="s5_sample",
    )(u.reshape(db, t * SSM_W), h0_re.reshape(db, SSM_FLAT), h0_im.reshape(db, SSM_FLAT),
      bbar, apr, api, ccat, d_row, wglu_bf, bglu_row)
    return (y.reshape(db, t, SSM_W), hre.reshape(db, SSM_GROUPS, SSM_STATE),
            him.reshape(db, SSM_GROUPS, SSM_STATE))


MERGE_TM = 512


def _merge_kernel(x_ref, att_ref, ssm_ref, g_ref, wa_ref, wb_ref, wo_ref, lg_ref, lb_ref, o_ref):
    ya = _dot(att_ref[...], wa_ref[...])
    yb = _dot(ssm_ref[...], wb_ref[...])
    merged = g_ref[:, :D_MODEL] * ya + g_ref[:, D_MODEL:] * yb
    y = DN_ALPHA * x_ref[...] + _dot(merged.astype(BF16), wo_ref[...])
    o_ref[...] = _layer_norm(y, lg_ref[...], lb_ref[...])


def _merge(x, att, ssm, gates, wa_bf, wb_bf, wo_bf, lg_row, lb_row):
    n = x.shape[0]
    tm = MERGE_TM
    row = lambda w: pl.BlockSpec((tm, w), lambda i: (i, 0))
    full = lambda a: pl.BlockSpec(a.shape, lambda i: (0,) * a.ndim)
    return pl.pallas_call(
        _merge_kernel,
        grid=(n // tm,),
        in_specs=[row(D_MODEL), row(ATT_W), row(SSM_W), row(2 * D_MODEL),
                  full(wa_bf), full(wb_bf), full(wo_bf), full(lg_row), full(lb_row)],
        out_specs=row(D_MODEL),
        out_shape=jax.ShapeDtypeStruct((n, D_MODEL), F32),
        compiler_params=_cparams("parallel"),
        name="merge",
    )(x, att, ssm, gates, wa_bf, wb_bf, wo_bf, lg_row, lb_row)


PEER_TS = 512
PEER_EC = 1024
PEER_NCH = PEER_EXPERTS // PEER_EC
PEER_ROWS_PER_STEP = PEER_EC // PEER_NKEYS
HALF_KEY = PEER_DKEY // 2


def _sorting_network(n):
    pairs = []
    t = (n - 1).bit_length()
    p = 1 << (t - 1)
    while p > 0:
        q, r, d = 1 << (t - 1), 0, p
        while d > 0:
            pairs.extend((i, i + d) for i in range(n - d) if (i & p) == r)
            d, q, r = q - p, q >> 1, p
        p >>= 1
    return pairs


def _top_sorted(s, k):
    assert s.shape[0] == k * SUBLANES and k & (k - 1) == 0
    v = [s[i * SUBLANES:(i + 1) * SUBLANES, :] for i in range(k)]
    for i, j in _sorting_network(k):
        v[i], v[j] = jnp.maximum(v[i], v[j]), jnp.minimum(v[i], v[j])
    shift = SUBLANES // 2
    while shift:
        other = [pltpu.roll(x, shift, axis=0) for x in v]
        v = [jnp.maximum(v[i], other[k - 1 - i]) for i in range(k)]
        stride = k // 2
        while stride:
            for i in range(k):
                if (i // stride) % 2 == 0:
                    lo, hi = v[i], v[i + stride]
                    v[i], v[i + stride] = jnp.maximum(lo, hi), jnp.minimum(lo, hi)
            stride //= 2
        shift //= 2
    return v


def _peer_tables(s1, s2):
    k = PEER_TOPK
    c = s1.shape[1]
    groups3 = (k, SUBLANES, c)
    v1 = _top_sorted(s1, k)
    v2 = _top_sorted(s2, k)
    s2g = s2.reshape(groups3)
    rank2 = jnp.zeros(groups3, F32)
    for b in range(k):
        rank2 = jnp.where(s2g < v2[b][None], float(b + 1), rank2)
    sub = lax.broadcasted_iota(jnp.int32, (SUBLANES, c), 0)

    def pack(rows):
        out = rows[0]
        for r in range(1, SUBLANES):
            out = jnp.where(sub == r, rows[r], out)
        return out

    v2_lo, v2_hi, v1_hi = pack(v2[:SUBLANES]), pack(v2[SUBLANES:]), pack(v1[SUBLANES:])
    cands = [v1[0] + v2_lo, v1[0] + v2_hi, v1[1] + v2_lo]
    for a in range(2, SUBLANES):
        cands.append(jnp.where(sub < k // (a + 1), v1[a] + v2_lo, NEG_INF))
    cands.append(v1_hi + v2[0])
    cands += [jnp.full((SUBLANES, c), NEG_INF, F32)] * (k - len(cands))
    sc = _top_sorted(jnp.concatenate(cands, axis=0), k)
    tau = sc[k - 1]
    z = jnp.zeros_like(tau)
    for r in range(k):
        z = z + jnp.exp(sc[r] - sc[0])
    s1g = s1.reshape(groups3)
    count = jnp.zeros(groups3, F32)
    for a in range(k):
        n_sel = jnp.zeros_like(tau)
        for b in range(k // (a + 1)):
            n_sel = n_sel + jnp.where(v1[a] + v2[b] >= tau, 1.0, 0.0)
        count = jnp.where(s1g == v1[a][None], n_sel[None], count)
    row_weight = jnp.exp(s1g - v1[0][None]) / z[None]
    p2 = jnp.exp(s2g - v2[0][None])
    flat = lambda a: a.reshape(k * SUBLANES, c)
    return flat(rank2), flat(p2), flat(count), flat(row_weight)


def _peer_kernel(x_ref, wq_ref, k1_ref, k2_ref,
                 u0_ref, un_ref, vp_ref, vl_ref, lg_ref, lb_ref, o_ref,
                 xb_ref, s_ref, r2_ref, p2_ref, cnt_ref, cw_ref, acc_ref, st_ref, ht_ref, w_ref):
    c = pl.program_id(1)
    ts = x_ref.shape[0]
    groups = PEER_NKEYS // BF16_ROWS
    cur = c % 2
    nxt = 1 - cur

    @pl.when(c == 0)
    def _prologue():
        xb = x_ref[...].astype(BF16)
        xb_ref[...] = xb
        st_ref[0] = _dot_nt(u0_ref[...], xb)
        qt = _dot_nt(wq_ref[...], xb).astype(BF16)
        for h in range(PEER_HEADS):
            base = h * PEER_DKEY
            s_ref[0] = _dot(k1_ref[h], qt[base:base + HALF_KEY, :])
            s_ref[1] = _dot(k2_ref[h], qt[base + HALF_KEY:base + PEER_DKEY, :])

            def lane_tile(lt, _):
                cols = pl.ds(pl.multiple_of(lt * LANES, LANES), LANES)
                rank2, p2, count, row_weight = _peer_tables(s_ref[0, :, cols], s_ref[1, :, cols])
                r2_ref[h, :, :, cols] = rank2.astype(BF16).reshape(groups, BF16_ROWS, LANES)
                p2_ref[h, :, :, cols] = p2.astype(BF16).reshape(groups, BF16_ROWS, LANES)
                cnt_ref[h, :, cols] = count
                cw_ref[h, :, cols] = row_weight
                return 0

            lax.fori_loop(0, ts // LANES, lane_tile, 0)
        acc_ref[...] = jnp.zeros_like(acc_ref)
        ht_ref[1] = jnp.zeros((PEER_EC, ts), BF16)

    i1_base = pl.multiple_of(c * PEER_ROWS_PER_STEP, PEER_ROWS_PER_STEP)
    cnts = [cnt_ref[h, pl.ds(i1_base, PEER_ROWS_PER_STEP), :] for h in range(PEER_HEADS)]
    cws = [cw_ref[h, pl.ds(i1_base, PEER_ROWS_PER_STEP), :] for h in range(PEER_HEADS)]
    mblk = 2 * PEER_NKEYS
    for j in range(PEER_ROWS_PER_STEP):
        rows = slice(j * PEER_NKEYS, (j + 1) * PEER_NKEYS)
        w = jnp.zeros((groups, BF16_ROWS, ts), BF16)
        for h in range(PEER_HEADS):
            cnt = jnp.broadcast_to(cnts[h][j:j + 1, :], (BF16_ROWS, ts)).astype(BF16)
            cw = jnp.broadcast_to(cws[h][j:j + 1, :], (BF16_ROWS, ts)).astype(BF16)
            w = w + jnp.where(r2_ref[h] < cnt[None], cw[None], 0.0) * p2_ref[h]
        w_ref[rows, :] = w.reshape(PEER_NKEYS, ts)
        if j % 2 == 1:
            er = slice((j // 2) * mblk, (j // 2 + 1) * mblk)
            st_ref[nxt, er, :] = _dot_nt(un_ref[er, :], xb_ref[...])
    acc_ref[...] += _dot(vp_ref[...], ht_ref[nxt])
    for j in range(PEER_ROWS_PER_STEP):
        rows = slice(j * PEER_NKEYS, (j + 1) * PEER_NKEYS)
        ht_ref[cur, rows, :] = w_ref[rows, :] * _gelu(st_ref[cur, rows, :].astype(BF16))

    @pl.when(c == PEER_NCH - 1)
    def _epilogue():
        acc = acc_ref[...] + _dot(vl_ref[...], ht_ref[cur])
        y = DN_ALPHA * x_ref[...] + acc.T
        o_ref[...] = _layer_norm(y, lg_ref[...], lb_ref[...])


def _peer(x1, wq_bf, k1_bf, k2_bf, u_bf, vt_bf, lg_row, lb_row):
    n = x1.shape[0]
    ts = PEER_TS
    last = PEER_NCH - 1
    full = lambda a: pl.BlockSpec(a.shape, lambda i, c: (0,) * a.ndim)
    tok = pl.BlockSpec((ts, D_MODEL), lambda i, c: (i, 0))
    tables = pltpu.VMEM((PEER_HEADS, PEER_NKEYS, ts), F32)
    packed = pltpu.VMEM((PEER_HEADS, PEER_NKEYS // BF16_ROWS, BF16_ROWS, ts), BF16)
    return pl.pallas_call(
        _peer_kernel,
        grid=(n // ts, PEER_NCH),
        in_specs=[tok, full(wq_bf), full(k1_bf), full(k2_bf),
                  pl.BlockSpec((PEER_EC, D_MODEL), lambda i, c: (0, 0)),
                  pl.BlockSpec((PEER_EC, D_MODEL), lambda i, c: (jnp.minimum(c + 1, last), 0)),
                  pl.BlockSpec((D_MODEL, PEER_EC), lambda i, c: (0, jnp.maximum(c - 1, 0))),
                  pl.BlockSpec((D_MODEL, PEER_EC), lambda i, c: (0, last)),
                  full(lg_row), full(lb_row)],
        out_specs=tok,
        out_shape=jax.ShapeDtypeStruct((n, D_MODEL), F32),
        scratch_shapes=[
            pltpu.VMEM((ts, D_MODEL), BF16),
            pltpu.VMEM((2, PEER_NKEYS, ts), F32),
            packed, packed,
            tables, tables,
            pltpu.VMEM((D_MODEL, ts), F32),
            pltpu.VMEM((2, PEER_EC, ts), F32),
            pltpu.VMEM((2, PEER_EC, ts), BF16),
            pltpu.VMEM((PEER_EC, ts), BF16),
        ],
        compiler_params=_cparams("parallel", "arbitrary"),
        name="peer",
    )(x1, wq_bf, k1_bf, k2_bf, u_bf, u_bf, vt_bf, vt_bf, lg_row, lb_row)


def kernel(x_prompt, x_sample, cache_k, cache_v, state_ssm_re, state_ssm_im, page_table, w_in, b_in, w_a, w_b, w_o, ln1_g, ln1_b, a_re, a_im, log_dt, b_re, b_im, c_re, c_im, d_skip, w_glu, b_glu, ln2_g, ln2_b, w_pq, sub_k1, sub_k2, peer_u, peer_v):
    bn, t, _ = x_prompt.shape
    db, dt_, _ = x_sample.shape
    row = lambda a: a.reshape(1, -1).astype(F32)
    slopes = jnp.exp2(-8.0 * (jnp.arange(N_HEADS, dtype=F32) + 1.0) / N_HEADS)

    w_in_bf = w_in.astype(BF16)
    proj_w = (w_in_bf, row(b_in), w_in_bf[:, ATT_W:3 * ATT_W].T,
              b_in[ATT_W:3 * ATT_W].reshape(2 * ATT_W, 1).astype(F32))
    wa_bf, wb_bf, wo_bf, wglu_bf = (w.astype(BF16) for w in (w_a, w_b, w_o, w_glu))
    eye = jnp.eye(SSM_GROUPS, dtype=F32)
    c_blk = lambda cm: jnp.einsum('gcp,gh->gphc', cm, eye).reshape(SSM_FLAT, SSM_W)
    ccat = jnp.concatenate([c_blk(c_re), -c_blk(c_im)], axis=0).astype(BF16)
    peer_w = (w_pq.T.astype(BF16), sub_k1.astype(BF16), sub_k2.astype(BF16))
    u_bf = peer_u.astype(BF16)
    vt_bf = peer_v.T.astype(BF16)

    bbar, apr, api = _s5_prep(a_re, a_im, log_dt, b_re, b_im)
    s5_w = (bbar, apr, api, ccat, row(d_skip), wglu_bf, row(b_glu))

    xp = x_prompt.reshape(bn * t, D_MODEL)
    q, k_t, v_t, u, gates = _proj(xp, *proj_w, seq_len=t)
    att = _moba_prompt(q.reshape(bn, t, ATT_W), k_t, v_t, slopes)
    ssm, hre_p, him_p = _s5_prompt(u.reshape(bn, t, SSM_W), *s5_w)
    x1 = _merge(xp, att.reshape(bn * t, ATT_W), ssm.reshape(bn * t, SSM_W), gates,
                wa_bf, wb_bf, wo_bf, row(ln1_g), row(ln1_b))
    y_prompt = _peer(x1, *peer_w, u_bf, vt_bf, row(ln2_g), row(ln2_b)).reshape(bn, t, D_MODEL)
    per_token = lambda a: a.reshape(bn, N_HEADS, HEAD_DIM, t).transpose(0, 3, 1, 2)
    k_prompt, v_prompt = per_token(k_t), per_token(v_t)

    xs = x_sample.reshape(db * dt_, D_MODEL)
    q, k, v, u, gates = _proj(xs, *proj_w)
    att = _moba_sample(q.reshape(db, dt_, ATT_W), k.reshape(db, dt_, ATT_W), v.reshape(db, dt_, ATT_W),
                       cache_k, cache_v, page_table, slopes)
    ssm, hre_s, him_s = _s5_sample(u.reshape(db, dt_, SSM_W), state_ssm_re, state_ssm_im, *s5_w)
    x1 = _merge(xs, att.reshape(db * dt_, ATT_W), ssm.reshape(db * dt_, SSM_W), gates,
                wa_bf, wb_bf, wo_bf, row(ln1_g), row(ln1_b))
    y_sample = _peer(x1, *peer_w, u_bf, vt_bf, row(ln2_g), row(ln2_b)).reshape(db, dt_, D_MODEL)
    k_sample = k.reshape(db, dt_, N_HEADS, HEAD_DIM)
    v_sample = v.reshape(db, dt_, N_HEADS, HEAD_DIM)

    return (y_prompt, y_sample, k_prompt, v_prompt, k_sample, v_sample,
            hre_p, him_p, hre_s, him_s)
```

```python
import functools
import math

import jax
import jax.numpy as jnp
from jax import lax
from jax.experimental import pallas as pl
from jax.experimental.pallas import tpu as pltpu

F32 = jnp.float32
BF16 = jnp.bfloat16

D_MODEL = 1024
ATT_W = 512
HEAD_DIM = 64
N_HEADS = 8
MOBA_BLOCK = 256
MOBA_TOPK = 3
SSM_W = 512
SSM_GROUP = 16
SSM_GROUPS = 32
SSM_STATE = 64
SSM_FLAT = SSM_GROUPS * SSM_STATE
PEER_HEADS = 8
PEER_NKEYS = 128
PEER_EXPERTS = PEER_NKEYS * PEER_NKEYS
PEER_DKEY = 256
PEER_TOPK = 16
PROJ_W = 3 * ATT_W + SSM_W + 2 * D_MODEL
DEPTH = 1
DN_ALPHA = (2.0 * DEPTH) ** 0.25
LN_EPS = 1e-5
PAGE_SIZE = 128

LANES = 128
SUBLANES = 8
BF16_ROWS = 16
NEG_INF = float("-inf")
VMEM_LIMIT = 56 * 1024 * 1024

NT_DIMS = (((1,), (1,)), ((), ()))


def _cparams(*sem):
    return pltpu.CompilerParams(dimension_semantics=sem, vmem_limit_bytes=VMEM_LIMIT)


def _dot(a, b):
    return jnp.dot(a, b, preferred_element_type=F32)


def _dot_nt(a, b):
    return lax.dot_general(a, b, NT_DIMS, preferred_element_type=F32)


def _split(x):
    hi = x.astype(BF16)
    lo = (x - hi.astype(F32)).astype(BF16)
    return hi, lo


def _gelu(x):
    c = math.sqrt(2.0 / math.pi)
    return 0.5 * x * (1.0 + jnp.tanh(c * (x + 0.044715 * (x * x * x))))


def _layer_norm(y, g, b):
    mu = jnp.mean(y, axis=-1, keepdims=True)
    yc = y - mu
    var = jnp.mean(yc * yc, axis=-1, keepdims=True)
    return yc * lax.rsqrt(var + LN_EPS) * g + b


PROJ_TM = 512


def _proj_kernel(x_ref, w_ref, b_ref, wkv_t_ref, bkv_col_ref, q_ref, k_ref, v_ref, u_ref, g_ref,
                 *, kv_transposed):
    xb = x_ref[...].astype(BF16)

    def seg(lo, hi):
        return _dot(xb, w_ref[:, lo:hi]) + b_ref[:, lo:hi]

    q_ref[...] = seg(0, ATT_W)
    if kv_transposed:
        kv_t = _dot_nt(wkv_t_ref[...], xb) + bkv_col_ref[...]
        k_ref[...] = kv_t[:ATT_W, :]
        v_ref[...] = kv_t[ATT_W:, :]
    else:
        k_ref[...] = seg(ATT_W, 2 * ATT_W)
        v_ref[...] = seg(2 * ATT_W, 3 * ATT_W)
    u_ref[...] = seg(3 * ATT_W, 3 * ATT_W + SSM_W)
    g_ref[...] = jax.nn.sigmoid(seg(3 * ATT_W + SSM_W, PROJ_W))


def _proj(x, w_bf, b_row, wkv_t, bkv_col, seq_len=None):
    n = x.shape[0]
    tm = PROJ_TM
    row = lambda w: pl.BlockSpec((tm, w), lambda i: (i, 0))
    full = lambda a: pl.BlockSpec(a.shape, lambda i: (0,) * a.ndim)
    if seq_len is None:
        kv_spec, kv_shape = row(ATT_W), jax.ShapeDtypeStruct((n, ATT_W), F32)
    else:
        per_seq = seq_len // tm
        kv_spec = pl.BlockSpec((None, ATT_W, tm), lambda i: (i // per_seq, 0, i % per_seq))
        kv_shape = jax.ShapeDtypeStruct((n // seq_len, ATT_W, seq_len), F32)
    rows = lambda w: jax.ShapeDtypeStruct((n, w), F32)
    return pl.pallas_call(
        functools.partial(_proj_kernel, kv_transposed=seq_len is not None),
        grid=(n // tm,),
        in_specs=[row(D_MODEL), full(w_bf), full(b_row), full(wkv_t), full(bkv_col)],
        out_specs=[row(ATT_W), kv_spec, kv_spec, row(SSM_W), row(2 * D_MODEL)],
        out_shape=[rows(ATT_W), kv_shape, kv_shape, rows(SSM_W), rows(2 * D_MODEL)],
        compiler_params=_cparams("parallel"),
        name="proj",
    )(x, w_bf, b_row, wkv_t, bkv_col)


def _moba_select_bias(gates, cur):
    nb = len(gates)
    g = [jnp.where(n < cur, gates[n], NEG_INF) for n in range(nb)]
    bias = []
    for n in range(nb):
        rank = jnp.zeros(gates[n].shape, jnp.int32)
        for m in range(nb):
            if m != n:
                ahead = (g[m] >= g[n]) if m < n else (g[m] > g[n])
                rank = rank + ahead.astype(jnp.int32)
        sel = (n < cur) & (rank < MOBA_TOPK)
        bias.append(jnp.where(sel, 0.0, NEG_INF))
    return bias


def _moba_prompt_kernel(slopes_ref, q_ref, kt_ref, vt_in_ref, o_ref,
                        kb_ref, vt_ref, ot_ref):
    t = q_ref.shape[0]
    nb = t // MOBA_BLOCK
    blk = MOBA_BLOCK
    hp = pl.program_id(1)
    lane = lax.broadcasted_iota(jnp.int32, (1, LANES), 1)

    k = kt_ref[...].T
    kb_ref[...] = k.astype(BF16)
    vt_ref[...] = vt_in_ref[...].astype(BF16)
    kmean = jnp.mean(k.reshape(nb, blk, LANES), axis=1)
    q_all = q_ref[...]
    q_hi, q_lo = _split(q_all)
    cur = lax.broadcasted_iota(jnp.int32, (1, t), 1) // blk
    d0 = (lax.broadcasted_iota(jnp.int32, (blk, blk), 1)
          - lax.broadcasted_iota(jnp.int32, (blk, blk), 0)).astype(F32)

    heads = []
    for hh in range(2):
        hmask = (lane >= HEAD_DIM * hh) & (lane < HEAD_DIM * (hh + 1))
        slope = slopes_ref[2 * hp + hh]
        km_hi, km_lo = _split(jnp.where(hmask, kmean, 0.0))
        gate = _dot_nt(km_hi, q_hi) + _dot_nt(km_hi, q_lo) + _dot_nt(km_lo, q_hi)
        selb = _moba_select_bias([gate[n:n + 1, :] for n in range(nb)], cur)
        heads.append((hmask, slope, selb, slope * d0, slice(HEAD_DIM * hh, HEAD_DIM * (hh + 1))))

    for i in range(nb):
        qcols = slice(i * blk, (i + 1) * blk)
        n_keys = (i + 1) * blk
        q_blk = q_ref[qcols, :] * (HEAD_DIM ** -0.5)
        s_alls = [_dot_nt(kb_ref[0:n_keys, :], jnp.where(hmask, q_blk, 0.0).astype(BF16))
                  for hmask, *_ in heads]
        tiles = []
        for (hmask, slope, selb, slope_d0, rows), s_all in zip(heads, s_alls):
            per_head = []
            for j in range(i + 1):
                s = s_all[j * blk:(j + 1) * blk, :] - (slope_d0 + slope * float((i - j) * blk))
                if j < i:
                    s = s + selb[j][:, qcols]
                else:
                    s = jnp.where(d0 >= 0.0, s, NEG_INF)
                per_head.append(s)
            tiles.append(per_head)
        maxes = []
        for per_head in tiles:
            m = jnp.max(per_head[0], axis=0, keepdims=True)
            for s in per_head[1:]:
                m = jnp.maximum(m, jnp.max(s, axis=0, keepdims=True))
            maxes.append(m)
        probs = [[jnp.exp(s - m) for s in per_head] for per_head, m in zip(tiles, maxes)]
        sums = []
        for per_head in probs:
            l = jnp.sum(per_head[0], axis=0, keepdims=True)
            for p in per_head[1:]:
                l = l + jnp.sum(p, axis=0, keepdims=True)
            sums.append(l)
        for (hmask, slope, selb, slope_d0, rows), per_head, l in zip(heads, probs, sums):
            p_all = jnp.concatenate([p.astype(BF16) for p in per_head], axis=0)
            res = _dot(vt_ref[:, 0:n_keys], p_all) / l
            ot_ref[rows, qcols] = res[rows, :]

    o_ref[...] = ot_ref[...].T.astype(o_ref.dtype)


def _moba_prompt(q, k_t, v_t, slopes):
    b, t, _ = q.shape
    spec = pl.BlockSpec((None, t, LANES), lambda bi, hp, *_: (bi, 0, hp))
    spec_t = pl.BlockSpec((None, LANES, t), lambda bi, hp, *_: (bi, hp, 0))
    grid_spec = pltpu.PrefetchScalarGridSpec(
        num_scalar_prefetch=1,
        grid=(b, ATT_W // LANES),
        in_specs=[spec, spec_t, spec_t],
        out_specs=spec,
        scratch_shapes=[
            pltpu.VMEM((t, LANES), BF16),
            pltpu.VMEM((LANES, t), BF16),
            pltpu.VMEM((LANES, t), F32),
        ],
    )
    return pl.pallas_call(
        _moba_prompt_kernel,
        grid_spec=grid_spec,
        out_shape=jax.ShapeDtypeStruct((b, t, ATT_W), BF16),
        compiler_params=_cparams("parallel", "parallel"),
        name="moba_prompt",
    )(slopes, q, k_t, v_t)


def _moba_sample_kernel(pt_ref, slopes_ref, q_ref, kn_ref, vn_ref, *rest, n_pages):
    kt_refs = rest[:n_pages]
    vt_refs = rest[n_pages:2 * n_pages]
    o_ref = rest[2 * n_pages]
    nq = q_ref.shape[0]
    n_past = n_pages * PAGE_SIZE
    nb = n_past // MOBA_BLOCK
    pages_per_block = MOBA_BLOCK // PAGE_SIZE
    rows = N_HEADS * nq
    scale = HEAD_DIM ** -0.5
    assert rows % SUBLANES == 0 and nq <= PAGE_SIZE

    q = q_ref[...]
    r_id = lax.broadcasted_iota(jnp.int32, (rows, ATT_W), 0)
    c_id = lax.broadcasted_iota(jnp.int32, (rows, ATT_W), 1)
    q_rep = jnp.concatenate([q] * (rows // nq), axis=0)
    own = c_id // HEAD_DIM == r_id // nq
    qs = jnp.where(own, q_rep * scale, 0.0).astype(BF16)

    r_col = lax.broadcasted_iota(jnp.int32, (rows, 1), 0)
    tq = (r_col % nq).astype(F32)
    slope = jnp.zeros((rows, 1), F32)
    for h in range(N_HEADS):
        slope = jnp.where(r_col // nq == h, slopes_ref[h], slope)
    lane = lax.broadcasted_iota(jnp.int32, (1, PAGE_SIZE), 1).astype(F32)

    raw = [_dot(qs, kt_refs[pg][...].astype(BF16)) for pg in range(n_pages)]

    gates = []
    for n in range(nb):
        total = raw[n * pages_per_block]
        for pg in range(n * pages_per_block + 1, (n + 1) * pages_per_block):
            total = total + raw[pg]
        gates.append(jnp.sum(total, axis=1, keepdims=True) * (1.0 / (MOBA_BLOCK * scale)))
    bias = _moba_select_bias(gates, nb)

    tiles = []
    for pg in range(n_pages):
        dist = (tq + float(n_past - pg * PAGE_SIZE)) - lane
        tiles.append(raw[pg] - slope * dist + bias[pg // pages_per_block])
    pad_rows = lambda a: jnp.concatenate([a, jnp.zeros((PAGE_SIZE - nq, ATT_W), F32)], axis=0)
    dist_new = tq - lane
    s_new = _dot_nt(qs, pad_rows(kn_ref[...]).astype(BF16))
    tiles.append(jnp.where(dist_new >= 0.0, s_new - slope * dist_new, NEG_INF))

    top = tiles[0]
    for s in tiles[1:]:
        top = jnp.maximum(top, s)
    m = jnp.max(top, axis=1, keepdims=True)
    probs = [jnp.exp(s - m) for s in tiles]
    total = probs[0]
    for p in probs[1:]:
        total = total + p
    inv_l = 1.0 / jnp.sum(total, axis=1, keepdims=True)
    out = _dot((probs[n_pages] * inv_l).astype(BF16), pad_rows(vn_ref[...]).astype(BF16))
    for pg in range(n_pages):
        out = out + _dot_nt((probs[pg] * inv_l).astype(BF16), vt_refs[pg][...].astype(BF16))

    res = jnp.zeros((nq, ATT_W), F32)
    c_q = lax.broadcasted_iota(jnp.int32, (nq, ATT_W), 1)
    for h in range(N_HEADS):
        res = jnp.where(c_q // HEAD_DIM == h, out[h * nq:(h + 1) * nq, :], res)
    o_ref[...] = res.astype(o_ref.dtype)


def _moba_sample(q, k_new, v_new, cache_k, cache_v, page_table, slopes):
    db, nq, _ = q.shape
    n_pages = page_table.shape[1]
    new_spec = pl.BlockSpec((None, nq, ATT_W), lambda s, *_: (s, 0, 0))

    def page_spec(pg):
        return pl.BlockSpec((None, ATT_W, PAGE_SIZE), lambda s, pt, sl: (pt[s, pg], 0, 0))

    grid_spec = pltpu.PrefetchScalarGridSpec(
        num_scalar_prefetch=2,
        grid=(db,),
        in_specs=([new_spec] * 3 + [page_spec(pg) for pg in range(n_pages)] * 2),
        out_specs=new_spec,
    )
    by_token = lambda c: jnp.transpose(c, (0, 2, 3, 1)).reshape(c.shape[0], ATT_W, PAGE_SIZE)
    ck, cv = by_token(cache_k), by_token(cache_v)
    return pl.pallas_call(
        functools.partial(_moba_sample_kernel, n_pages=n_pages),
        grid_spec=grid_spec,
        out_shape=jax.ShapeDtypeStruct((db, nq, ATT_W), BF16),
        compiler_params=_cparams("parallel"),
        name="moba_sample",
    )(page_table, slopes, q, k_new, v_new, *([ck] * n_pages), *([cv] * n_pages))


S5_CHUNK = 256


def _s5_prep_kernel(are_ref, aim_ref, ldt_ref, bre_ref, bim_ref, cre_ref, cim_ref,
                    bbar_ref, ccat_ref, apr_ref, api_ref):
    ar, ai = are_ref[...], aim_ref[...]
    dt = jnp.exp(ldt_ref[...])
    mag = jnp.exp(dt * ar)
    ang = dt * ai
    abr, abi = mag * jnp.cos(ang), mag * jnp.sin(ang)
    den = ar * ar + ai * ai
    nr, ni = abr - 1.0, abi
    f_re = (nr * ar + ni * ai) / den
    f_im = (ni * ar - nr * ai) / den

    row_g = lax.broadcasted_iota(jnp.int32, (SSM_W, SSM_FLAT), 0) // SSM_GROUP
    col_g = lax.broadcasted_iota(jnp.int32, (SSM_W, SSM_FLAT), 1) // SSM_STATE
    same_group = row_g == col_g

    def block_diag(ref):
        x = ref[...]
        pair = jnp.concatenate([x] * (LANES // SSM_STATE), axis=1)
        return jnp.where(same_group, jnp.concatenate([pair] * (SSM_FLAT // LANES), axis=1), 0.0)

    bre, bim = block_diag(bre_ref), block_diag(bim_ref)
    bbar_ref[:, :SSM_FLAT] = (f_re * bre - f_im * bim).astype(BF16)
    bbar_ref[:, SSM_FLAT:] = (f_re * bim + f_im * bre).astype(BF16)
    ccat_ref[:, :SSM_FLAT] = block_diag(cre_ref).astype(BF16)
    ccat_ref[:, SSM_FLAT:] = (-block_diag(cim_ref)).astype(BF16)
    apr_ref[0:1, :] = abr
    api_ref[0:1, :] = abi
    d = 1
    while d < S5_CHUNK:
        pr, pi = apr_ref[0:d, :], api_ref[0:d, :]
        er, ei = apr_ref[d - 1:d, :], api_ref[d - 1:d, :]
        apr_ref[d:2 * d, :] = pr * er - pi * ei
        api_ref[d:2 * d, :] = pr * ei + pi * er
        d *= 2


def _s5_prep(a_re, a_im, log_dt, b_re, b_im, c_re, c_im):
    flat = lambda a: a.reshape(1, SSM_FLAT)
    ldt = jnp.broadcast_to(log_dt[:, None], (SSM_GROUPS, SSM_STATE))
    by_channel = lambda b: b.transpose(0, 2, 1).reshape(SSM_W, SSM_STATE)
    c_rows = lambda cm: cm.reshape(SSM_W, SSM_STATE)
    return pl.pallas_call(
        _s5_prep_kernel,
        out_shape=[jax.ShapeDtypeStruct((SSM_W, 2 * SSM_FLAT), BF16),
                   jax.ShapeDtypeStruct((SSM_W, 2 * SSM_FLAT), BF16),
                   jax.ShapeDtypeStruct((S5_CHUNK, SSM_FLAT), F32),
                   jax.ShapeDtypeStruct((S5_CHUNK, SSM_FLAT), F32)],
        compiler_params=pltpu.CompilerParams(vmem_limit_bytes=VMEM_LIMIT),
        name="s5_prep",
    )(flat(a_re), flat(a_im), flat(ldt), by_channel(b_re), by_channel(b_im), c_rows(c_re), c_rows(c_im))


def _s5_readout(h_re, h_im, u, ccat_ref, d_ref, wglu_ref, bglu_ref):
    hcat = jnp.concatenate([h_re.astype(BF16), h_im.astype(BF16)], axis=1)
    y = _dot_nt(hcat, ccat_ref[...]) + d_ref[...] * u
    z = _gelu(y)
    return z * jax.nn.sigmoid(_dot(z.astype(BF16), wglu_ref[...]) + bglu_ref[...])


def _s5_prompt_kernel(u_ref, bbar_ref, apr_ref, api_ref, a8r_ref, a8i_ref, ccat_ref, d_ref,
                      wglu_ref, bglu_ref,
                      y_ref, hre_ref, him_ref, cr_ref, ci_ref, hr_ref, hi_ref, pr_ref, pi_ref):
    c = pl.program_id(1)
    n = u_ref.shape[0]
    n_groups = n // SUBLANES

    @pl.when(c == 0)
    def _():
        cr_ref[...] = jnp.zeros_like(cr_ref)
        ci_ref[...] = jnp.zeros_like(ci_ref)

    def scan_level(hr, hi, pos, d, power):
        er, ei = apr_ref[power - 1:power, :], api_ref[power - 1:power, :]
        keep = pos >= d
        sr = jnp.where(keep, pltpu.roll(hr, d, axis=0), 0.0)
        si = jnp.where(keep, pltpu.roll(hi, d, axis=0), 0.0)
        return hr + er * sr - ei * si, hi + er * si + ei * sr

    u = u_ref[...]
    bu = _dot(u.astype(BF16), bbar_ref[...])
    hr, hi = bu[:, :SSM_FLAT], bu[:, SSM_FLAT:]
    hr = hr.reshape(n_groups, SUBLANES, SSM_FLAT)
    hi = hi.reshape(n_groups, SUBLANES, SSM_FLAT)
    in_group = lax.broadcasted_iota(jnp.int32, (SUBLANES, SSM_FLAT), 0)
    d = 1
    while d < SUBLANES:
        er = jnp.where(in_group >= d, apr_ref[d - 1:d, :], 0.0)[None]
        ei = jnp.where(in_group >= d, api_ref[d - 1:d, :], 0.0)[None]
        sr, si = pltpu.roll(hr, d, axis=1), pltpu.roll(hi, d, axis=1)
        hr, hi = hr + er * sr - ei * si, hi + er * si + ei * sr
        d *= 2
    hr = hr.reshape(n, SSM_FLAT)
    hi = hi.reshape(n, SSM_FLAT)
    n_lt = SSM_FLAT // LANES
    lt_cols = [slice(l * LANES, (l + 1) * LANES) for l in range(n_lt)]
    for l in range(n_lt):
        hr_ref[l] = hr[:, lt_cols[l]]
        hi_ref[l] = hi[:, lt_cols[l]]

    ends = pl.ds(SUBLANES - 1, n_groups, stride=SUBLANES)
    gr = jnp.concatenate([hr_ref[l, ends, :] for l in range(n_lt)], axis=1)
    gi = jnp.concatenate([hi_ref[l, ends, :] for l in range(n_lt)], axis=1)
    group = lax.broadcasted_iota(jnp.int32, (n_groups, SSM_FLAT), 0)
    d = 1
    while d < n_groups:
        gr, gi = scan_level(gr, gi, group, d, d * SUBLANES)
        d *= 2
    cr, ci = cr_ref[...], ci_ref[...]
    a8r, a8i = a8r_ref[...], a8i_ref[...]
    gr = gr + a8r * cr - a8i * ci
    gi = gi + a8r * ci + a8i * cr
    cr_ref[...] = gr[n_groups - 1:n_groups, :]
    ci_ref[...] = gi[n_groups - 1:n_groups, :]
    hre_ref[...] = gr[n_groups - 1:n_groups, :]
    him_ref[...] = gi[n_groups - 1:n_groups, :]
    first = group == 0
    pr_ref[...] = jnp.where(first, cr, pltpu.roll(gr, 1, axis=0))
    pi_ref[...] = jnp.where(first, ci, pltpu.roll(gi, 1, axis=0))

    a1r, a1i = apr_ref[0:SUBLANES, :], api_ref[0:SUBLANES, :]
    for g in range(n_groups):
        rows = slice(g * SUBLANES, (g + 1) * SUBLANES)
        br = jnp.broadcast_to(pr_ref[g:g + 1, :], (SUBLANES, SSM_FLAT))
        bi = jnp.broadcast_to(pi_ref[g:g + 1, :], (SUBLANES, SSM_FLAT))
        add_r = a1r * br - a1i * bi
        add_i = a1r * bi + a1i * br
        for l in range(n_lt):
            hr_ref[l, rows, :] = hr_ref[l, rows, :] + add_r[:, lt_cols[l]]
            hi_ref[l, rows, :] = hi_ref[l, rows, :] + add_i[:, lt_cols[l]]
    hr = jnp.concatenate([hr_ref[l] for l in range(n_lt)], axis=1)
    hi = jnp.concatenate([hi_ref[l] for l in range(n_lt)], axis=1)
    y_ref[...] = _s5_readout(hr, hi, u, ccat_ref, d_ref, wglu_ref, bglu_ref).astype(y_ref.dtype)


def _s5_prompt(u, bbar, apr, api, ccat, d_row, wglu_bf, bglu_row):
    b, t, _ = u.shape
    n = S5_CHUNK
    full = lambda a: pl.BlockSpec(a.shape, lambda bi, c: (0,) * a.ndim)
    tok = pl.BlockSpec((None, n, SSM_W), lambda bi, c: (bi, c, 0))
    st = pl.BlockSpec((None, 1, SSM_FLAT), lambda bi, c: (bi, 0, 0))
    a8r, a8i = apr[SUBLANES - 1::SUBLANES], api[SUBLANES - 1::SUBLANES]
    lane_major = pltpu.VMEM((SSM_FLAT // LANES, n, LANES), F32)
    y, hre, him = pl.pallas_call(
        _s5_prompt_kernel,
        grid=(b, t // n),
        in_specs=[tok, full(bbar), full(apr), full(api), full(a8r), full(a8i), full(ccat),
                  full(d_row), full(wglu_bf), full(bglu_row)],
        out_specs=[tok, st, st],
        out_shape=[jax.ShapeDtypeStruct((b, t, SSM_W), BF16),
                   jax.ShapeDtypeStruct((b, 1, SSM_FLAT), F32),
                   jax.ShapeDtypeStruct((b, 1, SSM_FLAT), F32)],
        scratch_shapes=[pltpu.VMEM((1, SSM_FLAT), F32), pltpu.VMEM((1, SSM_FLAT), F32),
                        lane_major, lane_major,
                        pltpu.VMEM((n // SUBLANES, SSM_FLAT), F32),
                        pltpu.VMEM((n // SUBLANES, SSM_FLAT), F32)],
        compiler_params=_cparams("parallel", "arbitrary"),
        name="s5_prompt",
    )(u, bbar, apr, api, a8r, a8i, ccat, d_row, wglu_bf, bglu_row)
    return y, hre.reshape(b, SSM_GROUPS, SSM_STATE), him.reshape(b, SSM_GROUPS, SSM_STATE)


def _s5_sample_kernel(u_ref, h0r_ref, h0i_ref, bbar_ref, apr_ref, api_ref, ccat_ref, d_ref,
                      wglu_ref, bglu_ref, y_ref, hre_ref, him_ref, *, n_steps):
    hr, hi = h0r_ref[...], h0i_ref[...]
    ar, ai = apr_ref[0:1, :], api_ref[0:1, :]
    for t in range(n_steps):
        u = u_ref[:, t * SSM_W:(t + 1) * SSM_W]
        bu = _dot(u.astype(BF16), bbar_ref[...])
        hr, hi = (ar * hr - ai * hi + bu[:, :SSM_FLAT],
                  ar * hi + ai * hr + bu[:, SSM_FLAT:])
        y_ref[:, t * SSM_W:(t + 1) * SSM_W] = _s5_readout(
            hr, hi, u, ccat_ref, d_ref, wglu_ref, bglu_ref).astype(y_ref.dtype)
    hre_ref[...] = hr
    him_ref[...] = hi


def _s5_sample(u, h0_re, h0_im, bbar, apr, api, ccat, d_row, wglu_bf, bglu_row):
    db, t, _ = u.shape
    y, hre, him = pl.pallas_call(
        functools.partial(_s5_sample_kernel, n_steps=t),
        out_shape=[jax.ShapeDtypeStruct((db, t * SSM_W), BF16),
                   jax.ShapeDtypeStruct((db, SSM_FLAT), F32),
                   jax.ShapeDtypeStruct((db, SSM_FLAT), F32)],
        compiler_params=pltpu.CompilerParams(vmem_limit_bytes=VMEM_LIMIT),
        name="s5_sample",
    )(u.reshape(db, t * SSM_W), h0_re.reshape(db, SSM_FLAT), h0_im.reshape(db, SSM_FLAT),
      bbar, apr, api, ccat, d_row, wglu_bf, bglu_row)
    return (y.reshape(db, t, SSM_W), hre.reshape(db, SSM_GROUPS, SSM_STATE),
            him.reshape(db, SSM_GROUPS, SSM_STATE))


MERGE_TM = 512


def _merge_kernel(x_ref, att_ref, ssm_ref, g_ref, wa_ref, wb_ref, wo_ref, lg_ref, lb_ref, o_ref):
    ya = _dot(att_ref[...], wa_ref[...])
    yb = _dot(ssm_ref[...], wb_ref[...])
    merged = g_ref[:, :D_MODEL] * ya + g_ref[:, D_MODEL:] * yb
    y = DN_ALPHA * x_ref[...] + _dot(merged.astype(BF16), wo_ref[...])
    o_ref[...] = _layer_norm(y, lg_ref[...], lb_ref[...])


def _merge(x, att, ssm, gates, wa_bf, wb_bf, wo_bf, lg_row, lb_row):
    n = x.shape[0]
    tm = MERGE_TM
    row = lambda w: pl.BlockSpec((tm, w), lambda i: (i, 0))
    full = lambda a: pl.BlockSpec(a.shape, lambda i: (0,) * a.ndim)
    return pl.pallas_call(
        _merge_kernel,
        grid=(n // tm,),
        in_specs=[row(D_MODEL), row(ATT_W), row(SSM_W), row(2 * D_MODEL),
                  full(wa_bf), full(wb_bf), full(wo_bf), full(lg_row), full(lb_row)],
        out_specs=row(D_MODEL),
        out_shape=jax.ShapeDtypeStruct((n, D_MODEL), F32),
        compiler_params=_cparams("parallel"),
        name="merge",
    )(x, att, ssm, gates, wa_bf, wb_bf, wo_bf, lg_row, lb_row)


PEER_TS = 512
PEER_EC = 1024
PEER_NCH = PEER_EXPERTS // PEER_EC
PEER_ROWS_PER_STEP = PEER_EC // PEER_NKEYS
HALF_KEY = PEER_DKEY // 2


def _sorting_network(n):
    pairs = []
    t = (n - 1).bit_length()
    p = 1 << (t - 1)
    while p > 0:
        q, r, d = 1 << (t - 1), 0, p
        while d > 0:
            pairs.extend((i, i + d) for i in range(n - d) if (i & p) == r)
            d, q, r = q - p, q >> 1, p
        p >>= 1
    return pairs


def _top_sorted(s, k):
    assert s.shape[0] == k * SUBLANES and k & (k - 1) == 0
    v = [s[i * SUBLANES:(i + 1) * SUBLANES, :] for i in range(k)]
    for i, j in _sorting_network(k):
        v[i], v[j] = jnp.maximum(v[i], v[j]), jnp.minimum(v[i], v[j])
    shift = SUBLANES // 2
    while shift:
        other = [pltpu.roll(x, shift, axis=0) for x in v]
        v = [jnp.maximum(v[i], other[k - 1 - i]) for i in range(k)]
        stride = k // 2
        while stride:
            for i in range(k):
                if (i // stride) % 2 == 0:
                    lo, hi = v[i], v[i + stride]
                    v[i], v[i + stride] = jnp.maximum(lo, hi), jnp.minimum(lo, hi)
            stride //= 2
        shift //= 2
    return v


def _peer_tables(s1, s2):
    k = PEER_TOPK
    c = s1.shape[1]
    groups3 = (k, SUBLANES, c)
    v1 = _top_sorted(s1, k)
    v2 = _top_sorted(s2, k)
    s2g = s2.reshape(groups3)
    rank2 = jnp.zeros(groups3, F32)
    for b in range(k):
        rank2 = jnp.where(s2g < v2[b][None], float(b + 1), rank2)
    sub = lax.broadcasted_iota(jnp.int32, (SUBLANES, c), 0)

    def pack(rows):
        out = rows[0]
        for r in range(1, SUBLANES):
            out = jnp.where(sub == r, rows[r], out)
        return out

    v2_lo, v2_hi, v1_hi = pack(v2[:SUBLANES]), pack(v2[SUBLANES:]), pack(v1[SUBLANES:])
    cands = [v1[0] + v2_lo, v1[0] + v2_hi, v1[1] + v2_lo]
    for a in range(2, SUBLANES):
        cands.append(jnp.where(sub < k // (a + 1), v1[a] + v2_lo, NEG_INF))
    cands.append(v1_hi + v2[0])
    cands += [jnp.full((SUBLANES, c), NEG_INF, F32)] * (k - len(cands))
    sc = _top_sorted(jnp.concatenate(cands, axis=0), k)
    tau = sc[k - 1]
    z = jnp.zeros_like(tau)
    for r in range(k):
        z = z + jnp.exp(sc[r] - sc[0])
    s1g = s1.reshape(groups3)
    count = jnp.zeros(groups3, F32)
    for a in range(k):
        n_sel = jnp.zeros_like(tau)
        for b in range(k // (a + 1)):
            n_sel = n_sel + jnp.where(v1[a] + v2[b] >= tau, 1.0, 0.0)
        count = jnp.where(s1g == v1[a][None], n_sel[None], count)
    row_weight = jnp.exp(s1g - v1[0][None]) / z[None]
    p2 = jnp.exp(s2g - v2[0][None])
    flat = lambda a: a.reshape(k * SUBLANES, c)
    return flat(rank2), flat(p2), flat(count), flat(row_weight)


def _peer_kernel(x_ref, wq_ref, k1_ref, k2_ref,
                 u0_ref, un_ref, vp_ref, vl_ref, lg_ref, lb_ref, o_ref,
                 xb_ref, s_ref, r2_ref, p2_ref, cnt_ref, cw_ref, acc_ref, st_ref, ht_ref, w_ref):
    c = pl.program_id(1)
    ts = x_ref.shape[0]
    groups = PEER_NKEYS // BF16_ROWS
    cur = c % 2
    nxt = 1 - cur

    @pl.when(c == 0)
    def _prologue():
        xb = x_ref[...].astype(BF16)
        xb_ref[...] = xb
        st_ref[0] = _dot_nt(u0_ref[...], xb)
        qt = _dot_nt(wq_ref[...], xb).astype(BF16)
        for h in range(PEER_HEADS):
            base = h * PEER_DKEY
            s_ref[0] = _dot(k1_ref[h], qt[base:base + HALF_KEY, :])
            s_ref[1] = _dot(k2_ref[h], qt[base + HALF_KEY:base + PEER_DKEY, :])

            def lane_tile(lt, _):
                cols = pl.ds(pl.multiple_of(lt * LANES, LANES), LANES)
                rank2, p2, count, row_weight = _peer_tables(s_ref[0, :, cols], s_ref[1, :, cols])
                r2_ref[h, :, :, cols] = rank2.astype(BF16).reshape(groups, BF16_ROWS, LANES)
                p2_ref[h, :, :, cols] = p2.astype(BF16).reshape(groups, BF16_ROWS, LANES)
                cnt_ref[h, :, cols] = count
                cw_ref[h, :, cols] = row_weight
                return 0

            lax.fori_loop(0, ts // LANES, lane_tile, 0)
        acc_ref[...] = jnp.zeros_like(acc_ref)
        ht_ref[1] = jnp.zeros((PEER_EC, ts), BF16)

    i1_base = pl.multiple_of(c * PEER_ROWS_PER_STEP, PEER_ROWS_PER_STEP)
    cnts = [cnt_ref[h, pl.ds(i1_base, PEER_ROWS_PER_STEP), :] for h in range(PEER_HEADS)]
    cws = [cw_ref[h, pl.ds(i1_base, PEER_ROWS_PER_STEP), :] for h in range(PEER_HEADS)]
    mblk = 2 * PEER_NKEYS
    for j in range(PEER_ROWS_PER_STEP):
        rows = slice(j * PEER_NKEYS, (j + 1) * PEER_NKEYS)
        w = jnp.zeros((groups, BF16_ROWS, ts), BF16)
        for h in range(PEER_HEADS):
            cnt = jnp.broadcast_to(cnts[h][j:j + 1, :], (BF16_ROWS, ts)).astype(BF16)
            cw = jnp.broadcast_to(cws[h][j:j + 1, :], (BF16_ROWS, ts)).astype(BF16)
            w = w + jnp.where(r2_ref[h] < cnt[None], cw[None], 0.0) * p2_ref[h]
        w_ref[rows, :] = w.reshape(PEER_NKEYS, ts)
        if j % 2 == 1:
            er = slice((j // 2) * mblk, (j // 2 + 1) * mblk)
            st_ref[nxt, er, :] = _dot_nt(un_ref[er, :], xb_ref[...])
    acc_ref[...] += _dot(vp_ref[...], ht_ref[nxt])
    for j in range(PEER_ROWS_PER_STEP):
        rows = slice(j * PEER_NKEYS, (j + 1) * PEER_NKEYS)
        ht_ref[cur, rows, :] = w_ref[rows, :] * _gelu(st_ref[cur, rows, :].astype(BF16))

    @pl.when(c == PEER_NCH - 1)
    def _epilogue():
        acc = acc_ref[...] + _dot(vl_ref[...], ht_ref[cur])
        y = DN_ALPHA * x_ref[...] + acc.T
        o_ref[...] = _layer_norm(y, lg_ref[...], lb_ref[...])


def _peer(x1, wq_bf, k1_bf, k2_bf, u_bf, vt_bf, lg_row, lb_row):
    n = x1.shape[0]
    ts = PEER_TS
    last = PEER_NCH - 1
    full = lambda a: pl.BlockSpec(a.shape, lambda i, c: (0,) * a.ndim)
    tok = pl.BlockSpec((ts, D_MODEL), lambda i, c: (i, 0))
    tables = pltpu.VMEM((PEER_HEADS, PEER_NKEYS, ts), F32)
    packed = pltpu.VMEM((PEER_HEADS, PEER_NKEYS // BF16_ROWS, BF16_ROWS, ts), BF16)
    return pl.pallas_call(
        _peer_kernel,
        grid=(n // ts, PEER_NCH),
        in_specs=[tok, full(wq_bf), full(k1_bf), full(k2_bf),
                  pl.BlockSpec((PEER_EC, D_MODEL), lambda i, c: (0, 0)),
                  pl.BlockSpec((PEER_EC, D_MODEL), lambda i, c: (jnp.minimum(c + 1, last), 0)),
                  pl.BlockSpec((D_MODEL, PEER_EC), lambda i, c: (0, jnp.maximum(c - 1, 0))),
                  pl.BlockSpec((D_MODEL, PEER_EC), lambda i, c: (0, last)),
                  full(lg_row), full(lb_row)],
        out_specs=tok,
        out_shape=jax.ShapeDtypeStruct((n, D_MODEL), F32),
        scratch_shapes=[
            pltpu.VMEM((ts, D_MODEL), BF16),
            pltpu.VMEM((2, PEER_NKEYS, ts), F32),
            packed, packed,
            tables, tables,
            pltpu.VMEM((D_MODEL, ts), F32),
            pltpu.VMEM((2, PEER_EC, ts), F32),
            pltpu.VMEM((2, PEER_EC, ts), BF16),
            pltpu.VMEM((PEER_EC, ts), BF16),
        ],
        compiler_params=_cparams("parallel", "arbitrary"),
        name="peer",
    )(x1, wq_bf, k1_bf, k2_bf, u_bf, u_bf, vt_bf, vt_bf, lg_row, lb_row)


def kernel(x_prompt, x_sample, cache_k, cache_v, state_ssm_re, state_ssm_im, page_table, w_in, b_in, w_a, w_b, w_o, ln1_g, ln1_b, a_re, a_im, log_dt, b_re, b_im, c_re, c_im, d_skip, w_glu, b_glu, ln2_g, ln2_b, w_pq, sub_k1, sub_k2, peer_u, peer_v):
    bn, t, _ = x_prompt.shape
    db, dt_, _ = x_sample.shape
    row = lambda a: a.reshape(1, -1).astype(F32)
    slopes = jnp.exp2(-8.0 * (jnp.arange(N_HEADS, dtype=F32) + 1.0) / N_HEADS)

    w_in_bf = w_in.astype(BF16)
    proj_w = (w_in_bf, row(b_in), w_in_bf[:, ATT_W:3 * ATT_W].T,
              b_in[ATT_W:3 * ATT_W].reshape(2 * ATT_W, 1).astype(F32))
    wa_bf, wb_bf, wo_bf, wglu_bf = (w.astype(BF16) for w in (w_a, w_b, w_o, w_glu))
    peer_w = (w_pq.T.astype(BF16), sub_k1.astype(BF16), sub_k2.astype(BF16))
    u_bf = peer_u.astype(BF16)
    vt_bf = peer_v.T.astype(BF16)

    bbar, ccat, apr, api = _s5_prep(a_re, a_im, log_dt, b_re, b_im, c_re, c_im)
    s5_w = (bbar, apr, api, ccat, row(d_skip), wglu_bf, row(b_glu))

    xp = x_prompt.reshape(bn * t, D_MODEL)
    q, k_t, v_t, u, gates = _proj(xp, *proj_w, seq_len=t)
    att = _moba_prompt(q.reshape(bn, t, ATT_W), k_t, v_t, slopes)
    ssm, hre_p, him_p = _s5_prompt(u.reshape(bn, t, SSM_W), *s5_w)
    x1 = _merge(xp, att.reshape(bn * t, ATT_W), ssm.reshape(bn * t, SSM_W), gates,
                wa_bf, wb_bf, wo_bf, row(ln1_g), row(ln1_b))
    y_prompt = _peer(x1, *peer_w, u_bf, vt_bf, row(ln2_g), row(ln2_b)).reshape(bn, t, D_MODEL)
    per_token = lambda a: a.reshape(bn, N_HEADS, HEAD_DIM, t).transpose(0, 3, 1, 2)
    k_prompt, v_prompt = per_token(k_t), per_token(v_t)

    xs = x_sample.reshape(db * dt_, D_MODEL)
    q, k, v, u, gates = _proj(xs, *proj_w)
    att = _moba_sample(q.reshape(db, dt_, ATT_W), k.reshape(db, dt_, ATT_W), v.reshape(db, dt_, ATT_W),
                       cache_k, cache_v, page_table, slopes)
    ssm, hre_s, him_s = _s5_sample(u.reshape(db, dt_, SSM_W), state_ssm_re, state_ssm_im, *s5_w)
    x1 = _merge(xs, att.reshape(db * dt_, ATT_W), ssm.reshape(db * dt_, SSM_W), gates,
                wa_bf, wb_bf, wo_bf, row(ln1_g), row(ln1_b))
    y_sample = _peer(x1, *peer_w, u_bf, vt_bf, row(ln2_g), row(ln2_b)).reshape(db, dt_, D_MODEL)
    k_sample = k.reshape(db, dt_, N_HEADS, HEAD_DIM)
    v_sample = v.reshape(db, dt_, N_HEADS, HEAD_DIM)

    return (y_prompt, y_sample, k_prompt, v_prompt, k_sample, v_sample,
            hre_p, him_p, hre_s, him_s)
```

```python
import functools
import math

import jax
import jax.numpy as jnp
from jax import lax
from jax.experimental import pallas as pl
from jax.experimental.pallas import tpu as pltpu

F32 = jnp.float32
BF16 = jnp.bfloat16

D_MODEL = 1024
ATT_W = 512
HEAD_DIM = 64
N_HEADS = 8
MOBA_BLOCK = 256
MOBA_TOPK = 3
SSM_W = 512
SSM_GROUP = 16
SSM_GROUPS = 32
SSM_STATE = 64
SSM_FLAT = SSM_GROUPS * SSM_STATE
PEER_HEADS = 8
PEER_NKEYS = 128
PEER_EXPERTS = PEER_NKEYS * PEER_NKEYS
PEER_DKEY = 256
PEER_TOPK = 16
PROJ_W = 3 * ATT_W + SSM_W + 2 * D_MODEL
DEPTH = 1
DN_ALPHA = (2.0 * DEPTH) ** 0.25
LN_EPS = 1e-5
PAGE_SIZE = 128

LANES = 128
SUBLANES = 8
BF16_ROWS = 16
NEG_INF = float("-inf")
VMEM_LIMIT = 56 * 1024 * 1024

NT_DIMS = (((1,), (1,)), ((), ()))


def _cparams(*sem):
    return pltpu.CompilerParams(dimension_semantics=sem, vmem_limit_bytes=VMEM_LIMIT)


def _dot(a, b):
    return jnp.dot(a, b, preferred_element_type=F32)


def _dot_nt(a, b):
    return lax.dot_general(a, b, NT_DIMS, preferred_element_type=F32)


def _split(x):
    hi = x.astype(BF16)
    lo = (x - hi.astype(F32)).astype(BF16)
    return hi, lo


def _gelu(x):
    c = math.sqrt(2.0 / math.pi)
    return 0.5 * x * (1.0 + jnp.tanh(c * (x + 0.044715 * (x * x * x))))


def _layer_norm(y, g, b):
    mu = jnp.mean(y, axis=-1, keepdims=True)
    yc = y - mu
    var = jnp.mean(yc * yc, axis=-1, keepdims=True)
    return yc * lax.rsqrt(var + LN_EPS) * g + b


PROJ_TM = 512


def _proj_kernel(x_ref, w_ref, b_ref, wkv_t_ref, bkv_col_ref, q_ref, k_ref, v_ref, u_ref, g_ref,
                 *, kv_transposed):
    xb = x_ref[...].astype(BF16)

    def seg(lo, hi):
        return _dot(xb, w_ref[:, lo:hi]) + b_ref[:, lo:hi]

    q_ref[...] = seg(0, ATT_W)
    if kv_transposed:
        kv_t = _dot_nt(wkv_t_ref[...], xb) + bkv_col_ref[...]
        k_ref[...] = kv_t[:ATT_W, :]
        v_ref[...] = kv_t[ATT_W:, :]
    else:
        k_ref[...] = seg(ATT_W, 2 * ATT_W)
        v_ref[...] = seg(2 * ATT_W, 3 * ATT_W)
    u_ref[...] = seg(3 * ATT_W, 3 * ATT_W + SSM_W)
    g_ref[...] = jax.nn.sigmoid(seg(3 * ATT_W + SSM_W, PROJ_W))


def _proj(x, w_bf, b_row, wkv_t, bkv_col, seq_len=None):
    n = x.shape[0]
    tm = PROJ_TM
    row = lambda w: pl.BlockSpec((tm, w), lambda i: (i, 0))
    full = lambda a: pl.BlockSpec(a.shape, lambda i: (0,) * a.ndim)
    if seq_len is None:
        kv_spec, kv_shape = row(ATT_W), jax.ShapeDtypeStruct((n, ATT_W), F32)
    else:
        per_seq = seq_len // tm
        kv_spec = pl.BlockSpec((None, ATT_W, tm), lambda i: (i // per_seq, 0, i % per_seq))
        kv_shape = jax.ShapeDtypeStruct((n // seq_len, ATT_W, seq_len), F32)
    rows = lambda w: jax.ShapeDtypeStruct((n, w), F32)
    return pl.pallas_call(
        functools.partial(_proj_kernel, kv_transposed=seq_len is not None),
        grid=(n // tm,),
        in_specs=[row(D_MODEL), full(w_bf), full(b_row), full(wkv_t), full(bkv_col)],
        out_specs=[row(ATT_W), kv_spec, kv_spec, row(SSM_W), row(2 * D_MODEL)],
        out_shape=[rows(ATT_W), kv_shape, kv_shape, rows(SSM_W), rows(2 * D_MODEL)],
        compiler_params=_cparams("parallel"),
        name="proj",
    )(x, w_bf, b_row, wkv_t, bkv_col)


def _moba_select_bias(gates, cur):
    nb = len(gates)
    g = [jnp.where(n < cur, gates[n], NEG_INF) for n in range(nb)]
    bias = []
    for n in range(nb):
        rank = jnp.zeros(gates[n].shape, jnp.int32)
        for m in range(nb):
            if m != n:
                ahead = (g[m] >= g[n]) if m < n else (g[m] > g[n])
                rank = rank + ahead.astype(jnp.int32)
        sel = (n < cur) & (rank < MOBA_TOPK)
        bias.append(jnp.where(sel, 0.0, NEG_INF))
    return bias


def _moba_prompt_kernel(slopes_ref, q_ref, kt_ref, vt_in_ref, o_ref,
                        kb_ref, vt_ref, ot_ref):
    t = q_ref.shape[0]
    nb = t // MOBA_BLOCK
    blk = MOBA_BLOCK
    hp = pl.program_id(1)
    lane = lax.broadcasted_iota(jnp.int32, (1, LANES), 1)

    k = kt_ref[...].T
    kb_ref[...] = k.astype(BF16)
    vt_ref[...] = vt_in_ref[...].astype(BF16)
    kmean = jnp.mean(k.reshape(nb, blk, LANES), axis=1)
    q_all = q_ref[...]
    q_hi, q_lo = _split(q_all)
    cur = lax.broadcasted_iota(jnp.int32, (1, t), 1) // blk
    d0 = (lax.broadcasted_iota(jnp.int32, (blk, blk), 1)
          - lax.broadcasted_iota(jnp.int32, (blk, blk), 0)).astype(F32)

    heads = []
    for hh in range(2):
        hmask = (lane >= HEAD_DIM * hh) & (lane < HEAD_DIM * (hh + 1))
        slope = slopes_ref[2 * hp + hh]
        km_hi, km_lo = _split(jnp.where(hmask, kmean, 0.0))
        gate = _dot_nt(km_hi, q_hi) + _dot_nt(km_hi, q_lo) + _dot_nt(km_lo, q_hi)
        selb = _moba_select_bias([gate[n:n + 1, :] for n in range(nb)], cur)
        heads.append((hmask, slope, selb, slope * d0, slice(HEAD_DIM * hh, HEAD_DIM * (hh + 1))))

    for i in range(nb):
        qcols = slice(i * blk, (i + 1) * blk)
        n_keys = (i + 1) * blk
        q_blk = q_ref[qcols, :] * (HEAD_DIM ** -0.5)
        s_alls = [_dot_nt(kb_ref[0:n_keys, :], jnp.where(hmask, q_blk, 0.0).astype(BF16))
                  for hmask, *_ in heads]
        tiles = []
        for (hmask, slope, selb, slope_d0, rows), s_all in zip(heads, s_alls):
            per_head = []
            for j in range(i + 1):
                s = s_all[j * blk:(j + 1) * blk, :] - (slope_d0 + slope * float((i - j) * blk))
                if j < i:
                    s = s + selb[j][:, qcols]
                else:
                    s = jnp.where(d0 >= 0.0, s, NEG_INF)
                per_head.append(s)
            tiles.append(per_head)
        maxes = []
        for per_head in tiles:
            m = jnp.max(per_head[0], axis=0, keepdims=True)
            for s in per_head[1:]:
                m = jnp.maximum(m, jnp.max(s, axis=0, keepdims=True))
            maxes.append(m)
        probs = [[jnp.exp(s - m) for s in per_head] for per_head, m in zip(tiles, maxes)]
        sums = []
        for per_head in probs:
            l = jnp.sum(per_head[0], axis=0, keepdims=True)
            for p in per_head[1:]:
                l = l + jnp.sum(p, axis=0, keepdims=True)
            sums.append(l)
        for (hmask, slope, selb, slope_d0, rows), per_head, l in zip(heads, probs, sums):
            p_all = jnp.concatenate([p.astype(BF16) for p in per_head], axis=0)
            res = _dot(vt_ref[:, 0:n_keys], p_all) / l
            ot_ref[rows, qcols] = res[rows, :]

    o_ref[...] = ot_ref[...].T.astype(o_ref.dtype)


def _moba_prompt(q, k_t, v_t, slopes):
    b, t, _ = q.shape
    spec = pl.BlockSpec((None, t, LANES), lambda bi, hp, *_: (bi, 0, hp))
    spec_t = pl.BlockSpec((None, LANES, t), lambda bi, hp, *_: (bi, hp, 0))
    grid_spec = pltpu.PrefetchScalarGridSpec(
        num_scalar_prefetch=1,
        grid=(b, ATT_W // LANES),
        in_specs=[spec, spec_t, spec_t],
        out_specs=spec,
        scratch_shapes=[
            pltpu.VMEM((t, LANES), BF16),
            pltpu.VMEM((LANES, t), BF16),
            pltpu.VMEM((LANES, t), F32),
        ],
    )
    return pl.pallas_call(
        _moba_prompt_kernel,
        grid_spec=grid_spec,
        out_shape=jax.ShapeDtypeStruct((b, t, ATT_W), BF16),
        compiler_params=_cparams("parallel", "parallel"),
        name="moba_prompt",
    )(slopes, q, k_t, v_t)


def _moba_sample_kernel(pt_ref, slopes_ref, q_ref, kn_ref, vn_ref, *rest, n_pages):
    kt_refs = rest[:n_pages]
    vt_refs = rest[n_pages:2 * n_pages]
    o_ref = rest[2 * n_pages]
    nq = q_ref.shape[0]
    n_past = n_pages * PAGE_SIZE
    nb = n_past // MOBA_BLOCK
    pages_per_block = MOBA_BLOCK // PAGE_SIZE
    rows = N_HEADS * nq
    scale = HEAD_DIM ** -0.5
    assert rows % SUBLANES == 0 and nq <= PAGE_SIZE

    q = q_ref[...]
    r_id = lax.broadcasted_iota(jnp.int32, (rows, ATT_W), 0)
    c_id = lax.broadcasted_iota(jnp.int32, (rows, ATT_W), 1)
    q_rep = jnp.concatenate([q] * (rows // nq), axis=0)
    own = c_id // HEAD_DIM == r_id // nq
    qs = jnp.where(own, q_rep * scale, 0.0).astype(BF16)

    r_col = lax.broadcasted_iota(jnp.int32, (rows, 1), 0)
    tq = (r_col % nq).astype(F32)
    slope = jnp.zeros((rows, 1), F32)
    for h in range(N_HEADS):
        slope = jnp.where(r_col // nq == h, slopes_ref[h], slope)
    lane = lax.broadcasted_iota(jnp.int32, (1, PAGE_SIZE), 1).astype(F32)

    raw = [_dot(qs, kt_refs[pg][...].astype(BF16)) for pg in range(n_pages)]

    gates = []
    for n in range(nb):
        total = raw[n * pages_per_block]
        for pg in range(n * pages_per_block + 1, (n + 1) * pages_per_block):
            total = total + raw[pg]
        gates.append(jnp.sum(total, axis=1, keepdims=True) * (1.0 / (MOBA_BLOCK * scale)))
    bias = _moba_select_bias(gates, nb)

    tiles = []
    for pg in range(n_pages):
        dist = (tq + float(n_past - pg * PAGE_SIZE)) - lane
        tiles.append(raw[pg] - slope * dist + bias[pg // pages_per_block])
    pad_rows = lambda a: jnp.concatenate([a, jnp.zeros((PAGE_SIZE - nq, ATT_W), F32)], axis=0)
    dist_new = tq - lane
    s_new = _dot_nt(qs, pad_rows(kn_ref[...]).astype(BF16))
    tiles.append(jnp.where(dist_new >= 0.0, s_new - slope * dist_new, NEG_INF))

    top = tiles[0]
    for s in tiles[1:]:
        top = jnp.maximum(top, s)
    m = jnp.max(top, axis=1, keepdims=True)
    probs = [jnp.exp(s - m) for s in tiles]
    total = probs[0]
    for p in probs[1:]:
        total = total + p
    inv_l = 1.0 / jnp.sum(total, axis=1, keepdims=True)
    out = _dot((probs[n_pages] * inv_l).astype(BF16), pad_rows(vn_ref[...]).astype(BF16))
    for pg in range(n_pages):
        out = out + _dot_nt((probs[pg] * inv_l).astype(BF16), vt_refs[pg][...].astype(BF16))

    res = jnp.zeros((nq, ATT_W), F32)
    c_q = lax.broadcasted_iota(jnp.int32, (nq, ATT_W), 1)
    for h in range(N_HEADS):
        res = jnp.where(c_q // HEAD_DIM == h, out[h * nq:(h + 1) * nq, :], res)
    o_ref[...] = res.astype(o_ref.dtype)


def _moba_sample(q, k_new, v_new, cache_k, cache_v, page_table, slopes):
    db, nq, _ = q.shape
    n_pages = page_table.shape[1]
    new_spec = pl.BlockSpec((None, nq, ATT_W), lambda s, *_: (s, 0, 0))

    def page_spec(pg):
        return pl.BlockSpec((None, ATT_W, PAGE_SIZE), lambda s, pt, sl: (pt[s, pg], 0, 0))

    grid_spec = pltpu.PrefetchScalarGridSpec(
        num_scalar_prefetch=2,
        grid=(db,),
        in_specs=([new_spec] * 3 + [page_spec(pg) for pg in range(n_pages)] * 2),
        out_specs=new_spec,
    )
    by_token = lambda c: jnp.transpose(c, (0, 2, 3, 1)).reshape(c.shape[0], ATT_W, PAGE_SIZE)
    ck, cv = by_token(cache_k), by_token(cache_v)
    return pl.pallas_call(
        functools.partial(_moba_sample_kernel, n_pages=n_pages),
        grid_spec=grid_spec,
        out_shape=jax.ShapeDtypeStruct((db, nq, ATT_W), BF16),
        compiler_params=_cparams("parallel"),
        name="moba_sample",
    )(page_table, slopes, q, k_new, v_new, *([ck] * n_pages), *([cv] * n_pages))


S5_CHUNK = 256


def _s5_prep_kernel(are_ref, aim_ref, ldt_ref, bre_ref, bim_ref, cre_ref, cim_ref,
                    bbar_ref, ccat_ref, apr_ref, api_ref):
    ar, ai = are_ref[...], aim_ref[...]
    dt = jnp.exp(ldt_ref[...])
    mag = jnp.exp(dt * ar)
    ang = dt * ai
    abr, abi = mag * jnp.cos(ang), mag * jnp.sin(ang)
    den = ar * ar + ai * ai
    nr, ni = abr - 1.0, abi
    f_re = (nr * ar + ni * ai) / den
    f_im = (ni * ar - nr * ai) / den

    row_g = lax.broadcasted_iota(jnp.int32, (SSM_W, SSM_FLAT), 0) // SSM_GROUP
    col_g = lax.broadcasted_iota(jnp.int32, (SSM_W, SSM_FLAT), 1) // SSM_STATE
    same_group = row_g == col_g

    def block_diag(ref):
        x = ref[...]
        pair = jnp.concatenate([x] * (LANES // SSM_STATE), axis=1)
        return jnp.where(same_group, jnp.concatenate([pair] * (SSM_FLAT // LANES), axis=1), 0.0)

    bre, bim = block_diag(bre_ref), block_diag(bim_ref)
    bbar_ref[:, :SSM_FLAT] = (f_re * bre - f_im * bim).astype(BF16)
    bbar_ref[:, SSM_FLAT:] = (f_re * bim + f_im * bre).astype(BF16)
    ccat_ref[:, :SSM_FLAT] = block_diag(cre_ref).astype(BF16)
    ccat_ref[:, SSM_FLAT:] = (-block_diag(cim_ref)).astype(BF16)
    apr_ref[0:1, :] = abr
    api_ref[0:1, :] = abi
    d = 1
    while d < S5_CHUNK:
        pr, pi = apr_ref[0:d, :], api_ref[0:d, :]
        er, ei = apr_ref[d - 1:d, :], api_ref[d - 1:d, :]
        apr_ref[d:2 * d, :] = pr * er - pi * ei
        api_ref[d:2 * d, :] = pr * ei + pi * er
        d *= 2


def _s5_prep(a_re, a_im, log_dt, b_re, b_im, c_re, c_im):
    flat = lambda a: a.reshape(1, SSM_FLAT)
    ldt = jnp.broadcast_to(log_dt[:, None], (SSM_GROUPS, SSM_STATE))
    by_channel = lambda b: b.transpose(0, 2, 1).reshape(SSM_W, SSM_STATE)
    c_rows = lambda cm: cm.reshape(SSM_W, SSM_STATE)
    return pl.pallas_call(
        _s5_prep_kernel,
        out_shape=[jax.ShapeDtypeStruct((SSM_W, 2 * SSM_FLAT), BF16),
                   jax.ShapeDtypeStruct((SSM_W, 2 * SSM_FLAT), BF16),
                   jax.ShapeDtypeStruct((S5_CHUNK, SSM_FLAT), F32),
                   jax.ShapeDtypeStruct((S5_CHUNK, SSM_FLAT), F32)],
        compiler_params=pltpu.CompilerParams(vmem_limit_bytes=VMEM_LIMIT),
        name="s5_prep",
    )(flat(a_re), flat(a_im), flat(ldt), by_channel(b_re), by_channel(b_im), c_rows(c_re), c_rows(c_im))


def _s5_readout(h_re, h_im, u, ccat_ref, d_ref, wglu_ref, bglu_ref):
    hcat = jnp.concatenate([h_re.astype(BF16), h_im.astype(BF16)], axis=1)
    y = _dot_nt(hcat, ccat_ref[...]) + d_ref[...] * u
    z = _gelu(y)
    return z * jax.nn.sigmoid(_dot(z.astype(BF16), wglu_ref[...]) + bglu_ref[...])


def _s5_prompt_kernel(u_ref, bbar_ref, apr_ref, api_ref, a8r_ref, a8i_ref, ccat_ref, d_ref,
                      wglu_ref, bglu_ref,
                      y_ref, hre_ref, him_ref, cr_ref, ci_ref, hr_ref, hi_ref, pr_ref, pi_ref):
    c = pl.program_id(1)
    n = u_ref.shape[0]
    n_groups = n // SUBLANES

    @pl.when(c == 0)
    def _():
        cr_ref[...] = jnp.zeros_like(cr_ref)
        ci_ref[...] = jnp.zeros_like(ci_ref)

    def scan_level(hr, hi, pos, d, power):
        er, ei = apr_ref[power - 1:power, :], api_ref[power - 1:power, :]
        keep = pos >= d
        sr = jnp.where(keep, pltpu.roll(hr, d, axis=0), 0.0)
        si = jnp.where(keep, pltpu.roll(hi, d, axis=0), 0.0)
        return hr + er * sr - ei * si, hi + er * si + ei * sr

    u = u_ref[...]
    bu = _dot(u.astype(BF16), bbar_ref[...])
    hr, hi = bu[:, :SSM_FLAT], bu[:, SSM_FLAT:]
    hr = hr.reshape(n_groups, SUBLANES, SSM_FLAT)
    hi = hi.reshape(n_groups, SUBLANES, SSM_FLAT)
    in_group = lax.broadcasted_iota(jnp.int32, (SUBLANES, SSM_FLAT), 0)
    d = 1
    while d < SUBLANES:
        er = jnp.where(in_group >= d, apr_ref[d - 1:d, :], 0.0)[None]
        ei = jnp.where(in_group >= d, api_ref[d - 1:d, :], 0.0)[None]
        sr, si = pltpu.roll(hr, d, axis=1), pltpu.roll(hi, d, axis=1)
        hr, hi = hr + er * sr - ei * si, hi + er * si + ei * sr
        d *= 2
    hr = hr.reshape(n, SSM_FLAT)
    hi = hi.reshape(n, SSM_FLAT)
    n_lt = SSM_FLAT // LANES
    lt_cols = [slice(l * LANES, (l + 1) * LANES) for l in range(n_lt)]
    for l in range(n_lt):
        hr_ref[l] = hr[:, lt_cols[l]]
        hi_ref[l] = hi[:, lt_cols[l]]

    ends = pl.ds(SUBLANES - 1, n_groups, stride=SUBLANES)
    gr = jnp.concatenate([hr_ref[l, ends, :] for l in range(n_lt)], axis=1)
    gi = jnp.concatenate([hi_ref[l, ends, :] for l in range(n_lt)], axis=1)
    group = lax.broadcasted_iota(jnp.int32, (n_groups, SSM_FLAT), 0)
    d = 1
    while d < n_groups:
        gr, gi = scan_level(gr, gi, group, d, d * SUBLANES)
        d *= 2
    cr, ci = cr_ref[...], ci_ref[...]
    a8r, a8i = a8r_ref[...], a8i_ref[...]
    gr = gr + a8r * cr - a8i * ci
    gi = gi + a8r * ci + a8i * cr
    cr_ref[...] = gr[n_groups - 1:n_groups, :]
    ci_ref[...] = gi[n_groups - 1:n_groups, :]
    hre_ref[...] = gr[n_groups - 1:n_groups, :]
    him_ref[...] = gi[n_groups - 1:n_groups, :]
    first = group == 0
    pr_ref[...] = jnp.where(first, cr, pltpu.roll(gr, 1, axis=0))
    pi_ref[...] = jnp.where(first, ci, pltpu.roll(gi, 1, axis=0))

    a1r, a1i = apr_ref[0:SUBLANES, :], api_ref[0:SUBLANES, :]
    for g in range(n_groups):
        rows = slice(g * SUBLANES, (g + 1) * SUBLANES)
        br = jnp.broadcast_to(pr_ref[g:g + 1, :], (SUBLANES, SSM_FLAT))
        bi = jnp.broadcast_to(pi_ref[g:g + 1, :], (SUBLANES, SSM_FLAT))
        add_r = a1r * br - a1i * bi
        add_i = a1r * bi + a1i * br
        for l in range(n_lt):
            hr_ref[l, rows, :] = hr_ref[l, rows, :] + add_r[:, lt_cols[l]]
            hi_ref[l, rows, :] = hi_ref[l, rows, :] + add_i[:, lt_cols[l]]
    hr = jnp.concatenate([hr_ref[l] for l in range(n_lt)], axis=1)
    hi = jnp.concatenate([hi_ref[l] for l in range(n_lt)], axis=1)
    y_ref[...] = _s5_readout(hr, hi, u, ccat_ref, d_ref, wglu_ref, bglu_ref).astype(y_ref.dtype)


def _s5_prompt(u, bbar, apr, api, ccat, d_row, wglu_bf, bglu_row):
    b, t, _ = u.shape
    n = S5_CHUNK
    full = lambda a: pl.BlockSpec(a.shape, lambda bi, c: (0,) * a.ndim)
    tok = pl.BlockSpec((None, n, SSM_W), lambda bi, c: (bi, c, 0))
    st = pl.BlockSpec((None, 1, SSM_FLAT), lambda bi, c: (bi, 0, 0))
    a8r, a8i = apr[SUBLANES - 1::SUBLANES], api[SUBLANES - 1::SUBLANES]
    lane_major = pltpu.VMEM((SSM_FLAT // LANES, n, LANES), F32)
    y, hre, him = pl.pallas_call(
        _s5_prompt_kernel,
        grid=(b, t // n),
        in_specs=[tok, full(bbar), full(apr), full(api), full(a8r), full(a8i), full(ccat),
                  full(d_row), full(wglu_bf), full(bglu_row)],
        out_specs=[tok, st, st],
        out_shape=[jax.ShapeDtypeStruct((b, t, SSM_W), BF16),
                   jax.ShapeDtypeStruct((b, 1, SSM_FLAT), F32),
                   jax.ShapeDtypeStruct((b, 1, SSM_FLAT), F32)],
        scratch_shapes=[pltpu.VMEM((1, SSM_FLAT), F32), pltpu.VMEM((1, SSM_FLAT), F32),
                        lane_major, lane_major,
                        pltpu.VMEM((n // SUBLANES, SSM_FLAT), F32),
                        pltpu.VMEM((n // SUBLANES, SSM_FLAT), F32)],
        compiler_params=_cparams("parallel", "arbitrary"),
        name="s5_prompt",
    )(u, bbar, apr, api, a8r, a8i, ccat, d_row, wglu_bf, bglu_row)
    return y, hre.reshape(b, SSM_GROUPS, SSM_STATE), him.reshape(b, SSM_GROUPS, SSM_STATE)


def _s5_sample_kernel(u_ref, h0r_ref, h0i_ref, bbar_ref, apr_ref, api_ref, ccat_ref, d_ref,
                      wglu_ref, bglu_ref, y_ref, hre_ref, him_ref, *, n_steps):
    hr, hi = h0r_ref[...], h0i_ref[...]
    ar, ai = apr_ref[0:1, :], api_ref[0:1, :]
    for t in range(n_steps):
        u = u_ref[:, t * SSM_W:(t + 1) * SSM_W]
        bu = _dot(u.astype(BF16), bbar_ref[...])
        hr, hi = (ar * hr - ai * hi + bu[:, :SSM_FLAT],
                  ar * hi + ai * hr + bu[:, SSM_FLAT:])
        y_ref[:, t * SSM_W:(t + 1) * SSM_W] = _s5_readout(
            hr, hi, u, ccat_ref, d_ref, wglu_ref, bglu_ref).astype(y_ref.dtype)
    hre_ref[...] = hr
    him_ref[...] = hi


def _s5_sample(u, h0_re, h0_im, bbar, apr, api, ccat, d_row, wglu_bf, bglu_row):
    db, t, _ = u.shape
    y, hre, him = pl.pallas_call(
        functools.partial(_s5_sample_kernel, n_steps=t),
        out_shape=[jax.ShapeDtypeStruct((db, t * SSM_W), BF16),
                   jax.ShapeDtypeStruct((db, SSM_FLAT), F32),
                   jax.ShapeDtypeStruct((db, SSM_FLAT), F32)],
        compiler_params=pltpu.CompilerParams(vmem_limit_bytes=VMEM_LIMIT),
        name="s5_sample",
    )(u.reshape(db, t * SSM_W), h0_re.reshape(db, SSM_FLAT), h0_im.reshape(db, SSM_FLAT),
      bbar, apr, api, ccat, d_row, wglu_bf, bglu_row)
    return (y.reshape(db, t, SSM_W), hre.reshape(db, SSM_GROUPS, SSM_STATE),
            him.reshape(db, SSM_GROUPS, SSM_STATE))


MERGE_TM = 512


def _merge_kernel(x_ref, att_ref, ssm_ref, g_ref, wa_ref, wb_ref, wo_ref, lg_ref, lb_ref, o_ref):
    ya = _dot(att_ref[...], wa_ref[...])
    yb = _dot(ssm_ref[...], wb_ref[...])
    merged = g_ref[:, :D_MODEL] * ya + g_ref[:, D_MODEL:] * yb
    y = DN_ALPHA * x_ref[...] + _dot(merged.astype(BF16), wo_ref[...])
    o_ref[...] = _layer_norm(y, lg_ref[...], lb_ref[...])


def _merge(x, att, ssm, gates, wa_bf, wb_bf, wo_bf, lg_row, lb_row):
    n = x.shape[0]
    tm = MERGE_TM
    row = lambda w: pl.BlockSpec((tm, w), lambda i: (i, 0))
    full = lambda a: pl.BlockSpec(a.shape, lambda i: (0,) * a.ndim)
    return pl.pallas_call(
        _merge_kernel,
        grid=(n // tm,),
        in_specs=[row(D_MODEL), row(ATT_W), row(SSM_W), row(2 * D_MODEL),
                  full(wa_bf), full(wb_bf), full(wo_bf), full(lg_row), full(lb_row)],
        out_specs=row(D_MODEL),
        out_shape=jax.ShapeDtypeStruct((n, D_MODEL), F32),
        compiler_params=_cparams("parallel"),
        name="merge",
    )(x, att, ssm, gates, wa_bf, wb_bf, wo_bf, lg_row, lb_row)


PEER_TS = 512
PEER_EC = 1024
PEER_NCH = PEER_EXPERTS // PEER_EC
PEER_ROWS_PER_STEP = PEER_EC // PEER_NKEYS
HALF_KEY = PEER_DKEY // 2


def _sorting_network(n):
    pairs = []
    t = (n - 1).bit_length()
    p = 1 << (t - 1)
    while p > 0:
        q, r, d = 1 << (t - 1), 0, p
        while d > 0:
            pairs.extend((i, i + d) for i in range(n - d) if (i & p) == r)
            d, q, r = q - p, q >> 1, p
        p >>= 1
    return pairs


def _top_sorted(s, k):
    assert s.shape[0] == k * SUBLANES and k & (k - 1) == 0
    v = [s[i * SUBLANES:(i + 1) * SUBLANES, :] for i in range(k)]
    for i, j in _sorting_network(k):
        v[i], v[j] = jnp.maximum(v[i], v[j]), jnp.minimum(v[i], v[j])
    shift = SUBLANES // 2
    while shift:
        other = [pltpu.roll(x, shift, axis=0) for x in v]
        v = [jnp.maximum(v[i], other[k - 1 - i]) for i in range(k)]
        stride = k // 2
        while stride:
            for i in range(k):
                if (i // stride) % 2 == 0:
                    lo, hi = v[i], v[i + stride]
                    v[i], v[i + stride] = jnp.maximum(lo, hi), jnp.minimum(lo, hi)
            stride //= 2
        shift //= 2
    return v


def _peer_tables(s1, s2):
    k = PEER_TOPK
    c = s1.shape[1]
    groups3 = (k, SUBLANES, c)
    v1 = _top_sorted(s1, k)
    v2 = _top_sorted(s2, k)
    s2g = s2.reshape(groups3)
    rank2 = jnp.zeros(groups3, F32)
    for b in range(k):
        rank2 = jnp.where(s2g < v2[b][None], float(b + 1), rank2)
    sub = lax.broadcasted_iota(jnp.int32, (SUBLANES, c), 0)

    def pack(rows):
        out = rows[0]
        for r in range(1, SUBLANES):
            out = jnp.where(sub == r, rows[r], out)
        return out

    v2_lo, v2_hi, v1_hi = pack(v2[:SUBLANES]), pack(v2[SUBLANES:]), pack(v1[SUBLANES:])
    cands = [v1[0] + v2_lo, v1[0] + v2_hi, v1[1] + v2_lo]
    for a in range(2, SUBLANES):
        cands.append(jnp.where(sub < k // (a + 1), v1[a] + v2_lo, NEG_INF))
    cands.append(v1_hi + v2[0])
    cands += [jnp.full((SUBLANES, c), NEG_INF, F32)] * (k - len(cands))
    sc = _top_sorted(jnp.concatenate(cands, axis=0), k)
    tau = sc[k - 1]
    z = jnp.zeros_like(tau)
    for r in range(k):
        z = z + jnp.exp(sc[r] - sc[0])
    s1g = s1.reshape(groups3)
    count = jnp.zeros(groups3, F32)
    for a in range(k):
        n_sel = jnp.zeros_like(tau)
        for b in range(k // (a + 1)):
            n_sel = n_sel + jnp.where(v1[a] + v2[b] >= tau, 1.0, 0.0)
        count = jnp.where(s1g == v1[a][None], n_sel[None], count)
    row_weight = jnp.exp(s1g - v1[0][None]) / z[None]
    p2 = jnp.exp(s2g - v2[0][None])
    flat = lambda a: a.reshape(k * SUBLANES, c)
    return flat(rank2), flat(p2), flat(count), flat(row_weight)


def _peer_kernel(x_ref, wq_ref, k1_ref, k2_ref,
                 u0_ref, un_ref, vp_ref, vl_ref, lg_ref, lb_ref, o_ref,
                 xb_ref, s_ref, r2_ref, p2_ref, cnt_ref, cw_ref, acc_ref, st_ref, ht_ref, w_ref):
    c = pl.program_id(1)
    ts = x_ref.shape[0]
    groups = PEER_NKEYS // BF16_ROWS
    cur = c % 2
    nxt = 1 - cur

    @pl.when(c == 0)
    def _prologue():
        xb = x_ref[...].astype(BF16)
        xb_ref[...] = xb
        st_ref[0] = _dot_nt(u0_ref[...], xb).astype(BF16)
        qt = _dot_nt(wq_ref[...], xb).astype(BF16)
        for h in range(PEER_HEADS):
            base = h * PEER_DKEY
            s_ref[0] = _dot(k1_ref[h], qt[base:base + HALF_KEY, :])
            s_ref[1] = _dot(k2_ref[h], qt[base + HALF_KEY:base + PEER_DKEY, :])

            def lane_tile(lt, _):
                cols = pl.ds(pl.multiple_of(lt * LANES, LANES), LANES)
                rank2, p2, count, row_weight = _peer_tables(s_ref[0, :, cols], s_ref[1, :, cols])
                r2_ref[h, :, :, cols] = rank2.astype(BF16).reshape(groups, BF16_ROWS, LANES)
                p2_ref[h, :, :, cols] = p2.astype(BF16).reshape(groups, BF16_ROWS, LANES)
                cnt_ref[h, :, cols] = count
                cw_ref[h, :, cols] = row_weight
                return 0

            lax.fori_loop(0, ts // LANES, lane_tile, 0)
        acc_ref[...] = jnp.zeros_like(acc_ref)
        ht_ref[1] = jnp.zeros((PEER_EC, ts), BF16)

    i1_base = pl.multiple_of(c * PEER_ROWS_PER_STEP, PEER_ROWS_PER_STEP)
    cnts = [cnt_ref[h, pl.ds(i1_base, PEER_ROWS_PER_STEP), :] for h in range(PEER_HEADS)]
    cws = [cw_ref[h, pl.ds(i1_base, PEER_ROWS_PER_STEP), :] for h in range(PEER_HEADS)]
    mblk = 2 * PEER_NKEYS
    for j in range(PEER_ROWS_PER_STEP):
        rows = slice(j * PEER_NKEYS, (j + 1) * PEER_NKEYS)
        w = jnp.zeros((groups, BF16_ROWS, ts), BF16)
        for h in range(PEER_HEADS):
            cnt = jnp.broadcast_to(cnts[h][j:j + 1, :], (BF16_ROWS, ts)).astype(BF16)
            cw = jnp.broadcast_to(cws[h][j:j + 1, :], (BF16_ROWS, ts)).astype(BF16)
            w = w + jnp.where(r2_ref[h] < cnt[None], cw[None], 0.0) * p2_ref[h]
        w_ref[rows, :] = w.reshape(PEER_NKEYS, ts)
        if j % 2 == 1:
            er = slice((j // 2) * mblk, (j // 2 + 1) * mblk)
            st_ref[nxt, er, :] = _dot_nt(un_ref[er, :], xb_ref[...]).astype(BF16)
    acc_ref[...] += _dot(vp_ref[...], ht_ref[nxt])
    for j in range(PEER_ROWS_PER_STEP):
        rows = slice(j * PEER_NKEYS, (j + 1) * PEER_NKEYS)
        ht_ref[cur, rows, :] = w_ref[rows, :] * _gelu(st_ref[cur, rows, :])

    @pl.when(c == PEER_NCH - 1)
    def _epilogue():
        acc = acc_ref[...] + _dot(vl_ref[...], ht_ref[cur])
        y = DN_ALPHA * x_ref[...] + acc.T
        o_ref[...] = _layer_norm(y, lg_ref[...], lb_ref[...])


def _peer(x1, wq_bf, k1_bf, k2_bf, u_bf, vt_bf, lg_row, lb_row):
    n = x1.shape[0]
    ts = PEER_TS
    last = PEER_NCH - 1
    full = lambda a: pl.BlockSpec(a.shape, lambda i, c: (0,) * a.ndim)
    tok = pl.BlockSpec((ts, D_MODEL), lambda i, c: (i, 0))
    tables = pltpu.VMEM((PEER_HEADS, PEER_NKEYS, ts), F32)
    packed = pltpu.VMEM((PEER_HEADS, PEER_NKEYS // BF16_ROWS, BF16_ROWS, ts), BF16)
    return pl.pallas_call(
        _peer_kernel,
        grid=(n // ts, PEER_NCH),
        in_specs=[tok, full(wq_bf), full(k1_bf), full(k2_bf),
                  pl.BlockSpec((PEER_EC, D_MODEL), lambda i, c: (0, 0)),
                  pl.BlockSpec((PEER_EC, D_MODEL), lambda i, c: (jnp.minimum(c + 1, last), 0)),
                  pl.BlockSpec((D_MODEL, PEER_EC), lambda i, c: (0, jnp.maximum(c - 1, 0))),
                  pl.BlockSpec((D_MODEL, PEER_EC), lambda i, c: (0, last)),
                  full(lg_row), full(lb_row)],
        out_specs=tok,
        out_shape=jax.ShapeDtypeStruct((n, D_MODEL), F32),
        scratch_shapes=[
            pltpu.VMEM((ts, D_MODEL), BF16),
            pltpu.VMEM((2, PEER_NKEYS, ts), F32),
            packed, packed,
            tables, tables,
            pltpu.VMEM((D_MODEL, ts), F32),
            pltpu.VMEM((2, PEER_EC, ts), BF16),
            pltpu.VMEM((2, PEER_EC, ts), BF16),
            pltpu.VMEM((PEER_EC, ts), BF16),
        ],
        compiler_params=_cparams("parallel", "arbitrary"),
        name="peer",
    )(x1, wq_bf, k1_bf, k2_bf, u_bf, u_bf, vt_bf, vt_bf, lg_row, lb_row)


def kernel(x_prompt, x_sample, cache_k, cache_v, state_ssm_re, state_ssm_im, page_table, w_in, b_in, w_a, w_b, w_o, ln1_g, ln1_b, a_re, a_im, log_dt, b_re, b_im, c_re, c_im, d_skip, w_glu, b_glu, ln2_g, ln2_b, w_pq, sub_k1, sub_k2, peer_u, peer_v):
    bn, t, _ = x_prompt.shape
    db, dt_, _ = x_sample.shape
    row = lambda a: a.reshape(1, -1).astype(F32)
    slopes = jnp.exp2(-8.0 * (jnp.arange(N_HEADS, dtype=F32) + 1.0) / N_HEADS)

    w_in_bf = w_in.astype(BF16)
    proj_w = (w_in_bf, row(b_in), w_in[:, ATT_W:3 * ATT_W].T.astype(BF16),
              b_in[ATT_W:3 * ATT_W].reshape(2 * ATT_W, 1).astype(F32))
    wa_bf, wb_bf, wo_bf, wglu_bf = (w.astype(BF16) for w in (w_a, w_b, w_o, w_glu))
    peer_w = (w_pq.T.astype(BF16), sub_k1.astype(BF16), sub_k2.astype(BF16))
    u_bf = peer_u.astype(BF16)
    vt_bf = peer_v.T.astype(BF16)

    bbar, ccat, apr, api = _s5_prep(a_re, a_im, log_dt, b_re, b_im, c_re, c_im)
    s5_w = (bbar, apr, api, ccat, row(d_skip), wglu_bf, row(b_glu))

    xp = x_prompt.reshape(bn * t, D_MODEL)
    q, k_t, v_t, u, gates = _proj(xp, *proj_w, seq_len=t)
    att = _moba_prompt(q.reshape(bn, t, ATT_W), k_t, v_t, slopes)
    ssm, hre_p, him_p = _s5_prompt(u.reshape(bn, t, SSM_W), *s5_w)
    x1 = _merge(xp, att.reshape(bn * t, ATT_W), ssm.reshape(bn * t, SSM_W), gates,
                wa_bf, wb_bf, wo_bf, row(ln1_g), row(ln1_b))
    y_prompt = _peer(x1, *peer_w, u_bf, vt_bf, row(ln2_g), row(ln2_b)).reshape(bn, t, D_MODEL)
    per_token = lambda a: a.reshape(bn, N_HEADS, HEAD_DIM, t).transpose(0, 3, 1, 2)
    k_prompt, v_prompt = per_token(k_t), per_token(v_t)

    xs = x_sample.reshape(db * dt_, D_MODEL)
    q, k, v, u, gates = _proj(xs, *proj_w)
    att = _moba_sample(q.reshape(db, dt_, ATT_W), k.reshape(db, dt_, ATT_W), v.reshape(db, dt_, ATT_W),
                       cache_k, cache_v, page_table, slopes)
    ssm, hre_s, him_s = _s5_sample(u.reshape(db, dt_, SSM_W), state_ssm_re, state_ssm_im, *s5_w)
    x1 = _merge(xs, att.reshape(db * dt_, ATT_W), ssm.reshape(db * dt_, SSM_W), gates,
                wa_bf, wb_bf, wo_bf, row(ln1_g), row(ln1_b))
    y_sample = _peer(x1, *peer_w, u_bf, vt_bf, row(ln2_g), row(ln2_b)).reshape(db, dt_, D_MODEL)
    k_sample = k.reshape(db, dt_, N_HEADS, HEAD_DIM)
    v_sample = v.reshape(db, dt_, N_HEADS, HEAD_DIM)

    return (y_prompt, y_sample, k_prompt, v_prompt, k_sample, v_sample,
            hre_p, him_p, hre_s, him_s)
```

```python
import functools
import math

import jax
import jax.numpy as jnp
from jax import lax
from jax.experimental import pallas as pl
from jax.experimental.pallas import tpu as pltpu

F32 = jnp.float32
BF16 = jnp.bfloat16

D_MODEL = 1024
ATT_W = 512
HEAD_DIM = 64
N_HEADS = 8
MOBA_BLOCK = 256
MOBA_TOPK = 3
SSM_W = 512
SSM_GROUP = 16
SSM_GROUPS = 32
SSM_STATE = 64
SSM_FLAT = SSM_GROUPS * SSM_STATE
PEER_HEADS = 8
PEER_NKEYS = 128
PEER_EXPERTS = PEER_NKEYS * PEER_NKEYS
PEER_DKEY = 256
PEER_TOPK = 16
PROJ_W = 3 * ATT_W + SSM_W + 2 * D_MODEL
DEPTH = 1
DN_ALPHA = (2.0 * DEPTH) ** 0.25
LN_EPS = 1e-5
PAGE_SIZE = 128

LANES = 128
SUBLANES = 8
BF16_ROWS = 16
NEG_INF = float("-inf")
VMEM_LIMIT = 56 * 1024 * 1024

NT_DIMS = (((1,), (1,)), ((), ()))


def _cparams(*sem):
    return pltpu.CompilerParams(dimension_semantics=sem, vmem_limit_bytes=VMEM_LIMIT)


def _dot(a, b):
    return jnp.dot(a, b, preferred_element_type=F32)


def _dot_nt(a, b):
    return lax.dot_general(a, b, NT_DIMS, preferred_element_type=F32)


def _split(x):
    hi = x.astype(BF16)
    lo = (x - hi.astype(F32)).astype(BF16)
    return hi, lo


def _gelu(x):
    c = math.sqrt(2.0 / math.pi)
    return 0.5 * x * (1.0 + jnp.tanh(c * (x + 0.044715 * (x * x * x))))


def _layer_norm(y, g, b):
    mu = jnp.mean(y, axis=-1, keepdims=True)
    yc = y - mu
    var = jnp.mean(yc * yc, axis=-1, keepdims=True)
    return yc * lax.rsqrt(var + LN_EPS) * g + b


PROJ_TM = 512


def _proj_kernel(x_ref, w_ref, b_ref, wkv_t_ref, bkv_col_ref, q_ref, k_ref, v_ref, u_ref, g_ref,
                 *, kv_transposed):
    xb = x_ref[...].astype(BF16)

    def seg(lo, hi):
        return _dot(xb, w_ref[:, lo:hi]) + b_ref[:, lo:hi]

    q_ref[...] = seg(0, ATT_W)
    if kv_transposed:
        kv_t = _dot_nt(wkv_t_ref[...], xb) + bkv_col_ref[...]
        k_ref[...] = kv_t[:ATT_W, :]
        v_ref[...] = kv_t[ATT_W:, :]
    else:
        k_ref[...] = seg(ATT_W, 2 * ATT_W)
        v_ref[...] = seg(2 * ATT_W, 3 * ATT_W)
    u_ref[...] = seg(3 * ATT_W, 3 * ATT_W + SSM_W)
    g_ref[...] = jax.nn.sigmoid(seg(3 * ATT_W + SSM_W, PROJ_W))


def _proj(x, w_bf, b_row, wkv_t, bkv_col, seq_len=None):
    n = x.shape[0]
    tm = PROJ_TM
    row = lambda w: pl.BlockSpec((tm, w), lambda i: (i, 0))
    full = lambda a: pl.BlockSpec(a.shape, lambda i: (0,) * a.ndim)
    if seq_len is None:
        kv_spec, kv_shape = row(ATT_W), jax.ShapeDtypeStruct((n, ATT_W), F32)
    else:
        per_seq = seq_len // tm
        kv_spec = pl.BlockSpec((None, ATT_W, tm), lambda i: (i // per_seq, 0, i % per_seq))
        kv_shape = jax.ShapeDtypeStruct((n // seq_len, ATT_W, seq_len), F32)
    rows = lambda w: jax.ShapeDtypeStruct((n, w), F32)
    return pl.pallas_call(
        functools.partial(_proj_kernel, kv_transposed=seq_len is not None),
        grid=(n // tm,),
        in_specs=[row(D_MODEL), full(w_bf), full(b_row), full(wkv_t), full(bkv_col)],
        out_specs=[row(ATT_W), kv_spec, kv_spec, row(SSM_W), row(2 * D_MODEL)],
        out_shape=[rows(ATT_W), kv_shape, kv_shape, rows(SSM_W), rows(2 * D_MODEL)],
        compiler_params=_cparams("parallel"),
        name="proj",
    )(x, w_bf, b_row, wkv_t, bkv_col)


def _moba_select_bias(gates, cur):
    nb = len(gates)
    g = [jnp.where(n < cur, gates[n], NEG_INF) for n in range(nb)]
    bias = []
    for n in range(nb):
        rank = jnp.zeros(gates[n].shape, jnp.int32)
        for m in range(nb):
            if m != n:
                ahead = (g[m] >= g[n]) if m < n else (g[m] > g[n])
                rank = rank + ahead.astype(jnp.int32)
        sel = (n < cur) & (rank < MOBA_TOPK)
        bias.append(jnp.where(sel, 0.0, NEG_INF))
    return bias


def _moba_prompt_kernel(slopes_ref, q_ref, kt_ref, vt_in_ref, o_ref,
                        kb_ref, vt_ref, ot_ref):
    t = q_ref.shape[0]
    nb = t // MOBA_BLOCK
    blk = MOBA_BLOCK
    hp = pl.program_id(1)
    lane = lax.broadcasted_iota(jnp.int32, (1, LANES), 1)

    k = kt_ref[...].T
    kb_ref[...] = k.astype(BF16)
    vt_ref[...] = vt_in_ref[...].astype(BF16)
    kmean = jnp.mean(k.reshape(nb, blk, LANES), axis=1)
    q_all = q_ref[...]
    q_hi, q_lo = _split(q_all)
    cur = lax.broadcasted_iota(jnp.int32, (1, t), 1) // blk
    d0 = (lax.broadcasted_iota(jnp.int32, (blk, blk), 1)
          - lax.broadcasted_iota(jnp.int32, (blk, blk), 0)).astype(F32)

    heads = []
    for hh in range(2):
        hmask = (lane >= HEAD_DIM * hh) & (lane < HEAD_DIM * (hh + 1))
        slope = slopes_ref[2 * hp + hh]
        km_hi, km_lo = _split(jnp.where(hmask, kmean, 0.0))
        gate = _dot_nt(km_hi, q_hi) + _dot_nt(km_hi, q_lo) + _dot_nt(km_lo, q_hi)
        selb = _moba_select_bias([gate[n:n + 1, :] for n in range(nb)], cur)
        heads.append((hmask, slope, selb, slope * d0, slice(HEAD_DIM * hh, HEAD_DIM * (hh + 1))))

    for i in range(nb):
        qcols = slice(i * blk, (i + 1) * blk)
        n_keys = (i + 1) * blk
        q_blk = q_ref[qcols, :] * (HEAD_DIM ** -0.5)
        s_alls = [_dot_nt(kb_ref[0:n_keys, :], jnp.where(hmask, q_blk, 0.0).astype(BF16))
                  for hmask, *_ in heads]
        tiles = []
        for (hmask, slope, selb, slope_d0, rows), s_all in zip(heads, s_alls):
            per_head = []
            for j in range(i + 1):
                s = s_all[j * blk:(j + 1) * blk, :] - (slope_d0 + slope * float((i - j) * blk))
                if j < i:
                    s = s + selb[j][:, qcols]
                else:
                    s = jnp.where(d0 >= 0.0, s, NEG_INF)
                per_head.append(s)
            tiles.append(per_head)
        maxes = []
        for per_head in tiles:
            m = jnp.max(per_head[0], axis=0, keepdims=True)
            for s in per_head[1:]:
                m = jnp.maximum(m, jnp.max(s, axis=0, keepdims=True))
            maxes.append(m)
        probs = [[jnp.exp(s - m) for s in per_head] for per_head, m in zip(tiles, maxes)]
        sums = []
        for per_head in probs:
            l = jnp.sum(per_head[0], axis=0, keepdims=True)
            for p in per_head[1:]:
                l = l + jnp.sum(p, axis=0, keepdims=True)
            sums.append(l)
        for (hmask, slope, selb, slope_d0, rows), per_head, l in zip(heads, probs, sums):
            p_all = jnp.concatenate([p.astype(BF16) for p in per_head], axis=0)
            res = _dot(vt_ref[:, 0:n_keys], p_all) / l
            ot_ref[rows, qcols] = res[rows, :]

    o_ref[...] = ot_ref[...].T.astype(o_ref.dtype)


def _moba_prompt(q, k_t, v_t, slopes):
    b, t, _ = q.shape
    spec = pl.BlockSpec((None, t, LANES), lambda bi, hp, *_: (bi, 0, hp))
    spec_t = pl.BlockSpec((None, LANES, t), lambda bi, hp, *_: (bi, hp, 0))
    grid_spec = pltpu.PrefetchScalarGridSpec(
        num_scalar_prefetch=1,
        grid=(b, ATT_W // LANES),
        in_specs=[spec, spec_t, spec_t],
        out_specs=spec,
        scratch_shapes=[
            pltpu.VMEM((t, LANES), BF16),
            pltpu.VMEM((LANES, t), BF16),
            pltpu.VMEM((LANES, t), F32),
        ],
    )
    return pl.pallas_call(
        _moba_prompt_kernel,
        grid_spec=grid_spec,
        out_shape=jax.ShapeDtypeStruct((b, t, ATT_W), BF16),
        compiler_params=_cparams("parallel", "parallel"),
        name="moba_prompt",
    )(slopes, q, k_t, v_t)


def _moba_sample_kernel(pt_ref, slopes_ref, q_ref, kn_ref, vn_ref, *rest, n_pages):
    kt_refs = rest[:n_pages]
    vt_refs = rest[n_pages:2 * n_pages]
    o_ref = rest[2 * n_pages]
    nq = q_ref.shape[0]
    n_past = n_pages * PAGE_SIZE
    nb = n_past // MOBA_BLOCK
    pages_per_block = MOBA_BLOCK // PAGE_SIZE
    rows = N_HEADS * nq
    scale = HEAD_DIM ** -0.5
    assert rows % SUBLANES == 0 and nq <= PAGE_SIZE

    q = q_ref[...]
    r_id = lax.broadcasted_iota(jnp.int32, (rows, ATT_W), 0)
    c_id = lax.broadcasted_iota(jnp.int32, (rows, ATT_W), 1)
    q_rep = jnp.concatenate([q] * (rows // nq), axis=0)
    own = c_id // HEAD_DIM == r_id // nq
    qs = jnp.where(own, q_rep * scale, 0.0).astype(BF16)

    r_col = lax.broadcasted_iota(jnp.int32, (rows, 1), 0)
    tq = (r_col % nq).astype(F32)
    slope = jnp.zeros((rows, 1), F32)
    for h in range(N_HEADS):
        slope = jnp.where(r_col // nq == h, slopes_ref[h], slope)
    lane = lax.broadcasted_iota(jnp.int32, (1, PAGE_SIZE), 1).astype(F32)

    raw = [_dot(qs, kt_refs[pg][...].astype(BF16)) for pg in range(n_pages)]

    gates = []
    for n in range(nb):
        total = raw[n * pages_per_block]
        for pg in range(n * pages_per_block + 1, (n + 1) * pages_per_block):
            total = total + raw[pg]
        gates.append(jnp.sum(total, axis=1, keepdims=True) * (1.0 / (MOBA_BLOCK * scale)))
    bias = _moba_select_bias(gates, nb)

    tiles = []
    for pg in range(n_pages):
        dist = (tq + float(n_past - pg * PAGE_SIZE)) - lane
        tiles.append(raw[pg] - slope * dist + bias[pg // pages_per_block])
    pad_rows = lambda a: jnp.concatenate([a, jnp.zeros((PAGE_SIZE - nq, ATT_W), F32)], axis=0)
    dist_new = tq - lane
    s_new = _dot_nt(qs, pad_rows(kn_ref[...]).astype(BF16))
    tiles.append(jnp.where(dist_new >= 0.0, s_new - slope * dist_new, NEG_INF))

    top = tiles[0]
    for s in tiles[1:]:
        top = jnp.maximum(top, s)
    m = jnp.max(top, axis=1, keepdims=True)
    probs = [jnp.exp(s - m) for s in tiles]
    total = probs[0]
    for p in probs[1:]:
        total = total + p
    inv_l = 1.0 / jnp.sum(total, axis=1, keepdims=True)
    out = _dot((probs[n_pages] * inv_l).astype(BF16), pad_rows(vn_ref[...]).astype(BF16))
    for pg in range(n_pages):
        out = out + _dot_nt((probs[pg] * inv_l).astype(BF16), vt_refs[pg][...].astype(BF16))

    res = jnp.zeros((nq, ATT_W), F32)
    c_q = lax.broadcasted_iota(jnp.int32, (nq, ATT_W), 1)
    for h in range(N_HEADS):
        res = jnp.where(c_q // HEAD_DIM == h, out[h * nq:(h + 1) * nq, :], res)
    o_ref[...] = res.astype(o_ref.dtype)


def _moba_sample(q, k_new, v_new, cache_k, cache_v, page_table, slopes):
    db, nq, _ = q.shape
    n_pages = page_table.shape[1]
    new_spec = pl.BlockSpec((None, nq, ATT_W), lambda s, *_: (s, 0, 0))

    def page_spec(pg):
        return pl.BlockSpec((None, ATT_W, PAGE_SIZE), lambda s, pt, sl: (pt[s, pg], 0, 0))

    grid_spec = pltpu.PrefetchScalarGridSpec(
        num_scalar_prefetch=2,
        grid=(db,),
        in_specs=([new_spec] * 3 + [page_spec(pg) for pg in range(n_pages)] * 2),
        out_specs=new_spec,
    )
    by_token = lambda c: jnp.transpose(c, (0, 2, 3, 1)).reshape(c.shape[0], ATT_W, PAGE_SIZE)
    ck, cv = by_token(cache_k), by_token(cache_v)
    return pl.pallas_call(
        functools.partial(_moba_sample_kernel, n_pages=n_pages),
        grid_spec=grid_spec,
        out_shape=jax.ShapeDtypeStruct((db, nq, ATT_W), BF16),
        compiler_params=_cparams("parallel"),
        name="moba_sample",
    )(page_table, slopes, q, k_new, v_new, *([ck] * n_pages), *([cv] * n_pages))


S5_CHUNK = 256


def _s5_prep_kernel(are_ref, aim_ref, ldt_ref, bre_ref, bim_ref, cre_ref, cim_ref,
                    bbar_ref, ccat_ref, apr_ref, api_ref):
    ar, ai = are_ref[...], aim_ref[...]
    dt = jnp.exp(ldt_ref[...])
    mag = jnp.exp(dt * ar)
    ang = dt * ai
    abr, abi = mag * jnp.cos(ang), mag * jnp.sin(ang)
    den = ar * ar + ai * ai
    nr, ni = abr - 1.0, abi
    f_re = (nr * ar + ni * ai) / den
    f_im = (ni * ar - nr * ai) / den

    row_g = lax.broadcasted_iota(jnp.int32, (SSM_W, SSM_FLAT), 0) // SSM_GROUP
    col_g = lax.broadcasted_iota(jnp.int32, (SSM_W, SSM_FLAT), 1) // SSM_STATE
    same_group = row_g == col_g

    def block_diag(ref):
        x = ref[...]
        pair = jnp.concatenate([x] * (LANES // SSM_STATE), axis=1)
        return jnp.where(same_group, jnp.concatenate([pair] * (SSM_FLAT // LANES), axis=1), 0.0)

    bre, bim = block_diag(bre_ref), block_diag(bim_ref)
    bbar_ref[:, :SSM_FLAT] = (f_re * bre - f_im * bim).astype(BF16)
    bbar_ref[:, SSM_FLAT:] = (f_re * bim + f_im * bre).astype(BF16)
    ccat_ref[:, :SSM_FLAT] = block_diag(cre_ref).astype(BF16)
    ccat_ref[:, SSM_FLAT:] = (-block_diag(cim_ref)).astype(BF16)
    apr_ref[0:1, :] = abr
    api_ref[0:1, :] = abi
    d = 1
    while d < S5_CHUNK:
        pr, pi = apr_ref[0:d, :], api_ref[0:d, :]
        er, ei = apr_ref[d - 1:d, :], api_ref[d - 1:d, :]
        apr_ref[d:2 * d, :] = pr * er - pi * ei
        api_ref[d:2 * d, :] = pr * ei + pi * er
        d *= 2


def _s5_prep(a_re, a_im, log_dt, b_re, b_im, c_re, c_im):
    flat = lambda a: a.reshape(1, SSM_FLAT)
    ldt = jnp.broadcast_to(log_dt[:, None], (SSM_GROUPS, SSM_STATE))
    by_channel = lambda b: b.transpose(0, 2, 1).reshape(SSM_W, SSM_STATE)
    c_rows = lambda cm: cm.reshape(SSM_W, SSM_STATE)
    return pl.pallas_call(
        _s5_prep_kernel,
        out_shape=[jax.ShapeDtypeStruct((SSM_W, 2 * SSM_FLAT), BF16),
                   jax.ShapeDtypeStruct((SSM_W, 2 * SSM_FLAT), BF16),
                   jax.ShapeDtypeStruct((S5_CHUNK, SSM_FLAT), F32),
                   jax.ShapeDtypeStruct((S5_CHUNK, SSM_FLAT), F32)],
        compiler_params=pltpu.CompilerParams(vmem_limit_bytes=VMEM_LIMIT),
        name="s5_prep",
    )(flat(a_re), flat(a_im), flat(ldt), by_channel(b_re), by_channel(b_im), c_rows(c_re), c_rows(c_im))


def _s5_readout(h_re, h_im, u, ccat_ref, d_ref, wglu_ref, bglu_ref):
    hcat = jnp.concatenate([h_re.astype(BF16), h_im.astype(BF16)], axis=1)
    y = _dot_nt(hcat, ccat_ref[...]) + d_ref[...] * u
    z = _gelu(y)
    return z * jax.nn.sigmoid(_dot(z.astype(BF16), wglu_ref[...]) + bglu_ref[...])


def _s5_prompt_kernel(u_ref, bbar_ref, apr_ref, api_ref, a8r_ref, a8i_ref, ccat_ref, d_ref,
                      wglu_ref, bglu_ref,
                      y_ref, hre_ref, him_ref, cr_ref, ci_ref, hr_ref, hi_ref, pr_ref, pi_ref):
    c = pl.program_id(1)
    n = u_ref.shape[0]
    n_groups = n // SUBLANES

    @pl.when(c == 0)
    def _():
        cr_ref[...] = jnp.zeros_like(cr_ref)
        ci_ref[...] = jnp.zeros_like(ci_ref)

    def scan_level(hr, hi, pos, d, power):
        er, ei = apr_ref[power - 1:power, :], api_ref[power - 1:power, :]
        keep = pos >= d
        sr = jnp.where(keep, pltpu.roll(hr, d, axis=0), 0.0)
        si = jnp.where(keep, pltpu.roll(hi, d, axis=0), 0.0)
        return hr + er * sr - ei * si, hi + er * si + ei * sr

    u = u_ref[...]
    bu = _dot(u.astype(BF16), bbar_ref[...])
    hr, hi = bu[:, :SSM_FLAT], bu[:, SSM_FLAT:]
    hr = hr.reshape(n_groups, SUBLANES, SSM_FLAT)
    hi = hi.reshape(n_groups, SUBLANES, SSM_FLAT)
    in_group = lax.broadcasted_iota(jnp.int32, (SUBLANES, SSM_FLAT), 0)
    d = 1
    while d < SUBLANES:
        er = jnp.where(in_group >= d, apr_ref[d - 1:d, :], 0.0)[None]
        ei = jnp.where(in_group >= d, api_ref[d - 1:d, :], 0.0)[None]
        sr, si = pltpu.roll(hr, d, axis=1), pltpu.roll(hi, d, axis=1)
        hr, hi = hr + er * sr - ei * si, hi + er * si + ei * sr
        d *= 2
    hr = hr.reshape(n, SSM_FLAT)
    hi = hi.reshape(n, SSM_FLAT)
    n_lt = SSM_FLAT // LANES
    lt_cols = [slice(l * LANES, (l + 1) * LANES) for l in range(n_lt)]
    for l in range(n_lt):
        hr_ref[l] = hr[:, lt_cols[l]]
        hi_ref[l] = hi[:, lt_cols[l]]

    ends = pl.ds(SUBLANES - 1, n_groups, stride=SUBLANES)
    gr = jnp.concatenate([hr_ref[l, ends, :] for l in range(n_lt)], axis=1)
    gi = jnp.concatenate([hi_ref[l, ends, :] for l in range(n_lt)], axis=1)
    group = lax.broadcasted_iota(jnp.int32, (n_groups, SSM_FLAT), 0)
    d = 1
    while d < n_groups:
        gr, gi = scan_level(gr, gi, group, d, d * SUBLANES)
        d *= 2
    cr, ci = cr_ref[...], ci_ref[...]
    a8r, a8i = a8r_ref[...], a8i_ref[...]
    gr = gr + a8r * cr - a8i * ci
    gi = gi + a8r * ci + a8i * cr
    cr_ref[...] = gr[n_groups - 1:n_groups, :]
    ci_ref[...] = gi[n_groups - 1:n_groups, :]
    hre_ref[...] = gr[n_groups - 1:n_groups, :]
    him_ref[...] = gi[n_groups - 1:n_groups, :]
    first = group == 0
    pr_ref[...] = jnp.where(first, cr, pltpu.roll(gr, 1, axis=0))
    pi_ref[...] = jnp.where(first, ci, pltpu.roll(gi, 1, axis=0))

    a1r, a1i = apr_ref[0:SUBLANES, :], api_ref[0:SUBLANES, :]
    for g in range(n_groups):
        rows = slice(g * SUBLANES, (g + 1) * SUBLANES)
        br = jnp.broadcast_to(pr_ref[g:g + 1, :], (SUBLANES, SSM_FLAT))
        bi = jnp.broadcast_to(pi_ref[g:g + 1, :], (SUBLANES, SSM_FLAT))
        add_r = a1r * br - a1i * bi
        add_i = a1r * bi + a1i * br
        for l in range(n_lt):
            hr_ref[l, rows, :] = hr_ref[l, rows, :] + add_r[:, lt_cols[l]]
            hi_ref[l, rows, :] = hi_ref[l, rows, :] + add_i[:, lt_cols[l]]
    hr = jnp.concatenate([hr_ref[l] for l in range(n_lt)], axis=1)
    hi = jnp.concatenate([hi_ref[l] for l in range(n_lt)], axis=1)
    y_ref[...] = _s5_readout(hr, hi, u, ccat_ref, d_ref, wglu_ref, bglu_ref).astype(y_ref.dtype)


def _s5_prompt(u, bbar, apr, api, ccat, d_row, wglu_bf, bglu_row):
    b, t, _ = u.shape
    n = S5_CHUNK
    full = lambda a: pl.BlockSpec(a.shape, lambda bi, c: (0,) * a.ndim)
    tok = pl.BlockSpec((None, n, SSM_W), lambda bi, c: (bi, c, 0))
    st = pl.BlockSpec((None, 1, SSM_FLAT), lambda bi, c: (bi, 0, 0))
    a8r, a8i = apr[SUBLANES - 1::SUBLANES], api[SUBLANES - 1::SUBLANES]
    lane_major = pltpu.VMEM((SSM_FLAT // LANES, n, LANES), F32)
    y, hre, him = pl.pallas_call(
        _s5_prompt_kernel,
        grid=(b, t // n),
        in_specs=[tok, full(bbar), full(apr), full(api), full(a8r), full(a8i), full(ccat),
                  full(d_row), full(wglu_bf), full(bglu_row)],
        out_specs=[tok, st, st],
        out_shape=[jax.ShapeDtypeStruct((b, t, SSM_W), BF16),
                   jax.ShapeDtypeStruct((b, 1, SSM_FLAT), F32),
                   jax.ShapeDtypeStruct((b, 1, SSM_FLAT), F32)],
        scratch_shapes=[pltpu.VMEM((1, SSM_FLAT), F32), pltpu.VMEM((1, SSM_FLAT), F32),
                        lane_major, lane_major,
                        pltpu.VMEM((n // SUBLANES, SSM_FLAT), F32),
                        pltpu.VMEM((n // SUBLANES, SSM_FLAT), F32)],
        compiler_params=_cparams("parallel", "arbitrary"),
        name="s5_prompt",
    )(u, bbar, apr, api, a8r, a8i, ccat, d_row, wglu_bf, bglu_row)
    return y, hre.reshape(b, SSM_GROUPS, SSM_STATE), him.reshape(b, SSM_GROUPS, SSM_STATE)


def _s5_sample_kernel(u_ref, h0r_ref, h0i_ref, bbar_ref, apr_ref, api_ref, ccat_ref, d_ref,
                      wglu_ref, bglu_ref, y_ref, hre_ref, him_ref, *, n_steps):
    hr, hi = h0r_ref[...], h0i_ref[...]
    ar, ai = apr_ref[0:1, :], api_ref[0:1, :]
    for t in range(n_steps):
        u = u_ref[:, t * SSM_W:(t + 1) * SSM_W]
        bu = _dot(u.astype(BF16), bbar_ref[...])
        hr, hi = (ar * hr - ai * hi + bu[:, :SSM_FLAT],
                  ar * hi + ai * hr + bu[:, SSM_FLAT:])
        y_ref[:, t * SSM_W:(t + 1) * SSM_W] = _s5_readout(
            hr, hi, u, ccat_ref, d_ref, wglu_ref, bglu_ref).astype(y_ref.dtype)
    hre_ref[...] = hr
    him_ref[...] = hi


def _s5_sample(u, h0_re, h0_im, bbar, apr, api, ccat, d_row, wglu_bf, bglu_row):
    db, t, _ = u.shape
    y, hre, him = pl.pallas_call(
        functools.partial(_s5_sample_kernel, n_steps=t),
        out_shape=[jax.ShapeDtypeStruct((db, t * SSM_W), BF16),
                   jax.ShapeDtypeStruct((db, SSM_FLAT), F32),
                   jax.ShapeDtypeStruct((db, SSM_FLAT), F32)],
        compiler_params=pltpu.CompilerParams(vmem_limit_bytes=VMEM_LIMIT),
        name="s5_sample",
    )(u.reshape(db, t * SSM_W), h0_re.reshape(db, SSM_FLAT), h0_im.reshape(db, SSM_FLAT),
      bbar, apr, api, ccat, d_row, wglu_bf, bglu_row)
    return (y.reshape(db, t, SSM_W), hre.reshape(db, SSM_GROUPS, SSM_STATE),
            him.reshape(db, SSM_GROUPS, SSM_STATE))


MERGE_TM = 512


def _merge_kernel(x_ref, att_ref, ssm_ref, g_ref, wa_ref, wb_ref, wo_ref, lg_ref, lb_ref, o_ref):
    ya = _dot(att_ref[...], wa_ref[...])
    yb = _dot(ssm_ref[...], wb_ref[...])
    merged = g_ref[:, :D_MODEL] * ya + g_ref[:, D_MODEL:] * yb
    y = DN_ALPHA * x_ref[...] + _dot(merged.astype(BF16), wo_ref[...])
    o_ref[...] = _layer_norm(y, lg_ref[...], lb_ref[...])


def _merge(x, att, ssm, gates, wa_bf, wb_bf, wo_bf, lg_row, lb_row):
    n = x.shape[0]
    tm = MERGE_TM
    row = lambda w: pl.BlockSpec((tm, w), lambda i: (i, 0))
    full = lambda a: pl.BlockSpec(a.shape, lambda i: (0,) * a.ndim)
    return pl.pallas_call(
        _merge_kernel,
        grid=(n // tm,),
        in_specs=[row(D_MODEL), row(ATT_W), row(SSM_W), row(2 * D_MODEL),
                  full(wa_bf), full(wb_bf), full(wo_bf), full(lg_row), full(lb_row)],
        out_specs=row(D_MODEL),
        out_shape=jax.ShapeDtypeStruct((n, D_MODEL), F32),
        compiler_params=_cparams("parallel"),
        name="merge",
    )(x, att, ssm, gates, wa_bf, wb_bf, wo_bf, lg_row, lb_row)


PEER_TS = 512
PEER_EC = 1024
PEER_NCH = PEER_EXPERTS // PEER_EC
PEER_ROWS_PER_STEP = PEER_EC // PEER_NKEYS
HALF_KEY = PEER_DKEY // 2


def _sorting_network(n):
    pairs = []
    t = (n - 1).bit_length()
    p = 1 << (t - 1)
    while p > 0:
        q, r, d = 1 << (t - 1), 0, p
        while d > 0:
            pairs.extend((i, i + d) for i in range(n - d) if (i & p) == r)
            d, q, r = q - p, q >> 1, p
        p >>= 1
    return pairs


def _top_sorted(s, k):
    assert s.shape[0] == k * SUBLANES and k & (k - 1) == 0
    v = [s[i * SUBLANES:(i + 1) * SUBLANES, :] for i in range(k)]
    for i, j in _sorting_network(k):
        v[i], v[j] = jnp.maximum(v[i], v[j]), jnp.minimum(v[i], v[j])
    shift = SUBLANES // 2
    while shift:
        other = [pltpu.roll(x, shift, axis=0) for x in v]
        v = [jnp.maximum(v[i], other[k - 1 - i]) for i in range(k)]
        stride = k // 2
        while stride:
            for i in range(k):
                if (i // stride) % 2 == 0:
                    lo, hi = v[i], v[i + stride]
                    v[i], v[i + stride] = jnp.maximum(lo, hi), jnp.minimum(lo, hi)
            stride //= 2
        shift //= 2
    return v


def _peer_tables(s1, s2):
    k = PEER_TOPK
    c = s1.shape[1]
    groups3 = (k, SUBLANES, c)
    v1 = _top_sorted(s1, k)
    v2 = _top_sorted(s2, k)
    s2g = s2.reshape(groups3)
    rank2 = jnp.zeros(groups3, F32)
    for b in range(k):
        rank2 = jnp.where(s2g < v2[b][None], float(b + 1), rank2)
    sub = lax.broadcasted_iota(jnp.int32, (SUBLANES, c), 0)

    def pack(rows):
        out = rows[0]
        for r in range(1, SUBLANES):
            out = jnp.where(sub == r, rows[r], out)
        return out

    v2_lo, v2_hi, v1_hi = pack(v2[:SUBLANES]), pack(v2[SUBLANES:]), pack(v1[SUBLANES:])
    cands = [v1[0] + v2_lo, v1[0] + v2_hi, v1[1] + v2_lo]
    for a in range(2, SUBLANES):
        cands.append(jnp.where(sub < k // (a + 1), v1[a] + v2_lo, NEG_INF))
    cands.append(v1_hi + v2[0])
    cands += [jnp.full((SUBLANES, c), NEG_INF, F32)] * (k - len(cands))
    sc = _top_sorted(jnp.concatenate(cands, axis=0), k)
    tau = sc[k - 1]
    z = jnp.zeros_like(tau)
    for r in range(k):
        z = z + jnp.exp(sc[r] - sc[0])
    s1g = s1.reshape(groups3)
    count = jnp.zeros(groups3, F32)
    for a in range(k):
        n_sel = jnp.zeros_like(tau)
        for b in range(k // (a + 1)):
            n_sel = n_sel + jnp.where(v1[a] + v2[b] >= tau, 1.0, 0.0)
        count = jnp.where(s1g == v1[a][None], n_sel[None], count)
    row_weight = jnp.exp(s1g - v1[0][None]) / z[None]
    p2 = jnp.exp(s2g - v2[0][None])
    flat = lambda a: a.reshape(k * SUBLANES, c)
    return flat(rank2), flat(p2), flat(count), flat(row_weight)


def _peer_kernel(x_ref, wq_ref, k1_ref, k2_ref,
                 u0_ref, un_ref, vp_ref, vl_ref, lg_ref, lb_ref, o_ref,
                 xb_ref, s_ref, r2_ref, p2_ref, cnt_ref, cw_ref, acc_ref, st_ref, ht_ref, w_ref):
    c = pl.program_id(1)
    ts = x_ref.shape[0]
    groups = PEER_NKEYS // BF16_ROWS
    cur = c % 2
    nxt = 1 - cur

    @pl.when(c == 0)
    def _prologue():
        xb = x_ref[...].astype(BF16)
        xb_ref[...] = xb
        st_ref[0] = _dot_nt(u0_ref[...], xb).astype(BF16)
        qt = _dot_nt(wq_ref[...], xb).astype(BF16)
        for h in range(PEER_HEADS):
            base = h * PEER_DKEY
            s_ref[0] = _dot(k1_ref[h], qt[base:base + HALF_KEY, :])
            s_ref[1] = _dot(k2_ref[h], qt[base + HALF_KEY:base + PEER_DKEY, :])

            def lane_tile(lt, _):
                cols = pl.ds(pl.multiple_of(lt * LANES, LANES), LANES)
                rank2, p2, count, row_weight = _peer_tables(s_ref[0, :, cols], s_ref[1, :, cols])
                r2_ref[h, :, :, cols] = rank2.astype(BF16).reshape(groups, BF16_ROWS, LANES)
                p2_ref[h, :, :, cols] = p2.astype(BF16).reshape(groups, BF16_ROWS, LANES)
                cnt_ref[h, :, cols] = count
                cw_ref[h, :, cols] = row_weight
                return 0

            lax.fori_loop(0, ts // LANES, lane_tile, 0)
        acc_ref[...] = jnp.zeros_like(acc_ref)
        ht_ref[1] = jnp.zeros((PEER_EC, ts), BF16)

    i1_base = pl.multiple_of(c * PEER_ROWS_PER_STEP, PEER_ROWS_PER_STEP)
    cnts = [cnt_ref[h, pl.ds(i1_base, PEER_ROWS_PER_STEP), :].astype(BF16) for h in range(PEER_HEADS)]
    cws = [cw_ref[h, pl.ds(i1_base, PEER_ROWS_PER_STEP), :].astype(BF16) for h in range(PEER_HEADS)]
    mblk = 2 * PEER_NKEYS
    for j in range(PEER_ROWS_PER_STEP):
        rows = slice(j * PEER_NKEYS, (j + 1) * PEER_NKEYS)
        w = jnp.zeros((groups, BF16_ROWS, ts), BF16)
        for h in range(PEER_HEADS):
            cnt = jnp.broadcast_to(cnts[h][j:j + 1, :], (BF16_ROWS, ts))
            cw = jnp.broadcast_to(cws[h][j:j + 1, :], (BF16_ROWS, ts))
            w = w + jnp.where(r2_ref[h] < cnt[None], cw[None], 0.0) * p2_ref[h]
        w_ref[rows, :] = w.reshape(PEER_NKEYS, ts)
        if j % 2 == 1:
            er = slice((j // 2) * mblk, (j // 2 + 1) * mblk)
            st_ref[nxt, er, :] = _dot_nt(un_ref[er, :], xb_ref[...]).astype(BF16)
    acc_ref[...] += _dot(vp_ref[...], ht_ref[nxt])
    for j in range(PEER_ROWS_PER_STEP):
        rows = slice(j * PEER_NKEYS, (j + 1) * PEER_NKEYS)
        ht_ref[cur, rows, :] = w_ref[rows, :] * _gelu(st_ref[cur, rows, :])

    @pl.when(c == PEER_NCH - 1)
    def _epilogue():
        acc = acc_ref[...] + _dot(vl_ref[...], ht_ref[cur])
        y = DN_ALPHA * x_ref[...] + acc.T
        o_ref[...] = _layer_norm(y, lg_ref[...], lb_ref[...])


def _peer(x1, wq_bf, k1_bf, k2_bf, u_bf, vt_bf, lg_row, lb_row):
    n = x1.shape[0]
    ts = PEER_TS
    last = PEER_NCH - 1
    full = lambda a: pl.BlockSpec(a.shape, lambda i, c: (0,) * a.ndim)
    tok = pl.BlockSpec((ts, D_MODEL), lambda i, c: (i, 0))
    tables = pltpu.VMEM((PEER_HEADS, PEER_NKEYS, ts), F32)
    packed = pltpu.VMEM((PEER_HEADS, PEER_NKEYS // BF16_ROWS, BF16_ROWS, ts), BF16)
    return pl.pallas_call(
        _peer_kernel,
        grid=(n // ts, PEER_NCH),
        in_specs=[tok, full(wq_bf), full(k1_bf), full(k2_bf),
                  pl.BlockSpec((PEER_EC, D_MODEL), lambda i, c: (0, 0)),
                  pl.BlockSpec((PEER_EC, D_MODEL), lambda i, c: (jnp.minimum(c + 1, last), 0)),
                  pl.BlockSpec((D_MODEL, PEER_EC), lambda i, c: (0, jnp.maximum(c - 1, 0))),
                  pl.BlockSpec((D_MODEL, PEER_EC), lambda i, c: (0, last)),
                  full(lg_row), full(lb_row)],
        out_specs=tok,
        out_shape=jax.ShapeDtypeStruct((n, D_MODEL), F32),
        scratch_shapes=[
            pltpu.VMEM((ts, D_MODEL), BF16),
            pltpu.VMEM((2, PEER_NKEYS, ts), F32),
            packed, packed,
            tables, tables,
            pltpu.VMEM((D_MODEL, ts), F32),
            pltpu.VMEM((2, PEER_EC, ts), BF16),
            pltpu.VMEM((2, PEER_EC, ts), BF16),
            pltpu.VMEM((PEER_EC, ts), BF16),
        ],
        compiler_params=_cparams("parallel", "arbitrary"),
        name="peer",
    )(x1, wq_bf, k1_bf, k2_bf, u_bf, u_bf, vt_bf, vt_bf, lg_row, lb_row)


def kernel(x_prompt, x_sample, cache_k, cache_v, state_ssm_re, state_ssm_im, page_table, w_in, b_in, w_a, w_b, w_o, ln1_g, ln1_b, a_re, a_im, log_dt, b_re, b_im, c_re, c_im, d_skip, w_glu, b_glu, ln2_g, ln2_b, w_pq, sub_k1, sub_k2, peer_u, peer_v):
    bn, t, _ = x_prompt.shape
    db, dt_, _ = x_sample.shape
    row = lambda a: a.reshape(1, -1).astype(F32)
    slopes = jnp.exp2(-8.0 * (jnp.arange(N_HEADS, dtype=F32) + 1.0) / N_HEADS)

    w_in_bf = w_in.astype(BF16)
    proj_w = (w_in_bf, row(b_in), w_in[:, ATT_W:3 * ATT_W].T.astype(BF16),
              b_in[ATT_W:3 * ATT_W].reshape(2 * ATT_W, 1).astype(F32))
    wa_bf, wb_bf, wo_bf, wglu_bf = (w.astype(BF16) for w in (w_a, w_b, w_o, w_glu))
    peer_w = (w_pq.T.astype(BF16), sub_k1.astype(BF16), sub_k2.astype(BF16))
    u_bf = peer_u.astype(BF16)
    vt_bf = peer_v.T.astype(BF16)

    bbar, ccat, apr, api = _s5_prep(a_re, a_im, log_dt, b_re, b_im, c_re, c_im)
    s5_w = (bbar, apr, api, ccat, row(d_skip), wglu_bf, row(b_glu))

    xp = x_prompt.reshape(bn * t, D_MODEL)
    q, k_t, v_t, u, gates = _proj(xp, *proj_w, seq_len=t)
    att = _moba_prompt(q.reshape(bn, t, ATT_W), k_t, v_t, slopes)
    ssm, hre_p, him_p = _s5_prompt(u.reshape(bn, t, SSM_W), *s5_w)
    x1 = _merge(xp, att.reshape(bn * t, ATT_W), ssm.reshape(bn * t, SSM_W), gates,
                wa_bf, wb_bf, wo_bf, row(ln1_g), row(ln1_b))
    y_prompt = _peer(x1, *peer_w, u_bf, vt_bf, row(ln2_g), row(ln2_b)).reshape(bn, t, D_MODEL)
    per_token = lambda a: a.reshape(bn, N_HEADS, HEAD_DIM, t).transpose(0, 3, 1, 2)
    k_prompt, v_prompt = per_token(k_t), per_token(v_t)

    xs = x_sample.reshape(db * dt_, D_MODEL)
    q, k, v, u, gates = _proj(xs, *proj_w)
    att = _moba_sample(q.reshape(db, dt_, ATT_W), k.reshape(db, dt_, ATT_W), v.reshape(db, dt_, ATT_W),
                       cache_k, cache_v, page_table, slopes)
    ssm, hre_s, him_s = _s5_sample(u.reshape(db, dt_, SSM_W), state_ssm_re, state_ssm_im, *s5_w)
    x1 = _merge(xs, att.reshape(db * dt_, ATT_W), ssm.reshape(db * dt_, SSM_W), gates,
                wa_bf, wb_bf, wo_bf, row(ln1_g), row(ln1_b))
    y_sample = _peer(x1, *peer_w, u_bf, vt_bf, row(ln2_g), row(ln2_b)).reshape(db, dt_, D_MODEL)
    k_sample = k.reshape(db, dt_, N_HEADS, HEAD_DIM)
    v_sample = v.reshape(db, dt_, N_HEADS, HEAD_DIM)

    return (y_prompt, y_sample, k_prompt, v_prompt, k_sample, v_sample,
            hre_p, him_p, hre_s, him_s)
```

```python
import functools
import math

import jax
import jax.numpy as jnp
from jax import lax
from jax.experimental import pallas as pl
from jax.experimental.pallas import tpu as pltpu

F32 = jnp.float32
BF16 = jnp.bfloat16

D_MODEL = 1024
ATT_W = 512
HEAD_DIM = 64
N_HEADS = 8
MOBA_BLOCK = 256
MOBA_TOPK = 3
SSM_W = 512
SSM_GROUP = 16
SSM_GROUPS = 32
SSM_STATE = 64
SSM_FLAT = SSM_GROUPS * SSM_STATE
PEER_HEADS = 8
PEER_NKEYS = 128
PEER_EXPERTS = PEER_NKEYS * PEER_NKEYS
PEER_DKEY = 256
PEER_TOPK = 16
PROJ_W = 3 * ATT_W + SSM_W + 2 * D_MODEL
DEPTH = 1
DN_ALPHA = (2.0 * DEPTH) ** 0.25
LN_EPS = 1e-5
PAGE_SIZE = 128

LANES = 128
SUBLANES = 8
BF16_ROWS = 16
NEG_INF = float("-inf")
VMEM_LIMIT = 56 * 1024 * 1024

NT_DIMS = (((1,), (1,)), ((), ()))


def _cparams(*sem):
    return pltpu.CompilerParams(dimension_semantics=sem, vmem_limit_bytes=VMEM_LIMIT)


def _dot(a, b):
    return jnp.dot(a, b, preferred_element_type=F32)


def _dot_nt(a, b):
    return lax.dot_general(a, b, NT_DIMS, preferred_element_type=F32)


def _split(x):
    hi = x.astype(BF16)
    lo = (x - hi.astype(F32)).astype(BF16)
    return hi, lo


def _gelu(x):
    c = math.sqrt(2.0 / math.pi)
    return 0.5 * x * (1.0 + jnp.tanh(c * (x + 0.044715 * (x * x * x))))


def _layer_norm(y, g, b):
    mu = jnp.mean(y, axis=-1, keepdims=True)
    yc = y - mu
    var = jnp.mean(yc * yc, axis=-1, keepdims=True)
    return yc * lax.rsqrt(var + LN_EPS) * g + b


PROJ_TM = 512


def _proj_kernel(x_ref, w_ref, b_ref, wkv_t_ref, bkv_col_ref, q_ref, k_ref, v_ref, u_ref, g_ref,
                 *, kv_transposed):
    xb = x_ref[...].astype(BF16)

    def seg(lo, hi):
        return _dot(xb, w_ref[:, lo:hi]) + b_ref[:, lo:hi]

    q_ref[...] = seg(0, ATT_W)
    if kv_transposed:
        kv_t = _dot_nt(wkv_t_ref[...], xb) + bkv_col_ref[...]
        k_ref[...] = kv_t[:ATT_W, :]
        v_ref[...] = kv_t[ATT_W:, :]
    else:
        k_ref[...] = seg(ATT_W, 2 * ATT_W)
        v_ref[...] = seg(2 * ATT_W, 3 * ATT_W)
    u_ref[...] = seg(3 * ATT_W, 3 * ATT_W + SSM_W)
    g_ref[...] = jax.nn.sigmoid(seg(3 * ATT_W + SSM_W, PROJ_W))


def _proj(x, w_bf, b_row, wkv_t, bkv_col, seq_len=None):
    n = x.shape[0]
    tm = PROJ_TM
    row = lambda w: pl.BlockSpec((tm, w), lambda i: (i, 0))
    full = lambda a: pl.BlockSpec(a.shape, lambda i: (0,) * a.ndim)
    if seq_len is None:
        kv_spec, kv_shape = row(ATT_W), jax.ShapeDtypeStruct((n, ATT_W), F32)
    else:
        per_seq = seq_len // tm
        kv_spec = pl.BlockSpec((None, ATT_W, tm), lambda i: (i // per_seq, 0, i % per_seq))
        kv_shape = jax.ShapeDtypeStruct((n // seq_len, ATT_W, seq_len), F32)
    rows = lambda w: jax.ShapeDtypeStruct((n, w), F32)
    return pl.pallas_call(
        functools.partial(_proj_kernel, kv_transposed=seq_len is not None),
        grid=(n // tm,),
        in_specs=[row(D_MODEL), full(w_bf), full(b_row), full(wkv_t), full(bkv_col)],
        out_specs=[row(ATT_W), kv_spec, kv_spec, row(SSM_W), row(2 * D_MODEL)],
        out_shape=[rows(ATT_W), kv_shape, kv_shape, rows(SSM_W), rows(2 * D_MODEL)],
        compiler_params=_cparams("parallel"),
        name="proj",
    )(x, w_bf, b_row, wkv_t, bkv_col)


def _moba_select_bias(gates, cur):
    nb = len(gates)
    g = [jnp.where(n < cur, gates[n], NEG_INF) for n in range(nb)]
    bias = []
    for n in range(nb):
        rank = jnp.zeros(gates[n].shape, jnp.int32)
        for m in range(nb):
            if m != n:
                ahead = (g[m] >= g[n]) if m < n else (g[m] > g[n])
                rank = rank + ahead.astype(jnp.int32)
        sel = (n < cur) & (rank < MOBA_TOPK)
        bias.append(jnp.where(sel, 0.0, NEG_INF))
    return bias


def _moba_prompt_kernel(slopes_ref, q_ref, kt_ref, vt_in_ref, o_ref,
                        kb_ref, vt_ref, ot_ref):
    t = q_ref.shape[0]
    nb = t // MOBA_BLOCK
    blk = MOBA_BLOCK
    hp = pl.program_id(1)
    lane = lax.broadcasted_iota(jnp.int32, (1, LANES), 1)

    k = kt_ref[...].T
    kb_ref[...] = k.astype(BF16)
    vt_ref[...] = vt_in_ref[...].astype(BF16)
    kmean = jnp.mean(k.reshape(nb, blk, LANES), axis=1)
    q_all = q_ref[...]
    q_hi, q_lo = _split(q_all)
    cur = lax.broadcasted_iota(jnp.int32, (1, t), 1) // blk
    d0 = (lax.broadcasted_iota(jnp.int32, (blk, blk), 1)
          - lax.broadcasted_iota(jnp.int32, (blk, blk), 0)).astype(F32)

    heads = []
    for hh in range(2):
        hmask = (lane >= HEAD_DIM * hh) & (lane < HEAD_DIM * (hh + 1))
        slope = slopes_ref[2 * hp + hh]
        km_hi, km_lo = _split(jnp.where(hmask, kmean, 0.0))
        gate = _dot_nt(km_hi, q_hi) + _dot_nt(km_hi, q_lo) + _dot_nt(km_lo, q_hi)
        selb = _moba_select_bias([gate[n:n + 1, :] for n in range(nb)], cur)
        heads.append((hmask, slope, selb, slope * d0, slice(HEAD_DIM * hh, HEAD_DIM * (hh + 1))))

    for i in range(nb):
        qcols = slice(i * blk, (i + 1) * blk)
        n_keys = (i + 1) * blk
        q_blk = q_ref[qcols, :] * (HEAD_DIM ** -0.5)
        s_alls = [_dot_nt(kb_ref[0:n_keys, :], jnp.where(hmask, q_blk, 0.0).astype(BF16))
                  for hmask, *_ in heads]
        tiles = []
        for (hmask, slope, selb, slope_d0, rows), s_all in zip(heads, s_alls):
            per_head = []
            for j in range(i + 1):
                s = s_all[j * blk:(j + 1) * blk, :] - (slope_d0 + slope * float((i - j) * blk))
                if j < i:
                    s = s + selb[j][:, qcols]
                else:
                    s = jnp.where(d0 >= 0.0, s, NEG_INF)
                per_head.append(s)
            tiles.append(per_head)
        maxes = []
        for per_head in tiles:
            m = jnp.max(per_head[0], axis=0, keepdims=True)
            for s in per_head[1:]:
                m = jnp.maximum(m, jnp.max(s, axis=0, keepdims=True))
            maxes.append(m)
        probs = [[jnp.exp(s - m) for s in per_head] for per_head, m in zip(tiles, maxes)]
        sums = []
        for per_head in probs:
            l = jnp.sum(per_head[0], axis=0, keepdims=True)
            for p in per_head[1:]:
                l = l + jnp.sum(p, axis=0, keepdims=True)
            sums.append(l)
        for (hmask, slope, selb, slope_d0, rows), per_head, l in zip(heads, probs, sums):
            p_all = jnp.concatenate([p.astype(BF16) for p in per_head], axis=0)
            res = _dot(vt_ref[:, 0:n_keys], p_all) / l
            ot_ref[rows, qcols] = res[rows, :]

    o_ref[...] = ot_ref[...].T.astype(o_ref.dtype)


def _moba_prompt(q, k_t, v_t, slopes):
    b, t, _ = q.shape
    spec = pl.BlockSpec((None, t, LANES), lambda bi, hp, *_: (bi, 0, hp))
    spec_t = pl.BlockSpec((None, LANES, t), lambda bi, hp, *_: (bi, hp, 0))
    grid_spec = pltpu.PrefetchScalarGridSpec(
        num_scalar_prefetch=1,
        grid=(b, ATT_W // LANES),
        in_specs=[spec, spec_t, spec_t],
        out_specs=spec,
        scratch_shapes=[
            pltpu.VMEM((t, LANES), BF16),
            pltpu.VMEM((LANES, t), BF16),
            pltpu.VMEM((LANES, t), F32),
        ],
    )
    return pl.pallas_call(
        _moba_prompt_kernel,
        grid_spec=grid_spec,
        out_shape=jax.ShapeDtypeStruct((b, t, ATT_W), BF16),
        compiler_params=_cparams("parallel", "parallel"),
        name="moba_prompt",
    )(slopes, q, k_t, v_t)


def _moba_sample_kernel(pt_ref, slopes_ref, q_ref, kn_ref, vn_ref, *rest, n_pages):
    kt_refs = rest[:n_pages]
    vt_refs = rest[n_pages:2 * n_pages]
    o_ref = rest[2 * n_pages]
    nq = q_ref.shape[0]
    n_past = n_pages * PAGE_SIZE
    nb = n_past // MOBA_BLOCK
    pages_per_block = MOBA_BLOCK // PAGE_SIZE
    rows = N_HEADS * nq
    scale = HEAD_DIM ** -0.5
    assert rows % SUBLANES == 0 and nq <= PAGE_SIZE

    q = q_ref[...]
    r_id = lax.broadcasted_iota(jnp.int32, (rows, ATT_W), 0)
    c_id = lax.broadcasted_iota(jnp.int32, (rows, ATT_W), 1)
    q_rep = jnp.concatenate([q] * (rows // nq), axis=0)
    own = c_id // HEAD_DIM == r_id // nq
    qs = jnp.where(own, q_rep * scale, 0.0).astype(BF16)

    r_col = lax.broadcasted_iota(jnp.int32, (rows, 1), 0)
    tq = (r_col % nq).astype(F32)
    slope = jnp.zeros((rows, 1), F32)
    for h in range(N_HEADS):
        slope = jnp.where(r_col // nq == h, slopes_ref[h], slope)
    lane = lax.broadcasted_iota(jnp.int32, (1, PAGE_SIZE), 1).astype(F32)

    raw = [_dot(qs, kt_refs[pg][...].astype(BF16)) for pg in range(n_pages)]

    gates = []
    for n in range(nb):
        total = raw[n * pages_per_block]
        for pg in range(n * pages_per_block + 1, (n + 1) * pages_per_block):
            total = total + raw[pg]
        gates.append(jnp.sum(total, axis=1, keepdims=True) * (1.0 / (MOBA_BLOCK * scale)))
    bias = _moba_select_bias(gates, nb)

    tiles = []
    for pg in range(n_pages):
        dist = (tq + float(n_past - pg * PAGE_SIZE)) - lane
        tiles.append(raw[pg] - slope * dist + bias[pg // pages_per_block])
    pad_rows = lambda a: jnp.concatenate([a, jnp.zeros((PAGE_SIZE - nq, ATT_W), F32)], axis=0)
    dist_new = tq - lane
    s_new = _dot_nt(qs, pad_rows(kn_ref[...]).astype(BF16))
    tiles.append(jnp.where(dist_new >= 0.0, s_new - slope * dist_new, NEG_INF))

    top = tiles[0]
    for s in tiles[1:]:
        top = jnp.maximum(top, s)
    m = jnp.max(top, axis=1, keepdims=True)
    probs = [jnp.exp(s - m) for s in tiles]
    total = probs[0]
    for p in probs[1:]:
        total = total + p
    inv_l = 1.0 / jnp.sum(total, axis=1, keepdims=True)
    out = _dot((probs[n_pages] * inv_l).astype(BF16), pad_rows(vn_ref[...]).astype(BF16))
    for pg in range(n_pages):
        out = out + _dot_nt((probs[pg] * inv_l).astype(BF16), vt_refs[pg][...].astype(BF16))

    res = jnp.zeros((nq, ATT_W), F32)
    c_q = lax.broadcasted_iota(jnp.int32, (nq, ATT_W), 1)
    for h in range(N_HEADS):
        res = jnp.where(c_q // HEAD_DIM == h, out[h * nq:(h + 1) * nq, :], res)
    o_ref[...] = res.astype(o_ref.dtype)


def _moba_sample(q, k_new, v_new, cache_k, cache_v, page_table, slopes):
    db, nq, _ = q.shape
    n_pages = page_table.shape[1]
    new_spec = pl.BlockSpec((None, nq, ATT_W), lambda s, *_: (s, 0, 0))

    def page_spec(pg):
        return pl.BlockSpec((None, ATT_W, PAGE_SIZE), lambda s, pt, sl: (pt[s, pg], 0, 0))

    grid_spec = pltpu.PrefetchScalarGridSpec(
        num_scalar_prefetch=2,
        grid=(db,),
        in_specs=([new_spec] * 3 + [page_spec(pg) for pg in range(n_pages)] * 2),
        out_specs=new_spec,
    )
    by_token = lambda c: jnp.transpose(c, (0, 2, 3, 1)).reshape(c.shape[0], ATT_W, PAGE_SIZE)
    ck, cv = by_token(cache_k), by_token(cache_v)
    return pl.pallas_call(
        functools.partial(_moba_sample_kernel, n_pages=n_pages),
        grid_spec=grid_spec,
        out_shape=jax.ShapeDtypeStruct((db, nq, ATT_W), BF16),
        compiler_params=_cparams("parallel"),
        name="moba_sample",
    )(page_table, slopes, q, k_new, v_new, *([ck] * n_pages), *([cv] * n_pages))


S5_CHUNK = 256


def _s5_prep_kernel(are_ref, aim_ref, ldt_ref, bre_ref, bim_ref, cre_ref, cim_ref,
                    bbar_ref, ccat_ref, apr_ref, api_ref):
    ar, ai = are_ref[...], aim_ref[...]
    dt = jnp.exp(ldt_ref[...])
    mag = jnp.exp(dt * ar)
    ang = dt * ai
    abr, abi = mag * jnp.cos(ang), mag * jnp.sin(ang)
    den = ar * ar + ai * ai
    nr, ni = abr - 1.0, abi
    f_re = (nr * ar + ni * ai) / den
    f_im = (ni * ar - nr * ai) / den

    row_g = lax.broadcasted_iota(jnp.int32, (SSM_W, SSM_FLAT), 0) // SSM_GROUP
    col_g = lax.broadcasted_iota(jnp.int32, (SSM_W, SSM_FLAT), 1) // SSM_STATE
    same_group = row_g == col_g

    def block_diag(ref):
        x = ref[...]
        pair = jnp.concatenate([x] * (LANES // SSM_STATE), axis=1)
        return jnp.where(same_group, jnp.concatenate([pair] * (SSM_FLAT // LANES), axis=1), 0.0)

    bre, bim = block_diag(bre_ref), block_diag(bim_ref)
    bbar_ref[:, :SSM_FLAT] = (f_re * bre - f_im * bim).astype(BF16)
    bbar_ref[:, SSM_FLAT:] = (f_re * bim + f_im * bre).astype(BF16)
    ccat_ref[:, :SSM_FLAT] = block_diag(cre_ref).astype(BF16)
    ccat_ref[:, SSM_FLAT:] = (-block_diag(cim_ref)).astype(BF16)
    apr_ref[0:1, :] = abr
    api_ref[0:1, :] = abi
    d = 1
    while d < S5_CHUNK:
        pr, pi = apr_ref[0:d, :], api_ref[0:d, :]
        er, ei = apr_ref[d - 1:d, :], api_ref[d - 1:d, :]
        apr_ref[d:2 * d, :] = pr * er - pi * ei
        api_ref[d:2 * d, :] = pr * ei + pi * er
        d *= 2


def _s5_prep(a_re, a_im, log_dt, b_re, b_im, c_re, c_im):
    flat = lambda a: a.reshape(1, SSM_FLAT)
    ldt = jnp.broadcast_to(log_dt[:, None], (SSM_GROUPS, SSM_STATE))
    by_channel = lambda b: b.transpose(0, 2, 1).reshape(SSM_W, SSM_STATE)
    c_rows = lambda cm: cm.reshape(SSM_W, SSM_STATE)
    return pl.pallas_call(
        _s5_prep_kernel,
        out_shape=[jax.ShapeDtypeStruct((SSM_W, 2 * SSM_FLAT), BF16),
                   jax.ShapeDtypeStruct((SSM_W, 2 * SSM_FLAT), BF16),
                   jax.ShapeDtypeStruct((S5_CHUNK, SSM_FLAT), F32),
                   jax.ShapeDtypeStruct((S5_CHUNK, SSM_FLAT), F32)],
        compiler_params=pltpu.CompilerParams(vmem_limit_bytes=VMEM_LIMIT),
        name="s5_prep",
    )(flat(a_re), flat(a_im), flat(ldt), by_channel(b_re), by_channel(b_im), c_rows(c_re), c_rows(c_im))


def _s5_readout(h_re, h_im, u, ccat_ref, d_ref, wglu_ref, bglu_ref):
    hcat = jnp.concatenate([h_re.astype(BF16), h_im.astype(BF16)], axis=1)
    y = _dot_nt(hcat, ccat_ref[...]) + d_ref[...] * u
    z = _gelu(y)
    return z * jax.nn.sigmoid(_dot(z.astype(BF16), wglu_ref[...]) + bglu_ref[...])


def _s5_prompt_kernel(u_ref, bbar_ref, apr_ref, api_ref, a8r_ref, a8i_ref, ccat_ref, d_ref,
                      wglu_ref, bglu_ref,
                      y_ref, hre_ref, him_ref, cr_ref, ci_ref, hr_ref, hi_ref, pr_ref, pi_ref):
    c = pl.program_id(1)
    n = u_ref.shape[0]
    n_groups = n // SUBLANES

    @pl.when(c == 0)
    def _():
        cr_ref[...] = jnp.zeros_like(cr_ref)
        ci_ref[...] = jnp.zeros_like(ci_ref)

    def scan_level(hr, hi, pos, d, power):
        er, ei = apr_ref[power - 1:power, :], api_ref[power - 1:power, :]
        keep = pos >= d
        sr = jnp.where(keep, pltpu.roll(hr, d, axis=0), 0.0)
        si = jnp.where(keep, pltpu.roll(hi, d, axis=0), 0.0)
        return hr + er * sr - ei * si, hi + er * si + ei * sr

    u = u_ref[...]
    bu = _dot(u.astype(BF16), bbar_ref[...])
    hr, hi = bu[:, :SSM_FLAT], bu[:, SSM_FLAT:]
    hr = hr.reshape(n_groups, SUBLANES, SSM_FLAT)
    hi = hi.reshape(n_groups, SUBLANES, SSM_FLAT)
    in_group = lax.broadcasted_iota(jnp.int32, (SUBLANES, SSM_FLAT), 0)
    d = 1
    while d < SUBLANES:
        er = jnp.where(in_group >= d, apr_ref[d - 1:d, :], 0.0)[None]
        ei = jnp.where(in_group >= d, api_ref[d - 1:d, :], 0.0)[None]
        sr, si = pltpu.roll(hr, d, axis=1), pltpu.roll(hi, d, axis=1)
        hr, hi = hr + er * sr - ei * si, hi + er * si + ei * sr
        d *= 2
    hr = hr.reshape(n, SSM_FLAT)
    hi = hi.reshape(n, SSM_FLAT)
    n_lt = SSM_FLAT // LANES
    lt_cols = [slice(l * LANES, (l + 1) * LANES) for l in range(n_lt)]
    for l in range(n_lt):
        hr_ref[l] = hr[:, lt_cols[l]]
        hi_ref[l] = hi[:, lt_cols[l]]

    ends = pl.ds(SUBLANES - 1, n_groups, stride=SUBLANES)
    gr = jnp.concatenate([hr_ref[l, ends, :] for l in range(n_lt)], axis=1)
    gi = jnp.concatenate([hi_ref[l, ends, :] for l in range(n_lt)], axis=1)
    group = lax.broadcasted_iota(jnp.int32, (n_groups, SSM_FLAT), 0)
    d = 1
    while d < n_groups:
        gr, gi = scan_level(gr, gi, group, d, d * SUBLANES)
        d *= 2
    cr, ci = cr_ref[...], ci_ref[...]
    a8r, a8i = a8r_ref[...], a8i_ref[...]
    gr = gr + a8r * cr - a8i * ci
    gi = gi + a8r * ci + a8i * cr
    cr_ref[...] = gr[n_groups - 1:n_groups, :]
    ci_ref[...] = gi[n_groups - 1:n_groups, :]
    hre_ref[...] = gr[n_groups - 1:n_groups, :]
    him_ref[...] = gi[n_groups - 1:n_groups, :]
    first = group == 0
    pr_ref[...] = jnp.where(first, cr, pltpu.roll(gr, 1, axis=0))
    pi_ref[...] = jnp.where(first, ci, pltpu.roll(gi, 1, axis=0))

    a1r, a1i = apr_ref[0:SUBLANES, :], api_ref[0:SUBLANES, :]
    for g in range(n_groups):
        rows = slice(g * SUBLANES, (g + 1) * SUBLANES)
        br = jnp.broadcast_to(pr_ref[g:g + 1, :], (SUBLANES, SSM_FLAT))
        bi = jnp.broadcast_to(pi_ref[g:g + 1, :], (SUBLANES, SSM_FLAT))
        add_r = a1r * br - a1i * bi
        add_i = a1r * bi + a1i * br
        for l in range(n_lt):
            hr_ref[l, rows, :] = hr_ref[l, rows, :] + add_r[:, lt_cols[l]]
            hi_ref[l, rows, :] = hi_ref[l, rows, :] + add_i[:, lt_cols[l]]
    hr = jnp.concatenate([hr_ref[l] for l in range(n_lt)], axis=1)
    hi = jnp.concatenate([hi_ref[l] for l in range(n_lt)], axis=1)
    y_ref[...] = _s5_readout(hr, hi, u, ccat_ref, d_ref, wglu_ref, bglu_ref).astype(y_ref.dtype)


def _s5_prompt(u, bbar, apr, api, ccat, d_row, wglu_bf, bglu_row):
    b, t, _ = u.shape
    n = S5_CHUNK
    full = lambda a: pl.BlockSpec(a.shape, lambda bi, c: (0,) * a.ndim)
    tok = pl.BlockSpec((None, n, SSM_W), lambda bi, c: (bi, c, 0))
    st = pl.BlockSpec((None, 1, SSM_FLAT), lambda bi, c: (bi, 0, 0))
    a8r, a8i = apr[SUBLANES - 1::SUBLANES], api[SUBLANES - 1::SUBLANES]
    lane_major = pltpu.VMEM((SSM_FLAT // LANES, n, LANES), F32)
    y, hre, him = pl.pallas_call(
        _s5_prompt_kernel,
        grid=(b, t // n),
        in_specs=[tok, full(bbar), full(apr), full(api), full(a8r), full(a8i), full(ccat),
                  full(d_row), full(wglu_bf), full(bglu_row)],
        out_specs=[tok, st, st],
        out_shape=[jax.ShapeDtypeStruct((b, t, SSM_W), BF16),
                   jax.ShapeDtypeStruct((b, 1, SSM_FLAT), F32),
                   jax.ShapeDtypeStruct((b, 1, SSM_FLAT), F32)],
        scratch_shapes=[pltpu.VMEM((1, SSM_FLAT), F32), pltpu.VMEM((1, SSM_FLAT), F32),
                        lane_major, lane_major,
                        pltpu.VMEM((n // SUBLANES, SSM_FLAT), F32),
                        pltpu.VMEM((n // SUBLANES, SSM_FLAT), F32)],
        compiler_params=_cparams("parallel", "arbitrary"),
        name="s5_prompt",
    )(u, bbar, apr, api, a8r, a8i, ccat, d_row, wglu_bf, bglu_row)
    return y, hre.reshape(b, SSM_GROUPS, SSM_STATE), him.reshape(b, SSM_GROUPS, SSM_STATE)


def _s5_sample_kernel(u_ref, h0r_ref, h0i_ref, bbar_ref, apr_ref, api_ref, ccat_ref, d_ref,
                      wglu_ref, bglu_ref, y_ref, hre_ref, him_ref, *, n_steps):
    hr, hi = h0r_ref[...], h0i_ref[...]
    ar, ai = apr_ref[0:1, :], api_ref[0:1, :]
    for t in range(n_steps):
        u = u_ref[:, t * SSM_W:(t + 1) * SSM_W]
        bu = _dot(u.astype(BF16), bbar_ref[...])
        hr, hi = (ar * hr - ai * hi + bu[:, :SSM_FLAT],
                  ar * hi + ai * hr + bu[:, SSM_FLAT:])
        y_ref[:, t * SSM_W:(t + 1) * SSM_W] = _s5_readout(
            hr, hi, u, ccat_ref, d_ref, wglu_ref, bglu_ref).astype(y_ref.dtype)
    hre_ref[...] = hr
    him_ref[...] = hi


def _s5_sample(u, h0_re, h0_im, bbar, apr, api, ccat, d_row, wglu_bf, bglu_row):
    db, t, _ = u.shape
    y, hre, him = pl.pallas_call(
        functools.partial(_s5_sample_kernel, n_steps=t),
        out_shape=[jax.ShapeDtypeStruct((db, t * SSM_W), BF16),
                   jax.ShapeDtypeStruct((db, SSM_FLAT), F32),
                   jax.ShapeDtypeStruct((db, SSM_FLAT), F32)],
        compiler_params=pltpu.CompilerParams(vmem_limit_bytes=VMEM_LIMIT),
        name="s5_sample",
    )(u.reshape(db, t * SSM_W), h0_re.reshape(db, SSM_FLAT), h0_im.reshape(db, SSM_FLAT),
      bbar, apr, api, ccat, d_row, wglu_bf, bglu_row)
    return (y.reshape(db, t, SSM_W), hre.reshape(db, SSM_GROUPS, SSM_STATE),
            him.reshape(db, SSM_GROUPS, SSM_STATE))


MERGE_TM = 512


def _merge_kernel(x_ref, att_ref, ssm_ref, g_ref, wa_ref, wb_ref, wo_ref, lg_ref, lb_ref, o_ref):
    ya = _dot(att_ref[...], wa_ref[...])
    yb = _dot(ssm_ref[...], wb_ref[...])
    merged = g_ref[:, :D_MODEL] * ya + g_ref[:, D_MODEL:] * yb
    y = DN_ALPHA * x_ref[...] + _dot(merged.astype(BF16), wo_ref[...])
    o_ref[...] = _layer_norm(y, lg_ref[...], lb_ref[...])


def _merge(x, att, ssm, gates, wa_bf, wb_bf, wo_bf, lg_row, lb_row):
    n = x.shape[0]
    tm = MERGE_TM
    row = lambda w: pl.BlockSpec((tm, w), lambda i: (i, 0))
    full = lambda a: pl.BlockSpec(a.shape, lambda i: (0,) * a.ndim)
    return pl.pallas_call(
        _merge_kernel,
        grid=(n // tm,),
        in_specs=[row(D_MODEL), row(ATT_W), row(SSM_W), row(2 * D_MODEL),
                  full(wa_bf), full(wb_bf), full(wo_bf), full(lg_row), full(lb_row)],
        out_specs=row(D_MODEL),
        out_shape=jax.ShapeDtypeStruct((n, D_MODEL), F32),
        compiler_params=_cparams("parallel"),
        name="merge",
    )(x, att, ssm, gates, wa_bf, wb_bf, wo_bf, lg_row, lb_row)


PEER_TS = 512
PEER_EC = 1024
PEER_NCH = PEER_EXPERTS // PEER_EC
PEER_ROWS_PER_STEP = PEER_EC // PEER_NKEYS
HALF_KEY = PEER_DKEY // 2


def _sorting_network(n):
    pairs = []
    t = (n - 1).bit_length()
    p = 1 << (t - 1)
    while p > 0:
        q, r, d = 1 << (t - 1), 0, p
        while d > 0:
            pairs.extend((i, i + d) for i in range(n - d) if (i & p) == r)
            d, q, r = q - p, q >> 1, p
        p >>= 1
    return pairs


def _top_sorted(s, k):
    assert s.shape[0] == k * SUBLANES and k & (k - 1) == 0
    v = [s[i * SUBLANES:(i + 1) * SUBLANES, :] for i in range(k)]
    for i, j in _sorting_network(k):
        v[i], v[j] = jnp.maximum(v[i], v[j]), jnp.minimum(v[i], v[j])
    shift = SUBLANES // 2
    while shift:
        other = [pltpu.roll(x, shift, axis=0) for x in v]
        v = [jnp.maximum(v[i], other[k - 1 - i]) for i in range(k)]
        stride = k // 2
        while stride:
            for i in range(k):
                if (i // stride) % 2 == 0:
                    lo, hi = v[i], v[i + stride]
                    v[i], v[i + stride] = jnp.maximum(lo, hi), jnp.minimum(lo, hi)
            stride //= 2
        shift //= 2
    return v


def _peer_tables(s1, s2):
    k = PEER_TOPK
    c = s1.shape[1]
    groups3 = (k, SUBLANES, c)
    v1 = _top_sorted(s1, k)
    v2 = _top_sorted(s2, k)
    s2g = s2.reshape(groups3)
    rank2 = jnp.zeros(groups3, F32)
    for b in range(k):
        rank2 = jnp.where(s2g < v2[b][None], float(b + 1), rank2)
    sub = lax.broadcasted_iota(jnp.int32, (SUBLANES, c), 0)

    def pack(rows):
        out = rows[0]
        for r in range(1, SUBLANES):
            out = jnp.where(sub == r, rows[r], out)
        return out

    v2_lo, v2_hi, v1_hi = pack(v2[:SUBLANES]), pack(v2[SUBLANES:]), pack(v1[SUBLANES:])
    cands = [v1[0] + v2_lo, v1[0] + v2_hi, v1[1] + v2_lo]
    for a in range(2, SUBLANES):
        cands.append(jnp.where(sub < k // (a + 1), v1[a] + v2_lo, NEG_INF))
    cands.append(v1_hi + v2[0])
    cands += [jnp.full((SUBLANES, c), NEG_INF, F32)] * (k - len(cands))
    sc = _top_sorted(jnp.concatenate(cands, axis=0), k)
    tau = sc[k - 1]
    z = jnp.zeros_like(tau)
    for r in range(k):
        z = z + jnp.exp(sc[r] - sc[0])
    s1g = s1.reshape(groups3)
    count = jnp.zeros(groups3, F32)
    for a in range(k):
        n_sel = jnp.zeros_like(tau)
        for b in range(k // (a + 1)):
            n_sel = n_sel + jnp.where(v1[a] + v2[b] >= tau, 1.0, 0.0)
        count = jnp.where(s1g == v1[a][None], n_sel[None], count)
    row_weight = jnp.exp(s1g - v1[0][None]) / z[None]
    p2 = jnp.exp(s2g - v2[0][None])
    flat = lambda a: a.reshape(k * SUBLANES, c)
    return flat(rank2), flat(p2), flat(count), flat(row_weight)


def _peer_kernel(x_ref, wq_ref, k1_ref, k2_ref,
                 u0_ref, un_ref, vp_ref, vl_ref, lg_ref, lb_ref, o_ref,
                 xb_ref, s_ref, r2_ref, p2_ref, cnt_ref, cw_ref, acc_ref, st_ref, ht_ref, w_ref):
    c = pl.program_id(1)
    ts = x_ref.shape[0]
    groups = PEER_NKEYS // BF16_ROWS
    cur = c % 2
    nxt = 1 - cur

    @pl.when(c == 0)
    def _prologue():
        xb = x_ref[...].astype(BF16)
        xb_ref[...] = xb
        st_ref[0] = _dot_nt(u0_ref[...], xb).astype(BF16)
        qt = _dot_nt(wq_ref[...], xb).astype(BF16)
        for h in range(PEER_HEADS):
            base = h * PEER_DKEY
            s_ref[0] = _dot(k1_ref[h], qt[base:base + HALF_KEY, :])
            s_ref[1] = _dot(k2_ref[h], qt[base + HALF_KEY:base + PEER_DKEY, :])

            def lane_tile(lt, _):
                cols = pl.ds(pl.multiple_of(lt * LANES, LANES), LANES)
                rank2, p2, count, row_weight = _peer_tables(s_ref[0, :, cols], s_ref[1, :, cols])
                r2_ref[h, :, :, cols] = rank2.astype(BF16).reshape(groups, BF16_ROWS, LANES)
                p2_ref[h, :, :, cols] = p2.astype(BF16).reshape(groups, BF16_ROWS, LANES)
                cnt_ref[h, :, cols] = count
                cw_ref[h, :, cols] = row_weight
                return 0

            lax.fori_loop(0, ts // LANES, lane_tile, 0)
        acc_ref[...] = jnp.zeros_like(acc_ref)
        ht_ref[1] = jnp.zeros((PEER_EC, ts), BF16)

    i1_base = pl.multiple_of(c * PEER_ROWS_PER_STEP, PEER_ROWS_PER_STEP)
    cnts = [cnt_ref[h, pl.ds(i1_base, PEER_ROWS_PER_STEP), :] for h in range(PEER_HEADS)]
    cws = [cw_ref[h, pl.ds(i1_base, PEER_ROWS_PER_STEP), :] for h in range(PEER_HEADS)]
    mblk = 2 * PEER_NKEYS
    for j in range(PEER_ROWS_PER_STEP):
        rows = slice(j * PEER_NKEYS, (j + 1) * PEER_NKEYS)
        w = jnp.zeros((groups, BF16_ROWS, ts), BF16)
        for h in range(PEER_HEADS):
            cnt = jnp.broadcast_to(cnts[h][j:j + 1, :], (BF16_ROWS, ts)).astype(BF16)
            cw = jnp.broadcast_to(cws[h][j:j + 1, :], (BF16_ROWS, ts)).astype(BF16)
            w = w + jnp.where(r2_ref[h] < cnt[None], cw[None], 0.0) * p2_ref[h]
        w_ref[rows, :] = w.reshape(PEER_NKEYS, ts)
        if j % 2 == 1:
            er = slice((j // 2) * mblk, (j // 2 + 1) * mblk)
            st_ref[nxt, er, :] = _dot_nt(un_ref[er, :], xb_ref[...]).astype(BF16)
    acc_ref[...] += _dot(vp_ref[...], ht_ref[nxt])
    for j in range(PEER_ROWS_PER_STEP):
        rows = slice(j * PEER_NKEYS, (j + 1) * PEER_NKEYS)
        ht_ref[cur, rows, :] = w_ref[rows, :] * _gelu(st_ref[cur, rows, :])

    @pl.when(c == PEER_NCH - 1)
    def _epilogue():
        acc = acc_ref[...] + _dot(vl_ref[...], ht_ref[cur])
        y = DN_ALPHA * x_ref[...] + acc.T
        o_ref[...] = _layer_norm(y, lg_ref[...], lb_ref[...])


def _peer(x1, wq_bf, k1_bf, k2_bf, u_bf, vt_bf, lg_row, lb_row):
    n = x1.shape[0]
    ts = PEER_TS
    last = PEER_NCH - 1
    full = lambda a: pl.BlockSpec(a.shape, lambda i, c: (0,) * a.ndim)
    tok = pl.BlockSpec((ts, D_MODEL), lambda i, c: (i, 0))
    tables = pltpu.VMEM((PEER_HEADS, PEER_NKEYS, ts), F32)
    packed = pltpu.VMEM((PEER_HEADS, PEER_NKEYS // BF16_ROWS, BF16_ROWS, ts), BF16)
    return pl.pallas_call(
        _peer_kernel,
        grid=(n // ts, PEER_NCH),
        in_specs=[tok, full(wq_bf), full(k1_bf), full(k2_bf),
                  pl.BlockSpec((PEER_EC, D_MODEL), lambda i, c: (0, 0)),
                  pl.BlockSpec((PEER_EC, D_MODEL), lambda i, c: (jnp.minimum(c + 1, last), 0)),
                  pl.BlockSpec((D_MODEL, PEER_EC), lambda i, c: (0, jnp.maximum(c - 1, 0))),
                  pl.BlockSpec((D_MODEL, PEER_EC), lambda i, c: (0, last)),
                  full(lg_row), full(lb_row)],
        out_specs=tok,
        out_shape=jax.ShapeDtypeStruct((n, D_MODEL), F32),
        scratch_shapes=[
            pltpu.VMEM((ts, D_MODEL), BF16),
            pltpu.VMEM((2, PEER_NKEYS, ts), F32),
            packed, packed,
            tables, tables,
            pltpu.VMEM((D_MODEL, ts), F32),
            pltpu.VMEM((2, PEER_EC, ts), BF16),
            pltpu.VMEM((2, PEER_EC, ts), BF16),
            pltpu.VMEM((PEER_EC, ts), BF16),
        ],
        compiler_params=_cparams("parallel", "arbitrary"),
        name="peer",
    )(x1, wq_bf, k1_bf, k2_bf, u_bf, u_bf, vt_bf, vt_bf, lg_row, lb_row)


def kernel(x_prompt, x_sample, cache_k, cache_v, state_ssm_re, state_ssm_im, page_table, w_in, b_in, w_a, w_b, w_o, ln1_g, ln1_b, a_re, a_im, log_dt, b_re, b_im, c_re, c_im, d_skip, w_glu, b_glu, ln2_g, ln2_b, w_pq, sub_k1, sub_k2, peer_u, peer_v):
    bn, t, _ = x_prompt.shape
    db, dt_, _ = x_sample.shape
    row = lambda a: a.reshape(1, -1).astype(F32)
    slopes = jnp.exp2(-8.0 * (jnp.arange(N_HEADS, dtype=F32) + 1.0) / N_HEADS)

    w_in_bf = w_in.astype(BF16)
    proj_w = (w_in_bf, row(b_in), w_in[:, ATT_W:3 * ATT_W].T.astype(BF16),
              b_in[ATT_W:3 * ATT_W].reshape(2 * ATT_W, 1).astype(F32))
    wa_bf, wb_bf, wo_bf, wglu_bf = (w.astype(BF16) for w in (w_a, w_b, w_o, w_glu))
    peer_w = (w_pq.T.astype(BF16), sub_k1.astype(BF16), sub_k2.astype(BF16))
    u_bf = peer_u.astype(BF16)
    vt_bf = peer_v.T.astype(BF16)

    bbar, ccat, apr, api = _s5_prep(a_re, a_im, log_dt, b_re, b_im, c_re, c_im)
    s5_w = (bbar, apr, api, ccat, row(d_skip), wglu_bf, row(b_glu))

    xp = x_prompt.reshape(bn * t, D_MODEL)
    q, k_t, v_t, u, gates = _proj(xp, *proj_w, seq_len=t)
    att = _moba_prompt(q.reshape(bn, t, ATT_W), k_t, v_t, slopes)
    ssm, hre_p, him_p = _s5_prompt(u.reshape(bn, t, SSM_W), *s5_w)
    x1 = _merge(xp, att.reshape(bn * t, ATT_W), ssm.reshape(bn * t, SSM_W), gates,
                wa_bf, wb_bf, wo_bf, row(ln1_g), row(ln1_b))
    y_prompt = _peer(x1, *peer_w, u_bf, vt_bf, row(ln2_g), row(ln2_b)).reshape(bn, t, D_MODEL)
    per_token = lambda a: a.reshape(bn, N_HEADS, HEAD_DIM, t).transpose(0, 3, 1, 2)
    k_prompt, v_prompt = per_token(k_t), per_token(v_t)

    xs = x_sample.reshape(db * dt_, D_MODEL)
    q, k, v, u, gates = _proj(xs, *proj_w)
    att = _moba_sample(q.reshape(db, dt_, ATT_W), k.reshape(db, dt_, ATT_W), v.reshape(db, dt_, ATT_W),
                       cache_k, cache_v, page_table, slopes)
    ssm, hre_s, him_s = _s5_sample(u.reshape(db, dt_, SSM_W), state_ssm_re, state_ssm_im, *s5_w)
    x1 = _merge(xs, att.reshape(db * dt_, ATT_W), ssm.reshape(db * dt_, SSM_W), gates,
                wa_bf, wb_bf, wo_bf, row(ln1_g), row(ln1_b))
    y_sample = _peer(x1, *peer_w, u_bf, vt_bf, row(ln2_g), row(ln2_b)).reshape(db, dt_, D_MODEL)
    k_sample = k.reshape(db, dt_, N_HEADS, HEAD_DIM)
    v_sample = v.reshape(db, dt_, N_HEADS, HEAD_DIM)

    return (y_prompt, y_sample, k_prompt, v_prompt, k_sample, v_sample,
            hre_p, him_p, hre_s, him_s)
```

```python
import functools
import math

import jax
import jax.numpy as jnp
from jax import lax
from jax.experimental import pallas as pl
from jax.experimental.pallas import tpu as pltpu

F32 = jnp.float32
BF16 = jnp.bfloat16

D_MODEL = 1024
ATT_W = 512
HEAD_DIM = 64
N_HEADS = 8
MOBA_BLOCK = 256
MOBA_TOPK = 3
SSM_W = 512
SSM_GROUP = 16
SSM_GROUPS = 32
SSM_STATE = 64
SSM_FLAT = SSM_GROUPS * SSM_STATE
PEER_HEADS = 8
PEER_NKEYS = 128
PEER_EXPERTS = PEER_NKEYS * PEER_NKEYS
PEER_DKEY = 256
PEER_TOPK = 16
PROJ_W = 3 * ATT_W + SSM_W + 2 * D_MODEL
DEPTH = 1
DN_ALPHA = (2.0 * DEPTH) ** 0.25
LN_EPS = 1e-5
PAGE_SIZE = 128

LANES = 128
SUBLANES = 8
BF16_ROWS = 16
NEG_INF = float("-inf")
VMEM_LIMIT = 56 * 1024 * 1024

NT_DIMS = (((1,), (1,)), ((), ()))


def _cparams(*sem):
    return pltpu.CompilerParams(dimension_semantics=sem, vmem_limit_bytes=VMEM_LIMIT)


def _dot(a, b):
    return jnp.dot(a, b, preferred_element_type=F32)


def _dot_nt(a, b):
    return lax.dot_general(a, b, NT_DIMS, preferred_element_type=F32)


def _split(x):
    hi = x.astype(BF16)
    lo = (x - hi.astype(F32)).astype(BF16)
    return hi, lo


def _gelu(x):
    c = math.sqrt(2.0 / math.pi)
    return 0.5 * x * (1.0 + jnp.tanh(c * (x + 0.044715 * (x * x * x))))


def _layer_norm(y, g, b):
    mu = jnp.mean(y, axis=-1, keepdims=True)
    yc = y - mu
    var = jnp.mean(yc * yc, axis=-1, keepdims=True)
    return yc * lax.rsqrt(var + LN_EPS) * g + b


PROJ_TM = 512


def _proj_kernel(x_ref, w_ref, b_ref, wkv_t_ref, bkv_col_ref, q_ref, k_ref, v_ref, u_ref, g_ref,
                 *, kv_transposed):
    xb = x_ref[...].astype(BF16)

    def seg(lo, hi):
        return _dot(xb, w_ref[:, lo:hi]) + b_ref[:, lo:hi]

    q_ref[...] = seg(0, ATT_W)
    if kv_transposed:
        kv_t = _dot_nt(wkv_t_ref[...], xb) + bkv_col_ref[...]
        k_ref[...] = kv_t[:ATT_W, :]
        v_ref[...] = kv_t[ATT_W:, :]
    else:
        k_ref[...] = seg(ATT_W, 2 * ATT_W)
        v_ref[...] = seg(2 * ATT_W, 3 * ATT_W)
    u_ref[...] = seg(3 * ATT_W, 3 * ATT_W + SSM_W)
    g_ref[...] = jax.nn.sigmoid(seg(3 * ATT_W + SSM_W, PROJ_W))


def _proj(x, w_bf, b_row, wkv_t, bkv_col, seq_len=None):
    n = x.shape[0]
    tm = PROJ_TM
    row = lambda w: pl.BlockSpec((tm, w), lambda i: (i, 0))
    full = lambda a: pl.BlockSpec(a.shape, lambda i: (0,) * a.ndim)
    if seq_len is None:
        kv_spec, kv_shape = row(ATT_W), jax.ShapeDtypeStruct((n, ATT_W), F32)
    else:
        per_seq = seq_len // tm
        kv_spec = pl.BlockSpec((None, ATT_W, tm), lambda i: (i // per_seq, 0, i % per_seq))
        kv_shape = jax.ShapeDtypeStruct((n // seq_len, ATT_W, seq_len), F32)
    rows = lambda w: jax.ShapeDtypeStruct((n, w), F32)
    return pl.pallas_call(
        functools.partial(_proj_kernel, kv_transposed=seq_len is not None),
        grid=(n // tm,),
        in_specs=[row(D_MODEL), full(w_bf), full(b_row), full(wkv_t), full(bkv_col)],
        out_specs=[row(ATT_W), kv_spec, kv_spec, row(SSM_W), row(2 * D_MODEL)],
        out_shape=[rows(ATT_W), kv_shape, kv_shape, rows(SSM_W), rows(2 * D_MODEL)],
        compiler_params=_cparams("parallel"),
        name="proj",
    )(x, w_bf, b_row, wkv_t, bkv_col)


def _moba_select_bias(gates, cur):
    nb = len(gates)
    g = [jnp.where(n < cur, gates[n], NEG_INF) for n in range(nb)]
    bias = []
    for n in range(nb):
        rank = jnp.zeros(gates[n].shape, jnp.int32)
        for m in range(nb):
            if m != n:
                ahead = (g[m] >= g[n]) if m < n else (g[m] > g[n])
                rank = rank + ahead.astype(jnp.int32)
        sel = (n < cur) & (rank < MOBA_TOPK)
        bias.append(jnp.where(sel, 0.0, NEG_INF))
    return bias


def _moba_prompt_kernel(slopes_ref, q_ref, kt_ref, vt_in_ref, o_ref,
                        kb_ref, vt_ref, ot_ref):
    t = q_ref.shape[0]
    nb = t // MOBA_BLOCK
    blk = MOBA_BLOCK
    hp = pl.program_id(1)
    lane = lax.broadcasted_iota(jnp.int32, (1, LANES), 1)

    k = kt_ref[...].T
    kb_ref[...] = k.astype(BF16)
    vt_ref[...] = vt_in_ref[...].astype(BF16)
    kmean = jnp.mean(k.reshape(nb, blk, LANES), axis=1)
    q_all = q_ref[...]
    q_hi, q_lo = _split(q_all)
    cur = lax.broadcasted_iota(jnp.int32, (1, t), 1) // blk
    d0 = (lax.broadcasted_iota(jnp.int32, (blk, blk), 1)
          - lax.broadcasted_iota(jnp.int32, (blk, blk), 0)).astype(F32)

    heads = []
    for hh in range(2):
        hmask = (lane >= HEAD_DIM * hh) & (lane < HEAD_DIM * (hh + 1))
        slope = slopes_ref[2 * hp + hh]
        km_hi, km_lo = _split(jnp.where(hmask, kmean, 0.0))
        gate = _dot_nt(km_hi, q_hi) + _dot_nt(km_hi, q_lo) + _dot_nt(km_lo, q_hi)
        selb = _moba_select_bias([gate[n:n + 1, :] for n in range(nb)], cur)
        heads.append((hmask, slope, selb, slope * d0, slice(HEAD_DIM * hh, HEAD_DIM * (hh + 1))))

    for i in range(nb):
        qcols = slice(i * blk, (i + 1) * blk)
        n_keys = (i + 1) * blk
        q_blk = q_ref[qcols, :] * (HEAD_DIM ** -0.5)
        s_alls = [_dot_nt(kb_ref[0:n_keys, :], jnp.where(hmask, q_blk, 0.0).astype(BF16))
                  for hmask, *_ in heads]
        tiles = []
        for (hmask, slope, selb, slope_d0, rows), s_all in zip(heads, s_alls):
            per_head = []
            for j in range(i + 1):
                s = s_all[j * blk:(j + 1) * blk, :] - (slope_d0 + slope * float((i - j) * blk))
                if j < i:
                    s = s + selb[j][:, qcols]
                else:
                    s = jnp.where(d0 >= 0.0, s, NEG_INF)
                per_head.append(s)
            tiles.append(per_head)
        maxes = []
        for per_head in tiles:
            m = jnp.max(per_head[0], axis=0, keepdims=True)
            for s in per_head[1:]:
                m = jnp.maximum(m, jnp.max(s, axis=0, keepdims=True))
            maxes.append(m)
        probs = [[jnp.exp(s - m) for s in per_head] for per_head, m in zip(tiles, maxes)]
        sums = []
        for per_head in probs:
            l = jnp.sum(per_head[0], axis=0, keepdims=True)
            for p in per_head[1:]:
                l = l + jnp.sum(p, axis=0, keepdims=True)
            sums.append(l)
        for (hmask, slope, selb, slope_d0, rows), per_head, l in zip(heads, probs, sums):
            p_all = jnp.concatenate([p.astype(BF16) for p in per_head], axis=0)
            res = _dot(vt_ref[:, 0:n_keys], p_all) / l
            ot_ref[rows, qcols] = res[rows, :]

    o_ref[...] = ot_ref[...].T.astype(o_ref.dtype)


def _moba_prompt(q, k_t, v_t, slopes):
    b, t, _ = q.shape
    spec = pl.BlockSpec((None, t, LANES), lambda bi, hp, *_: (bi, 0, hp))
    spec_t = pl.BlockSpec((None, LANES, t), lambda bi, hp, *_: (bi, hp, 0))
    grid_spec = pltpu.PrefetchScalarGridSpec(
        num_scalar_prefetch=1,
        grid=(b, ATT_W // LANES),
        in_specs=[spec, spec_t, spec_t],
        out_specs=spec,
        scratch_shapes=[
            pltpu.VMEM((t, LANES), BF16),
            pltpu.VMEM((LANES, t), BF16),
            pltpu.VMEM((LANES, t), F32),
        ],
    )
    return pl.pallas_call(
        _moba_prompt_kernel,
        grid_spec=grid_spec,
        out_shape=jax.ShapeDtypeStruct((b, t, ATT_W), BF16),
        compiler_params=_cparams("parallel", "parallel"),
        name="moba_prompt",
    )(slopes, q, k_t, v_t)


MOBA_SAMPLE_SEQS = 2


def _moba_sample_kernel(pt_ref, slopes_ref, q_ref, kn_ref, vn_ref, *rest, n_pages):
    n_seq = q_ref.shape[0]
    o_ref = rest[2 * n_seq * n_pages]
    for sq in range(n_seq):
        kt_refs = rest[sq * n_pages:(sq + 1) * n_pages]
        vt_refs = rest[(n_seq + sq) * n_pages:(n_seq + sq + 1) * n_pages]
        _moba_sample_sequence(slopes_ref, q_ref.at[sq], kn_ref.at[sq], vn_ref.at[sq],
                              kt_refs, vt_refs, o_ref.at[sq], n_pages)


def _moba_sample_sequence(slopes_ref, q_ref, kn_ref, vn_ref, kt_refs, vt_refs, o_ref, n_pages):
    nq = q_ref.shape[0]
    n_past = n_pages * PAGE_SIZE
    nb = n_past // MOBA_BLOCK
    pages_per_block = MOBA_BLOCK // PAGE_SIZE
    rows = N_HEADS * nq
    scale = HEAD_DIM ** -0.5
    assert rows % SUBLANES == 0 and nq <= PAGE_SIZE

    q = q_ref[...]
    r_id = lax.broadcasted_iota(jnp.int32, (rows, ATT_W), 0)
    c_id = lax.broadcasted_iota(jnp.int32, (rows, ATT_W), 1)
    q_rep = jnp.concatenate([q] * (rows // nq), axis=0)
    own = c_id // HEAD_DIM == r_id // nq
    qs = jnp.where(own, q_rep * scale, 0.0).astype(BF16)

    r_col = lax.broadcasted_iota(jnp.int32, (rows, 1), 0)
    tq = (r_col % nq).astype(F32)
    slope = jnp.zeros((rows, 1), F32)
    for h in range(N_HEADS):
        slope = jnp.where(r_col // nq == h, slopes_ref[h], slope)
    lane = lax.broadcasted_iota(jnp.int32, (1, PAGE_SIZE), 1).astype(F32)

    raw = [_dot(qs, kt_refs[pg][...].astype(BF16)) for pg in range(n_pages)]

    gates = []
    for n in range(nb):
        total = raw[n * pages_per_block]
        for pg in range(n * pages_per_block + 1, (n + 1) * pages_per_block):
            total = total + raw[pg]
        gates.append(jnp.sum(total, axis=1, keepdims=True) * (1.0 / (MOBA_BLOCK * scale)))
    bias = _moba_select_bias(gates, nb)

    tiles = []
    for pg in range(n_pages):
        dist = (tq + float(n_past - pg * PAGE_SIZE)) - lane
        tiles.append(raw[pg] - slope * dist + bias[pg // pages_per_block])
    pad_rows = lambda a: jnp.concatenate([a, jnp.zeros((PAGE_SIZE - nq, ATT_W), F32)], axis=0)
    dist_new = tq - lane
    s_new = _dot_nt(qs, pad_rows(kn_ref[...]).astype(BF16))
    tiles.append(jnp.where(dist_new >= 0.0, s_new - slope * dist_new, NEG_INF))

    top = tiles[0]
    for s in tiles[1:]:
        top = jnp.maximum(top, s)
    m = jnp.max(top, axis=1, keepdims=True)
    probs = [jnp.exp(s - m) for s in tiles]
    total = probs[0]
    for p in probs[1:]:
        total = total + p
    inv_l = 1.0 / jnp.sum(total, axis=1, keepdims=True)
    out = _dot((probs[n_pages] * inv_l).astype(BF16), pad_rows(vn_ref[...]).astype(BF16))
    for pg in range(n_pages):
        out = out + _dot_nt((probs[pg] * inv_l).astype(BF16), vt_refs[pg][...].astype(BF16))

    res = jnp.zeros((nq, ATT_W), F32)
    c_q = lax.broadcasted_iota(jnp.int32, (nq, ATT_W), 1)
    for h in range(N_HEADS):
        res = jnp.where(c_q // HEAD_DIM == h, out[h * nq:(h + 1) * nq, :], res)
    o_ref[...] = res.astype(o_ref.dtype)


def _moba_sample(q, k_new, v_new, cache_k, cache_v, page_table, slopes):
    db, nq, _ = q.shape
    n_pages = page_table.shape[1]
    sps = MOBA_SAMPLE_SEQS
    new_spec = pl.BlockSpec((sps, nq, ATT_W), lambda s, *_: (s, 0, 0))

    def page_spec(sq, pg):
        return pl.BlockSpec((None, ATT_W, PAGE_SIZE), lambda s, pt, sl: (pt[sps * s + sq, pg], 0, 0))

    pages = [page_spec(sq, pg) for sq in range(sps) for pg in range(n_pages)]
    grid_spec = pltpu.PrefetchScalarGridSpec(
        num_scalar_prefetch=2,
        grid=(db // sps,),
        in_specs=[new_spec] * 3 + pages * 2,
        out_specs=new_spec,
    )
    by_token = lambda c: jnp.transpose(c, (0, 2, 3, 1)).reshape(c.shape[0], ATT_W, PAGE_SIZE)
    ck, cv = by_token(cache_k), by_token(cache_v)
    return pl.pallas_call(
        functools.partial(_moba_sample_kernel, n_pages=n_pages),
        grid_spec=grid_spec,
        out_shape=jax.ShapeDtypeStruct((db, nq, ATT_W), BF16),
        compiler_params=_cparams("parallel"),
        name="moba_sample",
    )(page_table, slopes, q, k_new, v_new, *([ck] * (sps * n_pages)), *([cv] * (sps * n_pages)))


S5_CHUNK = 256


def _s5_prep_kernel(are_ref, aim_ref, ldt_ref, bre_ref, bim_ref, cre_ref, cim_ref,
                    bbar_ref, ccat_ref, apr_ref, api_ref):
    ar, ai = are_ref[...], aim_ref[...]
    dt = jnp.exp(ldt_ref[...])
    mag = jnp.exp(dt * ar)
    ang = dt * ai
    abr, abi = mag * jnp.cos(ang), mag * jnp.sin(ang)
    den = ar * ar + ai * ai
    nr, ni = abr - 1.0, abi
    f_re = (nr * ar + ni * ai) / den
    f_im = (ni * ar - nr * ai) / den

    row_g = lax.broadcasted_iota(jnp.int32, (SSM_W, SSM_FLAT), 0) // SSM_GROUP
    col_g = lax.broadcasted_iota(jnp.int32, (SSM_W, SSM_FLAT), 1) // SSM_STATE
    same_group = row_g == col_g

    def block_diag(ref):
        x = ref[...]
        pair = jnp.concatenate([x] * (LANES // SSM_STATE), axis=1)
        return jnp.where(same_group, jnp.concatenate([pair] * (SSM_FLAT // LANES), axis=1), 0.0)

    bre, bim = block_diag(bre_ref), block_diag(bim_ref)
    bbar_ref[:, :SSM_FLAT] = (f_re * bre - f_im * bim).astype(BF16)
    bbar_ref[:, SSM_FLAT:] = (f_re * bim + f_im * bre).astype(BF16)
    ccat_ref[:, :SSM_FLAT] = block_diag(cre_ref).astype(BF16)
    ccat_ref[:, SSM_FLAT:] = (-block_diag(cim_ref)).astype(BF16)
    apr_ref[0:1, :] = abr
    api_ref[0:1, :] = abi
    d = 1
    while d < S5_CHUNK:
        pr, pi = apr_ref[0:d, :], api_ref[0:d, :]
        er, ei = apr_ref[d - 1:d, :], api_ref[d - 1:d, :]
        apr_ref[d:2 * d, :] = pr * er - pi * ei
        api_ref[d:2 * d, :] = pr * ei + pi * er
        d *= 2


def _s5_prep(a_re, a_im, log_dt, b_re, b_im, c_re, c_im):
    flat = lambda a: a.reshape(1, SSM_FLAT)
    ldt = jnp.broadcast_to(log_dt[:, None], (SSM_GROUPS, SSM_STATE))
    by_channel = lambda b: b.transpose(0, 2, 1).reshape(SSM_W, SSM_STATE)
    c_rows = lambda cm: cm.reshape(SSM_W, SSM_STATE)
    return pl.pallas_call(
        _s5_prep_kernel,
        out_shape=[jax.ShapeDtypeStruct((SSM_W, 2 * SSM_FLAT), BF16),
                   jax.ShapeDtypeStruct((SSM_W, 2 * SSM_FLAT), BF16),
                   jax.ShapeDtypeStruct((S5_CHUNK, SSM_FLAT), F32),
                   jax.ShapeDtypeStruct((S5_CHUNK, SSM_FLAT), F32)],
        compiler_params=pltpu.CompilerParams(vmem_limit_bytes=VMEM_LIMIT),
        name="s5_prep",
    )(flat(a_re), flat(a_im), flat(ldt), by_channel(b_re), by_channel(b_im), c_rows(c_re), c_rows(c_im))


def _s5_readout(h_re, h_im, u, ccat_ref, d_ref, wglu_ref, bglu_ref):
    hcat = jnp.concatenate([h_re.astype(BF16), h_im.astype(BF16)], axis=1)
    y = _dot_nt(hcat, ccat_ref[...]) + d_ref[...] * u
    z = _gelu(y)
    return z * jax.nn.sigmoid(_dot(z.astype(BF16), wglu_ref[...]) + bglu_ref[...])


def _s5_prompt_kernel(u_ref, bbar_ref, apr_ref, api_ref, a8r_ref, a8i_ref, ccat_ref, d_ref,
                      wglu_ref, bglu_ref,
                      y_ref, hre_ref, him_ref, cr_ref, ci_ref, hr_ref, hi_ref, pr_ref, pi_ref):
    c = pl.program_id(1)
    n = u_ref.shape[0]
    n_groups = n // SUBLANES

    @pl.when(c == 0)
    def _():
        cr_ref[...] = jnp.zeros_like(cr_ref)
        ci_ref[...] = jnp.zeros_like(ci_ref)

    def scan_level(hr, hi, pos, d, power):
        er, ei = apr_ref[power - 1:power, :], api_ref[power - 1:power, :]
        keep = pos >= d
        sr = jnp.where(keep, pltpu.roll(hr, d, axis=0), 0.0)
        si = jnp.where(keep, pltpu.roll(hi, d, axis=0), 0.0)
        return hr + er * sr - ei * si, hi + er * si + ei * sr

    u = u_ref[...]
    bu = _dot(u.astype(BF16), bbar_ref[...])
    hr, hi = bu[:, :SSM_FLAT], bu[:, SSM_FLAT:]
    hr = hr.reshape(n_groups, SUBLANES, SSM_FLAT)
    hi = hi.reshape(n_groups, SUBLANES, SSM_FLAT)
    in_group = lax.broadcasted_iota(jnp.int32, (SUBLANES, SSM_FLAT), 0)
    d = 1
    while d < SUBLANES:
        er = jnp.where(in_group >= d, apr_ref[d - 1:d, :], 0.0)[None]
        ei = jnp.where(in_group >= d, api_ref[d - 1:d, :], 0.0)[None]
        sr, si = pltpu.roll(hr, d, axis=1), pltpu.roll(hi, d, axis=1)
        hr, hi = hr + er * sr - ei * si, hi + er * si + ei * sr
        d *= 2
    hr = hr.reshape(n, SSM_FLAT)
    hi = hi.reshape(n, SSM_FLAT)
    n_lt = SSM_FLAT // LANES
    lt_cols = [slice(l * LANES, (l + 1) * LANES) for l in range(n_lt)]
    for l in range(n_lt):
        hr_ref[l] = hr[:, lt_cols[l]]
        hi_ref[l] = hi[:, lt_cols[l]]

    ends = pl.ds(SUBLANES - 1, n_groups, stride=SUBLANES)
    gr = jnp.concatenate([hr_ref[l, ends, :] for l in range(n_lt)], axis=1)
    gi = jnp.concatenate([hi_ref[l, ends, :] for l in range(n_lt)], axis=1)
    group = lax.broadcasted_iota(jnp.int32, (n_groups, SSM_FLAT), 0)
    d = 1
    while d < n_groups:
        gr, gi = scan_level(gr, gi, group, d, d * SUBLANES)
        d *= 2
    cr, ci = cr_ref[...], ci_ref[...]
    a8r, a8i = a8r_ref[...], a8i_ref[...]
    gr = gr + a8r * cr - a8i * ci
    gi = gi + a8r * ci + a8i * cr
    cr_ref[...] = gr[n_groups - 1:n_groups, :]
    ci_ref[...] = gi[n_groups - 1:n_groups, :]
    hre_ref[...] = gr[n_groups - 1:n_groups, :]
    him_ref[...] = gi[n_groups - 1:n_groups, :]
    first = group == 0
    pr_ref[...] = jnp.where(first, cr, pltpu.roll(gr, 1, axis=0))
    pi_ref[...] = jnp.where(first, ci, pltpu.roll(gi, 1, axis=0))

    a1r, a1i = apr_ref[0:SUBLANES, :], api_ref[0:SUBLANES, :]
    for g in range(n_groups):
        rows = slice(g * SUBLANES, (g + 1) * SUBLANES)
        br = jnp.broadcast_to(pr_ref[g:g + 1, :], (SUBLANES, SSM_FLAT))
        bi = jnp.broadcast_to(pi_ref[g:g + 1, :], (SUBLANES, SSM_FLAT))
        add_r = a1r * br - a1i * bi
        add_i = a1r * bi + a1i * br
        for l in range(n_lt):
            hr_ref[l, rows, :] = hr_ref[l, rows, :] + add_r[:, lt_cols[l]]
            hi_ref[l, rows, :] = hi_ref[l, rows, :] + add_i[:, lt_cols[l]]
    hr = jnp.concatenate([hr_ref[l] for l in range(n_lt)], axis=1)
    hi = jnp.concatenate([hi_ref[l] for l in range(n_lt)], axis=1)
    y_ref[...] = _s5_readout(hr, hi, u, ccat_ref, d_ref, wglu_ref, bglu_ref).astype(y_ref.dtype)


def _s5_prompt(u, bbar, apr, api, ccat, d_row, wglu_bf, bglu_row):
    b, t, _ = u.shape
    n = S5_CHUNK
    full = lambda a: pl.BlockSpec(a.shape, lambda bi, c: (0,) * a.ndim)
    tok = pl.BlockSpec((None, n, SSM_W), lambda bi, c: (bi, c, 0))
    st = pl.BlockSpec((None, 1, SSM_FLAT), lambda bi, c: (bi, 0, 0))
    a8r, a8i = apr[SUBLANES - 1::SUBLANES], api[SUBLANES - 1::SUBLANES]
    lane_major = pltpu.VMEM((SSM_FLAT // LANES, n, LANES), F32)
    y, hre, him = pl.pallas_call(
        _s5_prompt_kernel,
        grid=(b, t // n),
        in_specs=[tok, full(bbar), full(apr), full(api), full(a8r), full(a8i), full(ccat),
                  full(d_row), full(wglu_bf), full(bglu_row)],
        out_specs=[tok, st, st],
        out_shape=[jax.ShapeDtypeStruct((b, t, SSM_W), BF16),
                   jax.ShapeDtypeStruct((b, 1, SSM_FLAT), F32),
                   jax.ShapeDtypeStruct((b, 1, SSM_FLAT), F32)],
        scratch_shapes=[pltpu.VMEM((1, SSM_FLAT), F32), pltpu.VMEM((1, SSM_FLAT), F32),
                        lane_major, lane_major,
                        pltpu.VMEM((n // SUBLANES, SSM_FLAT), F32),
                        pltpu.VMEM((n // SUBLANES, SSM_FLAT), F32)],
        compiler_params=_cparams("parallel", "arbitrary"),
        name="s5_prompt",
    )(u, bbar, apr, api, a8r, a8i, ccat, d_row, wglu_bf, bglu_row)
    return y, hre.reshape(b, SSM_GROUPS, SSM_STATE), him.reshape(b, SSM_GROUPS, SSM_STATE)


def _s5_sample_kernel(u_ref, h0r_ref, h0i_ref, bbar_ref, apr_ref, api_ref, ccat_ref, d_ref,
                      wglu_ref, bglu_ref, y_ref, hre_ref, him_ref, *, n_steps):
    hr, hi = h0r_ref[...], h0i_ref[...]
    ar, ai = apr_ref[0:1, :], api_ref[0:1, :]
    for t in range(n_steps):
        u = u_ref[:, t * SSM_W:(t + 1) * SSM_W]
        bu = _dot(u.astype(BF16), bbar_ref[...])
        hr, hi = (ar * hr - ai * hi + bu[:, :SSM_FLAT],
                  ar * hi + ai * hr + bu[:, SSM_FLAT:])
        y_ref[:, t * SSM_W:(t + 1) * SSM_W] = _s5_readout(
            hr, hi, u, ccat_ref, d_ref, wglu_ref, bglu_ref).astype(y_ref.dtype)
    hre_ref[...] = hr
    him_ref[...] = hi


def _s5_sample(u, h0_re, h0_im, bbar, apr, api, ccat, d_row, wglu_bf, bglu_row):
    db, t, _ = u.shape
    y, hre, him = pl.pallas_call(
        functools.partial(_s5_sample_kernel, n_steps=t),
        out_shape=[jax.ShapeDtypeStruct((db, t * SSM_W), BF16),
                   jax.ShapeDtypeStruct((db, SSM_FLAT), F32),
                   jax.ShapeDtypeStruct((db, SSM_FLAT), F32)],
        compiler_params=pltpu.CompilerParams(vmem_limit_bytes=VMEM_LIMIT),
        name="s5_sample",
    )(u.reshape(db, t * SSM_W), h0_re.reshape(db, SSM_FLAT), h0_im.reshape(db, SSM_FLAT),
      bbar, apr, api, ccat, d_row, wglu_bf, bglu_row)
    return (y.reshape(db, t, SSM_W), hre.reshape(db, SSM_GROUPS, SSM_STATE),
            him.reshape(db, SSM_GROUPS, SSM_STATE))


MERGE_TM = 512


def _merge_kernel(x_ref, att_ref, ssm_ref, g_ref, wa_ref, wb_ref, wo_ref, lg_ref, lb_ref, o_ref):
    ya = _dot(att_ref[...], wa_ref[...])
    yb = _dot(ssm_ref[...], wb_ref[...])
    merged = g_ref[:, :D_MODEL] * ya + g_ref[:, D_MODEL:] * yb
    y = DN_ALPHA * x_ref[...] + _dot(merged.astype(BF16), wo_ref[...])
    o_ref[...] = _layer_norm(y, lg_ref[...], lb_ref[...])


def _merge(x, att, ssm, gates, wa_bf, wb_bf, wo_bf, lg_row, lb_row):
    n = x.shape[0]
    tm = MERGE_TM
    row = lambda w: pl.BlockSpec((tm, w), lambda i: (i, 0))
    full = lambda a: pl.BlockSpec(a.shape, lambda i: (0,) * a.ndim)
    return pl.pallas_call(
        _merge_kernel,
        grid=(n // tm,),
        in_specs=[row(D_MODEL), row(ATT_W), row(SSM_W), row(2 * D_MODEL),
                  full(wa_bf), full(wb_bf), full(wo_bf), full(lg_row), full(lb_row)],
        out_specs=row(D_MODEL),
        out_shape=jax.ShapeDtypeStruct((n, D_MODEL), F32),
        compiler_params=_cparams("parallel"),
        name="merge",
    )(x, att, ssm, gates, wa_bf, wb_bf, wo_bf, lg_row, lb_row)


PEER_TS = 512
PEER_EC = 1024
PEER_NCH = PEER_EXPERTS // PEER_EC
PEER_ROWS_PER_STEP = PEER_EC // PEER_NKEYS
HALF_KEY = PEER_DKEY // 2


def _sorting_network(n):
    pairs = []
    t = (n - 1).bit_length()
    p = 1 << (t - 1)
    while p > 0:
        q, r, d = 1 << (t - 1), 0, p
        while d > 0:
            pairs.extend((i, i + d) for i in range(n - d) if (i & p) == r)
            d, q, r = q - p, q >> 1, p
        p >>= 1
    return pairs


def _top_sorted(s, k):
    assert s.shape[0] == k * SUBLANES and k & (k - 1) == 0
    v = [s[i * SUBLANES:(i + 1) * SUBLANES, :] for i in range(k)]
    for i, j in _sorting_network(k):
        v[i], v[j] = jnp.maximum(v[i], v[j]), jnp.minimum(v[i], v[j])
    shift = SUBLANES // 2
    while shift:
        other = [pltpu.roll(x, shift, axis=0) for x in v]
        v = [jnp.maximum(v[i], other[k - 1 - i]) for i in range(k)]
        stride = k // 2
        while stride:
            for i in range(k):
                if (i // stride) % 2 == 0:
                    lo, hi = v[i], v[i + stride]
                    v[i], v[i + stride] = jnp.maximum(lo, hi), jnp.minimum(lo, hi)
            stride //= 2
        shift //= 2
    return v


def _peer_tables(s1, s2):
    k = PEER_TOPK
    c = s1.shape[1]
    groups3 = (k, SUBLANES, c)
    v1 = _top_sorted(s1, k)
    v2 = _top_sorted(s2, k)
    s2g = s2.reshape(groups3)
    rank2 = jnp.zeros(groups3, F32)
    for b in range(k):
        rank2 = jnp.where(s2g < v2[b][None], float(b + 1), rank2)
    sub = lax.broadcasted_iota(jnp.int32, (SUBLANES, c), 0)

    def pack(rows):
        out = rows[0]
        for r in range(1, SUBLANES):
            out = jnp.where(sub == r, rows[r], out)
        return out

    v2_lo, v2_hi, v1_hi = pack(v2[:SUBLANES]), pack(v2[SUBLANES:]), pack(v1[SUBLANES:])
    cands = [v1[0] + v2_lo, v1[0] + v2_hi, v1[1] + v2_lo]
    for a in range(2, SUBLANES):
        cands.append(jnp.where(sub < k // (a + 1), v1[a] + v2_lo, NEG_INF))
    cands.append(v1_hi + v2[0])
    cands += [jnp.full((SUBLANES, c), NEG_INF, F32)] * (k - len(cands))
    sc = _top_sorted(jnp.concatenate(cands, axis=0), k)
    tau = sc[k - 1]
    z = jnp.zeros_like(tau)
    for r in range(k):
        z = z + jnp.exp(sc[r] - sc[0])
    s1g = s1.reshape(groups3)
    count = jnp.zeros(groups3, F32)
    for a in range(k):
        n_sel = jnp.zeros_like(tau)
        for b in range(k // (a + 1)):
            n_sel = n_sel + jnp.where(v1[a] + v2[b] >= tau, 1.0, 0.0)
        count = jnp.where(s1g == v1[a][None], n_sel[None], count)
    row_weight = jnp.exp(s1g - v1[0][None]) / z[None]
    p2 = jnp.exp(s2g - v2[0][None])
    flat = lambda a: a.reshape(k * SUBLANES, c)
    return flat(rank2), flat(p2), flat(count), flat(row_weight)


def _peer_kernel(x_ref, wq_ref, k1_ref, k2_ref,
                 u0_ref, un_ref, vp_ref, vl_ref, lg_ref, lb_ref, o_ref,
                 xb_ref, s_ref, r2_ref, p2_ref, cnt_ref, cw_ref, acc_ref, st_ref, ht_ref, w_ref):
    c = pl.program_id(1)
    ts = x_ref.shape[0]
    groups = PEER_NKEYS // BF16_ROWS
    cur = c % 2
    nxt = 1 - cur

    @pl.when(c == 0)
    def _prologue():
        xb = x_ref[...].astype(BF16)
        xb_ref[...] = xb
        st_ref[0] = _dot_nt(u0_ref[...], xb).astype(BF16)
        qt = _dot_nt(wq_ref[...], xb).astype(BF16)
        for h in range(PEER_HEADS):
            base = h * PEER_DKEY
            s_ref[0] = _dot(k1_ref[h], qt[base:base + HALF_KEY, :])
            s_ref[1] = _dot(k2_ref[h], qt[base + HALF_KEY:base + PEER_DKEY, :])

            def lane_tile(lt, _):
                cols = pl.ds(pl.multiple_of(lt * LANES, LANES), LANES)
                rank2, p2, count, row_weight = _peer_tables(s_ref[0, :, cols], s_ref[1, :, cols])
                r2_ref[h, :, :, cols] = rank2.astype(BF16).reshape(groups, BF16_ROWS, LANES)
                p2_ref[h, :, :, cols] = p2.astype(BF16).reshape(groups, BF16_ROWS, LANES)
                cnt_ref[h, :, cols] = count
                cw_ref[h, :, cols] = row_weight
                return 0

            lax.fori_loop(0, ts // LANES, lane_tile, 0)
        acc_ref[...] = jnp.zeros_like(acc_ref)
        ht_ref[1] = jnp.zeros((PEER_EC, ts), BF16)

    i1_base = pl.multiple_of(c * PEER_ROWS_PER_STEP, PEER_ROWS_PER_STEP)
    cnts = [cnt_ref[h, pl.ds(i1_base, PEER_ROWS_PER_STEP), :] for h in range(PEER_HEADS)]
    cws = [cw_ref[h, pl.ds(i1_base, PEER_ROWS_PER_STEP), :] for h in range(PEER_HEADS)]
    mblk = 2 * PEER_NKEYS
    for j in range(PEER_ROWS_PER_STEP):
        rows = slice(j * PEER_NKEYS, (j + 1) * PEER_NKEYS)
        w = jnp.zeros((groups, BF16_ROWS, ts), BF16)
        for h in range(PEER_HEADS):
            cnt = jnp.broadcast_to(cnts[h][j:j + 1, :], (BF16_ROWS, ts)).astype(BF16)
            cw = jnp.broadcast_to(cws[h][j:j + 1, :], (BF16_ROWS, ts)).astype(BF16)
            w = w + jnp.where(r2_ref[h] < cnt[None], cw[None], 0.0) * p2_ref[h]
        w_ref[rows, :] = w.reshape(PEER_NKEYS, ts)
        if j % 2 == 1:
            er = slice((j // 2) * mblk, (j // 2 + 1) * mblk)
            st_ref[nxt, er, :] = _dot_nt(un_ref[er, :], xb_ref[...]).astype(BF16)
    acc_ref[...] += _dot(vp_ref[...], ht_ref[nxt])
    for j in range(PEER_ROWS_PER_STEP):
        rows = slice(j * PEER_NKEYS, (j + 1) * PEER_NKEYS)
        ht_ref[cur, rows, :] = w_ref[rows, :] * _gelu(st_ref[cur, rows, :])

    @pl.when(c == PEER_NCH - 1)
    def _epilogue():
        acc = acc_ref[...] + _dot(vl_ref[...], ht_ref[cur])
        y = DN_ALPHA * x_ref[...] + acc.T
        o_ref[...] = _layer_norm(y, lg_ref[...], lb_ref[...])


def _peer(x1, wq_bf, k1_bf, k2_bf, u_bf, vt_bf, lg_row, lb_row):
    n = x1.shape[0]
    ts = PEER_TS
    last = PEER_NCH - 1
    full = lambda a: pl.BlockSpec(a.shape, lambda i, c: (0,) * a.ndim)
    tok = pl.BlockSpec((ts, D_MODEL), lambda i, c: (i, 0))
    tables = pltpu.VMEM((PEER_HEADS, PEER_NKEYS, ts), F32)
    packed = pltpu.VMEM((PEER_HEADS, PEER_NKEYS // BF16_ROWS, BF16_ROWS, ts), BF16)
    return pl.pallas_call(
        _peer_kernel,
        grid=(n // ts, PEER_NCH),
        in_specs=[tok, full(wq_bf), full(k1_bf), full(k2_bf),
                  pl.BlockSpec((PEER_EC, D_MODEL), lambda i, c: (0, 0)),
                  pl.BlockSpec((PEER_EC, D_MODEL), lambda i, c: (jnp.minimum(c + 1, last), 0)),
                  pl.BlockSpec((D_MODEL, PEER_EC), lambda i, c: (0, jnp.maximum(c - 1, 0))),
                  pl.BlockSpec((D_MODEL, PEER_EC), lambda i, c: (0, last)),
                  full(lg_row), full(lb_row)],
        out_specs=tok,
        out_shape=jax.ShapeDtypeStruct((n, D_MODEL), F32),
        scratch_shapes=[
            pltpu.VMEM((ts, D_MODEL), BF16),
            pltpu.VMEM((2, PEER_NKEYS, ts), F32),
            packed, packed,
            tables, tables,
            pltpu.VMEM((D_MODEL, ts), F32),
            pltpu.VMEM((2, PEER_EC, ts), BF16),
            pltpu.VMEM((2, PEER_EC, ts), BF16),
            pltpu.VMEM((PEER_EC, ts), BF16),
        ],
        compiler_params=_cparams("parallel", "arbitrary"),
        name="peer",
    )(x1, wq_bf, k1_bf, k2_bf, u_bf, u_bf, vt_bf, vt_bf, lg_row, lb_row)


def kernel(x_prompt, x_sample, cache_k, cache_v, state_ssm_re, state_ssm_im, page_table, w_in, b_in, w_a, w_b, w_o, ln1_g, ln1_b, a_re, a_im, log_dt, b_re, b_im, c_re, c_im, d_skip, w_glu, b_glu, ln2_g, ln2_b, w_pq, sub_k1, sub_k2, peer_u, peer_v):
    bn, t, _ = x_prompt.shape
    db, dt_, _ = x_sample.shape
    row = lambda a: a.reshape(1, -1).astype(F32)
    slopes = jnp.exp2(-8.0 * (jnp.arange(N_HEADS, dtype=F32) + 1.0) / N_HEADS)

    w_in_bf = w_in.astype(BF16)
    proj_w = (w_in_bf, row(b_in), w_in[:, ATT_W:3 * ATT_W].T.astype(BF16),
              b_in[ATT_W:3 * ATT_W].reshape(2 * ATT_W, 1).astype(F32))
    wa_bf, wb_bf, wo_bf, wglu_bf = (w.astype(BF16) for w in (w_a, w_b, w_o, w_glu))
    peer_w = (w_pq.T.astype(BF16), sub_k1.astype(BF16), sub_k2.astype(BF16))
    u_bf = peer_u.astype(BF16)
    vt_bf = peer_v.T.astype(BF16)

    bbar, ccat, apr, api = _s5_prep(a_re, a_im, log_dt, b_re, b_im, c_re, c_im)
    s5_w = (bbar, apr, api, ccat, row(d_skip), wglu_bf, row(b_glu))

    xp = x_prompt.reshape(bn * t, D_MODEL)
    q, k_t, v_t, u, gates = _proj(xp, *proj_w, seq_len=t)
    att = _moba_prompt(q.reshape(bn, t, ATT_W), k_t, v_t, slopes)
    ssm, hre_p, him_p = _s5_prompt(u.reshape(bn, t, SSM_W), *s5_w)
    x1 = _merge(xp, att.reshape(bn * t, ATT_W), ssm.reshape(bn * t, SSM_W), gates,
                wa_bf, wb_bf, wo_bf, row(ln1_g), row(ln1_b))
    y_prompt = _peer(x1, *peer_w, u_bf, vt_bf, row(ln2_g), row(ln2_b)).reshape(bn, t, D_MODEL)
    per_token = lambda a: a.reshape(bn, N_HEADS, HEAD_DIM, t).transpose(0, 3, 1, 2)
    k_prompt, v_prompt = per_token(k_t), per_token(v_t)

    xs = x_sample.reshape(db * dt_, D_MODEL)
    q, k, v, u, gates = _proj(xs, *proj_w)
    att = _moba_sample(q.reshape(db, dt_, ATT_W), k.reshape(db, dt_, ATT_W), v.reshape(db, dt_, ATT_W),
                       cache_k, cache_v, page_table, slopes)
    ssm, hre_s, him_s = _s5_sample(u.reshape(db, dt_, SSM_W), state_ssm_re, state_ssm_im, *s5_w)
    x1 = _merge(xs, att.reshape(db * dt_, ATT_W), ssm.reshape(db * dt_, SSM_W), gates,
                wa_bf, wb_bf, wo_bf, row(ln1_g), row(ln1_b))
    y_sample = _peer(x1, *peer_w, u_bf, vt_bf, row(ln2_g), row(ln2_b)).reshape(db, dt_, D_MODEL)
    k_sample = k.reshape(db, dt_, N_HEADS, HEAD_DIM)
    v_sample = v.reshape(db, dt_, N_HEADS, HEAD_DIM)

    return (y_prompt, y_sample, k_prompt, v_prompt, k_sample, v_sample,
            hre_p, him_p, hre_s, him_s)
```

```python
import functools
import math

import jax
import jax.numpy as jnp
from jax import lax
from jax.experimental import pallas as pl
from jax.experimental.pallas import tpu as pltpu

F32 = jnp.float32
BF16 = jnp.bfloat16

D_MODEL = 1024
ATT_W = 512
HEAD_DIM = 64
N_HEADS = 8
MOBA_BLOCK = 256
MOBA_TOPK = 3
SSM_W = 512
SSM_GROUP = 16
SSM_GROUPS = 32
SSM_STATE = 64
SSM_FLAT = SSM_GROUPS * SSM_STATE
PEER_HEADS = 8
PEER_NKEYS = 128
PEER_EXPERTS = PEER_NKEYS * PEER_NKEYS
PEER_DKEY = 256
PEER_TOPK = 16
PROJ_W = 3 * ATT_W + SSM_W + 2 * D_MODEL
DEPTH = 1
DN_ALPHA = (2.0 * DEPTH) ** 0.25
LN_EPS = 1e-5
PAGE_SIZE = 128

LANES = 128
SUBLANES = 8
BF16_ROWS = 16
NEG_INF = float("-inf")
VMEM_LIMIT = 56 * 1024 * 1024

NT_DIMS = (((1,), (1,)), ((), ()))


def _cparams(*sem):
    return pltpu.CompilerParams(dimension_semantics=sem, vmem_limit_bytes=VMEM_LIMIT)


def _dot(a, b):
    return jnp.dot(a, b, preferred_element_type=F32)


def _dot_nt(a, b):
    return lax.dot_general(a, b, NT_DIMS, preferred_element_type=F32)


def _split(x):
    hi = x.astype(BF16)
    lo = (x - hi.astype(F32)).astype(BF16)
    return hi, lo


def _gelu(x):
    c = math.sqrt(2.0 / math.pi)
    return 0.5 * x * (1.0 + jnp.tanh(c * (x + 0.044715 * (x * x * x))))


def _layer_norm(y, g, b):
    mu = jnp.mean(y, axis=-1, keepdims=True)
    yc = y - mu
    var = jnp.mean(yc * yc, axis=-1, keepdims=True)
    return yc * lax.rsqrt(var + LN_EPS) * g + b


PROJ_TM = 512


def _proj_kernel(x_ref, w_ref, b_ref, wkv_t_ref, bkv_col_ref, q_ref, k_ref, v_ref, u_ref, g_ref,
                 *, kv_transposed):
    xb = x_ref[...].astype(BF16)

    def seg(lo, hi):
        return _dot(xb, w_ref[:, lo:hi]) + b_ref[:, lo:hi]

    q_ref[...] = seg(0, ATT_W)
    if kv_transposed:
        kv_t = _dot_nt(wkv_t_ref[...], xb) + bkv_col_ref[...]
        k_ref[...] = kv_t[:ATT_W, :]
        v_ref[...] = kv_t[ATT_W:, :]
    else:
        k_ref[...] = seg(ATT_W, 2 * ATT_W)
        v_ref[...] = seg(2 * ATT_W, 3 * ATT_W)
    u_ref[...] = seg(3 * ATT_W, 3 * ATT_W + SSM_W)
    g_ref[...] = jax.nn.sigmoid(seg(3 * ATT_W + SSM_W, PROJ_W))


def _proj(x, w_bf, b_row, wkv_t, bkv_col, seq_len=None):
    n = x.shape[0]
    tm = PROJ_TM
    row = lambda w: pl.BlockSpec((tm, w), lambda i: (i, 0))
    full = lambda a: pl.BlockSpec(a.shape, lambda i: (0,) * a.ndim)
    if seq_len is None:
        kv_spec, kv_shape = row(ATT_W), jax.ShapeDtypeStruct((n, ATT_W), F32)
    else:
        per_seq = seq_len // tm
        kv_spec = pl.BlockSpec((None, ATT_W, tm), lambda i: (i // per_seq, 0, i % per_seq))
        kv_shape = jax.ShapeDtypeStruct((n // seq_len, ATT_W, seq_len), F32)
    rows = lambda w: jax.ShapeDtypeStruct((n, w), F32)
    return pl.pallas_call(
        functools.partial(_proj_kernel, kv_transposed=seq_len is not None),
        grid=(n // tm,),
        in_specs=[row(D_MODEL), full(w_bf), full(b_row), full(wkv_t), full(bkv_col)],
        out_specs=[row(ATT_W), kv_spec, kv_spec, row(SSM_W), row(2 * D_MODEL)],
        out_shape=[rows(ATT_W), kv_shape, kv_shape, rows(SSM_W), rows(2 * D_MODEL)],
        compiler_params=_cparams("parallel"),
        name="proj",
    )(x, w_bf, b_row, wkv_t, bkv_col)


def _moba_select_bias(gates, cur):
    nb = len(gates)
    g = [jnp.where(n < cur, gates[n], NEG_INF) for n in range(nb)]
    bias = []
    for n in range(nb):
        rank = jnp.zeros(gates[n].shape, jnp.int32)
        for m in range(nb):
            if m != n:
                ahead = (g[m] >= g[n]) if m < n else (g[m] > g[n])
                rank = rank + ahead.astype(jnp.int32)
        sel = (n < cur) & (rank < MOBA_TOPK)
        bias.append(jnp.where(sel, 0.0, NEG_INF))
    return bias


def _moba_prompt_kernel(slopes_ref, q_ref, kt_ref, vt_in_ref, o_ref,
                        kb_ref, vt_ref, ot_ref):
    t = q_ref.shape[0]
    nb = t // MOBA_BLOCK
    blk = MOBA_BLOCK
    hp = pl.program_id(1)
    lane = lax.broadcasted_iota(jnp.int32, (1, LANES), 1)

    k = kt_ref[...].T
    kb_ref[...] = k.astype(BF16)
    vt_ref[...] = vt_in_ref[...].astype(BF16)
    kmean = jnp.mean(k.reshape(nb, blk, LANES), axis=1)
    q_all = q_ref[...]
    q_hi, q_lo = _split(q_all)
    cur = lax.broadcasted_iota(jnp.int32, (1, t), 1) // blk
    d0 = (lax.broadcasted_iota(jnp.int32, (blk, blk), 1)
          - lax.broadcasted_iota(jnp.int32, (blk, blk), 0)).astype(F32)

    heads = []
    for hh in range(2):
        hmask = (lane >= HEAD_DIM * hh) & (lane < HEAD_DIM * (hh + 1))
        slope = slopes_ref[2 * hp + hh]
        km_hi, km_lo = _split(jnp.where(hmask, kmean, 0.0))
        gate = _dot_nt(km_hi, q_hi) + _dot_nt(km_hi, q_lo) + _dot_nt(km_lo, q_hi)
        selb = _moba_select_bias([gate[n:n + 1, :] for n in range(nb)], cur)
        heads.append((hmask, slope, selb, slope * d0, slice(HEAD_DIM * hh, HEAD_DIM * (hh + 1))))

    for i in range(nb):
        qcols = slice(i * blk, (i + 1) * blk)
        n_keys = (i + 1) * blk
        q_blk = q_ref[qcols, :] * (HEAD_DIM ** -0.5)
        s_alls = [_dot_nt(kb_ref[0:n_keys, :], jnp.where(hmask, q_blk, 0.0).astype(BF16))
                  for hmask, *_ in heads]
        tiles = []
        for (hmask, slope, selb, slope_d0, rows), s_all in zip(heads, s_alls):
            per_head = []
            for j in range(i + 1):
                s = s_all[j * blk:(j + 1) * blk, :] - (slope_d0 + slope * float((i - j) * blk))
                if j < i:
                    s = s + selb[j][:, qcols]
                else:
                    s = jnp.where(d0 >= 0.0, s, NEG_INF)
                per_head.append(s)
            tiles.append(per_head)
        maxes = []
        for per_head in tiles:
            m = jnp.max(per_head[0], axis=0, keepdims=True)
            for s in per_head[1:]:
                m = jnp.maximum(m, jnp.max(s, axis=0, keepdims=True))
            maxes.append(m)
        probs = [[jnp.exp(s - m) for s in per_head] for per_head, m in zip(tiles, maxes)]
        sums = []
        for per_head in probs:
            l = jnp.sum(per_head[0], axis=0, keepdims=True)
            for p in per_head[1:]:
                l = l + jnp.sum(p, axis=0, keepdims=True)
            sums.append(l)
        for (hmask, slope, selb, slope_d0, rows), per_head, l in zip(heads, probs, sums):
            p_all = jnp.concatenate([p.astype(BF16) for p in per_head], axis=0)
            res = _dot(vt_ref[:, 0:n_keys], p_all) / l
            ot_ref[rows, qcols] = res[rows, :]

    o_ref[...] = ot_ref[...].T.astype(o_ref.dtype)


def _moba_prompt(q, k_t, v_t, slopes):
    b, t, _ = q.shape
    spec = pl.BlockSpec((None, t, LANES), lambda bi, hp, *_: (bi, 0, hp))
    spec_t = pl.BlockSpec((None, LANES, t), lambda bi, hp, *_: (bi, hp, 0))
    grid_spec = pltpu.PrefetchScalarGridSpec(
        num_scalar_prefetch=1,
        grid=(b, ATT_W // LANES),
        in_specs=[spec, spec_t, spec_t],
        out_specs=spec,
        scratch_shapes=[
            pltpu.VMEM((t, LANES), BF16),
            pltpu.VMEM((LANES, t), BF16),
            pltpu.VMEM((LANES, t), F32),
        ],
    )
    return pl.pallas_call(
        _moba_prompt_kernel,
        grid_spec=grid_spec,
        out_shape=jax.ShapeDtypeStruct((b, t, ATT_W), BF16),
        compiler_params=_cparams("parallel", "parallel"),
        name="moba_prompt",
    )(slopes, q, k_t, v_t)


MOBA_SAMPLE_SEQS = 2


def _moba_sample_kernel(pt_ref, slopes_ref, q_ref, kn_ref, vn_ref, *rest, n_pages):
    n_seq = q_ref.shape[0]
    o_ref = rest[2 * n_seq * n_pages]
    for sq in range(n_seq):
        kt_refs = rest[sq * n_pages:(sq + 1) * n_pages]
        vt_refs = rest[(n_seq + sq) * n_pages:(n_seq + sq + 1) * n_pages]
        _moba_sample_sequence(slopes_ref, q_ref.at[sq], kn_ref.at[sq], vn_ref.at[sq],
                              kt_refs, vt_refs, o_ref.at[sq], n_pages)


def _moba_sample_sequence(slopes_ref, q_ref, kn_ref, vn_ref, kt_refs, vt_refs, o_ref, n_pages):
    nq = q_ref.shape[0]
    n_past = n_pages * PAGE_SIZE
    nb = n_past // MOBA_BLOCK
    pages_per_block = MOBA_BLOCK // PAGE_SIZE
    rows = N_HEADS * nq
    scale = HEAD_DIM ** -0.5
    assert rows % SUBLANES == 0 and nq <= PAGE_SIZE

    q = q_ref[...]
    r_id = lax.broadcasted_iota(jnp.int32, (rows, ATT_W), 0)
    c_id = lax.broadcasted_iota(jnp.int32, (rows, ATT_W), 1)
    q_rep = jnp.concatenate([q] * (rows // nq), axis=0)
    own = c_id // HEAD_DIM == r_id // nq
    qs = jnp.where(own, q_rep * scale, 0.0).astype(BF16)

    r_col = lax.broadcasted_iota(jnp.int32, (rows, 1), 0)
    tq = (r_col % nq).astype(F32)
    slope = jnp.zeros((rows, 1), F32)
    for h in range(N_HEADS):
        slope = jnp.where(r_col // nq == h, slopes_ref[h], slope)
    lane = lax.broadcasted_iota(jnp.int32, (1, PAGE_SIZE), 1).astype(F32)

    raw = [_dot(qs, kt_refs[pg][...].astype(BF16)) for pg in range(n_pages)]

    gates = []
    for n in range(nb):
        total = raw[n * pages_per_block]
        for pg in range(n * pages_per_block + 1, (n + 1) * pages_per_block):
            total = total + raw[pg]
        gates.append(jnp.sum(total, axis=1, keepdims=True) * (1.0 / (MOBA_BLOCK * scale)))
    bias = _moba_select_bias(gates, nb)

    tiles = []
    for pg in range(n_pages):
        dist = (tq + float(n_past - pg * PAGE_SIZE)) - lane
        tiles.append(raw[pg] - slope * dist + bias[pg // pages_per_block])
    pad_rows = lambda a: jnp.concatenate([a, jnp.zeros((PAGE_SIZE - nq, ATT_W), F32)], axis=0)
    dist_new = tq - lane
    s_new = _dot_nt(qs, pad_rows(kn_ref[...]).astype(BF16))
    tiles.append(jnp.where(dist_new >= 0.0, s_new - slope * dist_new, NEG_INF))

    top = tiles[0]
    for s in tiles[1:]:
        top = jnp.maximum(top, s)
    m = jnp.max(top, axis=1, keepdims=True)
    probs = [jnp.exp(s - m) for s in tiles]
    total = probs[0]
    for p in probs[1:]:
        total = total + p
    inv_l = 1.0 / jnp.sum(total, axis=1, keepdims=True)
    out = _dot((probs[n_pages] * inv_l).astype(BF16), pad_rows(vn_ref[...]).astype(BF16))
    for pg in range(n_pages):
        out = out + _dot_nt((probs[pg] * inv_l).astype(BF16), vt_refs[pg][...].astype(BF16))

    res = jnp.zeros((nq, ATT_W), F32)
    c_q = lax.broadcasted_iota(jnp.int32, (nq, ATT_W), 1)
    for h in range(N_HEADS):
        res = jnp.where(c_q // HEAD_DIM == h, out[h * nq:(h + 1) * nq, :], res)
    o_ref[...] = res.astype(o_ref.dtype)


def _moba_sample(q, k_new, v_new, cache_k, cache_v, page_table, slopes):
    db, nq, _ = q.shape
    n_pages = page_table.shape[1]
    sps = MOBA_SAMPLE_SEQS
    new_spec = pl.BlockSpec((sps, nq, ATT_W), lambda s, *_: (s, 0, 0))

    def page_spec(sq, pg):
        return pl.BlockSpec((None, ATT_W, PAGE_SIZE), lambda s, pt, sl: (pt[sps * s + sq, pg], 0, 0))

    pages = [page_spec(sq, pg) for sq in range(sps) for pg in range(n_pages)]
    grid_spec = pltpu.PrefetchScalarGridSpec(
        num_scalar_prefetch=2,
        grid=(db // sps,),
        in_specs=[new_spec] * 3 + pages * 2,
        out_specs=new_spec,
    )
    by_token = lambda c: jnp.transpose(c, (0, 2, 3, 1)).reshape(c.shape[0], ATT_W, PAGE_SIZE)
    ck, cv = by_token(cache_k), by_token(cache_v)
    return pl.pallas_call(
        functools.partial(_moba_sample_kernel, n_pages=n_pages),
        grid_spec=grid_spec,
        out_shape=jax.ShapeDtypeStruct((db, nq, ATT_W), BF16),
        compiler_params=_cparams("parallel"),
        name="moba_sample",
    )(page_table, slopes, q, k_new, v_new, *([ck] * (sps * n_pages)), *([cv] * (sps * n_pages)))


S5_CHUNK = 256


def _s5_prep_kernel(are_ref, aim_ref, ldt_ref, bre_ref, bim_ref, cre_ref, cim_ref,
                    bbar_ref, ccat_ref, apr_ref, api_ref):
    ar, ai = are_ref[...], aim_ref[...]
    dt = jnp.exp(ldt_ref[...])
    mag = jnp.exp(dt * ar)
    ang = dt * ai
    abr, abi = mag * jnp.cos(ang), mag * jnp.sin(ang)
    den = ar * ar + ai * ai
    nr, ni = abr - 1.0, abi
    f_re = (nr * ar + ni * ai) / den
    f_im = (ni * ar - nr * ai) / den

    row_g = lax.broadcasted_iota(jnp.int32, (SSM_W, SSM_FLAT), 0) // SSM_GROUP
    col_g = lax.broadcasted_iota(jnp.int32, (SSM_W, SSM_FLAT), 1) // SSM_STATE
    same_group = row_g == col_g

    def block_diag(ref):
        x = ref[...]
        pair = jnp.concatenate([x] * (LANES // SSM_STATE), axis=1)
        return jnp.where(same_group, jnp.concatenate([pair] * (SSM_FLAT // LANES), axis=1), 0.0)

    bre, bim = block_diag(bre_ref), block_diag(bim_ref)
    bbar_ref[:, :SSM_FLAT] = (f_re * bre - f_im * bim).astype(BF16)
    bbar_ref[:, SSM_FLAT:] = (f_re * bim + f_im * bre).astype(BF16)
    ccat_ref[:, :SSM_FLAT] = block_diag(cre_ref).astype(BF16)
    ccat_ref[:, SSM_FLAT:] = (-block_diag(cim_ref)).astype(BF16)
    apr_ref[0:1, :] = abr
    api_ref[0:1, :] = abi
    d = 1
    while d < S5_CHUNK:
        pr, pi = apr_ref[0:d, :], api_ref[0:d, :]
        er, ei = apr_ref[d - 1:d, :], api_ref[d - 1:d, :]
        apr_ref[d:2 * d, :] = pr * er - pi * ei
        api_ref[d:2 * d, :] = pr * ei + pi * er
        d *= 2


def _s5_prep(a_re, a_im, log_dt, b_re, b_im, c_re, c_im):
    flat = lambda a: a.reshape(1, SSM_FLAT)
    ldt = jnp.broadcast_to(log_dt[:, None], (SSM_GROUPS, SSM_STATE))
    by_channel = lambda b: b.transpose(0, 2, 1).reshape(SSM_W, SSM_STATE)
    c_rows = lambda cm: cm.reshape(SSM_W, SSM_STATE)
    return pl.pallas_call(
        _s5_prep_kernel,
        out_shape=[jax.ShapeDtypeStruct((SSM_W, 2 * SSM_FLAT), BF16),
                   jax.ShapeDtypeStruct((SSM_W, 2 * SSM_FLAT), BF16),
                   jax.ShapeDtypeStruct((S5_CHUNK, SSM_FLAT), F32),
                   jax.ShapeDtypeStruct((S5_CHUNK, SSM_FLAT), F32)],
        compiler_params=pltpu.CompilerParams(vmem_limit_bytes=VMEM_LIMIT),
        name="s5_prep",
    )(flat(a_re), flat(a_im), flat(ldt), by_channel(b_re), by_channel(b_im), c_rows(c_re), c_rows(c_im))


def _s5_readout(h_re, h_im, u, ccat_ref, d_ref, wglu_ref, bglu_ref):
    hcat = jnp.concatenate([h_re.astype(BF16), h_im.astype(BF16)], axis=1)
    y = _dot_nt(hcat, ccat_ref[...]) + d_ref[...] * u
    z = _gelu(y)
    return z * jax.nn.sigmoid(_dot(z.astype(BF16), wglu_ref[...]) + bglu_ref[...])


def _s5_prompt_kernel(u_ref, bbar_ref, apr_ref, api_ref, a8r_ref, a8i_ref, ccat_ref, d_ref,
                      wglu_ref, bglu_ref,
                      y_ref, hre_ref, him_ref, cr_ref, ci_ref, hr_ref, hi_ref, pr_ref, pi_ref):
    c = pl.program_id(1)
    n = u_ref.shape[0]
    n_groups = n // SUBLANES

    @pl.when(c == 0)
    def _():
        cr_ref[...] = jnp.zeros_like(cr_ref)
        ci_ref[...] = jnp.zeros_like(ci_ref)

    def scan_level(hr, hi, pos, d, power):
        er, ei = apr_ref[power - 1:power, :], api_ref[power - 1:power, :]
        keep = pos >= d
        sr = jnp.where(keep, pltpu.roll(hr, d, axis=0), 0.0)
        si = jnp.where(keep, pltpu.roll(hi, d, axis=0), 0.0)
        return hr + er * sr - ei * si, hi + er * si + ei * sr

    u = u_ref[...]
    bu = _dot(u.astype(BF16), bbar_ref[...])
    hr, hi = bu[:, :SSM_FLAT], bu[:, SSM_FLAT:]
    hr = hr.reshape(n_groups, SUBLANES, SSM_FLAT)
    hi = hi.reshape(n_groups, SUBLANES, SSM_FLAT)
    in_group = lax.broadcasted_iota(jnp.int32, (SUBLANES, SSM_FLAT), 0)
    d = 1
    while d < SUBLANES:
        er = jnp.where(in_group >= d, apr_ref[d - 1:d, :], 0.0)[None]
        ei = jnp.where(in_group >= d, api_ref[d - 1:d, :], 0.0)[None]
        sr, si = pltpu.roll(hr, d, axis=1), pltpu.roll(hi, d, axis=1)
        hr, hi = hr + er * sr - ei * si, hi + er * si + ei * sr
        d *= 2
    hr = hr.reshape(n, SSM_FLAT)
    hi = hi.reshape(n, SSM_FLAT)
    n_lt = SSM_FLAT // LANES
    lt_cols = [slice(l * LANES, (l + 1) * LANES) for l in range(n_lt)]
    for l in range(n_lt):
        hr_ref[l] = hr[:, lt_cols[l]]
        hi_ref[l] = hi[:, lt_cols[l]]

    ends = pl.ds(SUBLANES - 1, n_groups, stride=SUBLANES)
    gr = jnp.concatenate([hr_ref[l, ends, :] for l in range(n_lt)], axis=1)
    gi = jnp.concatenate([hi_ref[l, ends, :] for l in range(n_lt)], axis=1)
    group = lax.broadcasted_iota(jnp.int32, (n_groups, SSM_FLAT), 0)
    d = 1
    while d < n_groups:
        gr, gi = scan_level(gr, gi, group, d, d * SUBLANES)
        d *= 2
    cr, ci = cr_ref[...], ci_ref[...]
    a8r, a8i = a8r_ref[...], a8i_ref[...]
    gr = gr + a8r * cr - a8i * ci
    gi = gi + a8r * ci + a8i * cr
    cr_ref[...] = gr[n_groups - 1:n_groups, :]
    ci_ref[...] = gi[n_groups - 1:n_groups, :]
    hre_ref[...] = gr[n_groups - 1:n_groups, :]
    him_ref[...] = gi[n_groups - 1:n_groups, :]
    first = group == 0
    pr_ref[...] = jnp.where(first, cr, pltpu.roll(gr, 1, axis=0))
    pi_ref[...] = jnp.where(first, ci, pltpu.roll(gi, 1, axis=0))

    a1r, a1i = apr_ref[0:SUBLANES, :], api_ref[0:SUBLANES, :]
    for g in range(n_groups):
        rows = slice(g * SUBLANES, (g + 1) * SUBLANES)
        br = jnp.broadcast_to(pr_ref[g:g + 1, :], (SUBLANES, SSM_FLAT))
        bi = jnp.broadcast_to(pi_ref[g:g + 1, :], (SUBLANES, SSM_FLAT))
        add_r = a1r * br - a1i * bi
        add_i = a1r * bi + a1i * br
        for l in range(n_lt):
            hr_ref[l, rows, :] = hr_ref[l, rows, :] + add_r[:, lt_cols[l]]
            hi_ref[l, rows, :] = hi_ref[l, rows, :] + add_i[:, lt_cols[l]]
    hr = jnp.concatenate([hr_ref[l] for l in range(n_lt)], axis=1)
    hi = jnp.concatenate([hi_ref[l] for l in range(n_lt)], axis=1)
    y_ref[...] = _s5_readout(hr, hi, u, ccat_ref, d_ref, wglu_ref, bglu_ref).astype(y_ref.dtype)


def _s5_prompt(u, bbar, apr, api, ccat, d_row, wglu_bf, bglu_row):
    b, t, _ = u.shape
    n = S5_CHUNK
    full = lambda a: pl.BlockSpec(a.shape, lambda bi, c: (0,) * a.ndim)
    tok = pl.BlockSpec((None, n, SSM_W), lambda bi, c: (bi, c, 0))
    st = pl.BlockSpec((None, 1, SSM_FLAT), lambda bi, c: (bi, 0, 0))
    a8r, a8i = apr[SUBLANES - 1::SUBLANES], api[SUBLANES - 1::SUBLANES]
    lane_major = pltpu.VMEM((SSM_FLAT // LANES, n, LANES), F32)
    y, hre, him = pl.pallas_call(
        _s5_prompt_kernel,
        grid=(b, t // n),
        in_specs=[tok, full(bbar), full(apr), full(api), full(a8r), full(a8i), full(ccat),
                  full(d_row), full(wglu_bf), full(bglu_row)],
        out_specs=[tok, st, st],
        out_shape=[jax.ShapeDtypeStruct((b, t, SSM_W), BF16),
                   jax.ShapeDtypeStruct((b, 1, SSM_FLAT), F32),
                   jax.ShapeDtypeStruct((b, 1, SSM_FLAT), F32)],
        scratch_shapes=[pltpu.VMEM((1, SSM_FLAT), F32), pltpu.VMEM((1, SSM_FLAT), F32),
                        lane_major, lane_major,
                        pltpu.VMEM((n // SUBLANES, SSM_FLAT), F32),
                        pltpu.VMEM((n // SUBLANES, SSM_FLAT), F32)],
        compiler_params=_cparams("parallel", "arbitrary"),
        name="s5_prompt",
    )(u, bbar, apr, api, a8r, a8i, ccat, d_row, wglu_bf, bglu_row)
    return y, hre.reshape(b, SSM_GROUPS, SSM_STATE), him.reshape(b, SSM_GROUPS, SSM_STATE)


def _s5_sample_kernel(u_ref, h0r_ref, h0i_ref, bbar_ref, apr_ref, api_ref, ccat_ref, d_ref,
                      wglu_ref, bglu_ref, y_ref, hre_ref, him_ref, *, n_steps):
    hr, hi = h0r_ref[...], h0i_ref[...]
    ar, ai = apr_ref[0:1, :], api_ref[0:1, :]
    for t in range(n_steps):
        u = u_ref[:, t * SSM_W:(t + 1) * SSM_W]
        bu = _dot(u.astype(BF16), bbar_ref[...])
        hr, hi = (ar * hr - ai * hi + bu[:, :SSM_FLAT],
                  ar * hi + ai * hr + bu[:, SSM_FLAT:])
        y_ref[:, t * SSM_W:(t + 1) * SSM_W] = _s5_readout(
            hr, hi, u, ccat_ref, d_ref, wglu_ref, bglu_ref).astype(y_ref.dtype)
    hre_ref[...] = hr
    him_ref[...] = hi


def _s5_sample(u, h0_re, h0_im, bbar, apr, api, ccat, d_row, wglu_bf, bglu_row):
    db, t, _ = u.shape
    y, hre, him = pl.pallas_call(
        functools.partial(_s5_sample_kernel, n_steps=t),
        out_shape=[jax.ShapeDtypeStruct((db, t * SSM_W), BF16),
                   jax.ShapeDtypeStruct((db, SSM_FLAT), F32),
                   jax.ShapeDtypeStruct((db, SSM_FLAT), F32)],
        compiler_params=pltpu.CompilerParams(vmem_limit_bytes=VMEM_LIMIT),
        name="s5_sample",
    )(u.reshape(db, t * SSM_W), h0_re.reshape(db, SSM_FLAT), h0_im.reshape(db, SSM_FLAT),
      bbar, apr, api, ccat, d_row, wglu_bf, bglu_row)
    return (y.reshape(db, t, SSM_W), hre.reshape(db, SSM_GROUPS, SSM_STATE),
            him.reshape(db, SSM_GROUPS, SSM_STATE))


MERGE_TM = 512


def _merge_kernel(x_ref, att_ref, ssm_ref, g_ref, wa_ref, wb_ref, wo_ref, lg_ref, lb_ref, wq_ref,
                  o_ref, qt_ref):
    ya = _dot(att_ref[...], wa_ref[...])
    yb = _dot(ssm_ref[...], wb_ref[...])
    merged = g_ref[:, :D_MODEL] * ya + g_ref[:, D_MODEL:] * yb
    y = DN_ALPHA * x_ref[...] + _dot(merged.astype(BF16), wo_ref[...])
    x1 = _layer_norm(y, lg_ref[...], lb_ref[...])
    o_ref[...] = x1
    qt_ref[...] = _dot_nt(wq_ref[...], x1.astype(BF16)).astype(BF16)


def _merge(x, att, ssm, gates, wa_bf, wb_bf, wo_bf, lg_row, lb_row, wq_bf):
    n = x.shape[0]
    tm = MERGE_TM
    row = lambda w: pl.BlockSpec((tm, w), lambda i: (i, 0))
    full = lambda a: pl.BlockSpec(a.shape, lambda i: (0,) * a.ndim)
    n_q = PEER_HEADS * PEER_DKEY
    return pl.pallas_call(
        _merge_kernel,
        grid=(n // tm,),
        in_specs=[row(D_MODEL), row(ATT_W), row(SSM_W), row(2 * D_MODEL),
                  full(wa_bf), full(wb_bf), full(wo_bf), full(lg_row), full(lb_row), full(wq_bf)],
        out_specs=[row(D_MODEL), pl.BlockSpec((n_q, tm), lambda i: (0, i))],
        out_shape=[jax.ShapeDtypeStruct((n, D_MODEL), F32), jax.ShapeDtypeStruct((n_q, n), BF16)],
        compiler_params=_cparams("parallel"),
        name="merge",
    )(x, att, ssm, gates, wa_bf, wb_bf, wo_bf, lg_row, lb_row, wq_bf)


PEER_TS = 512
PEER_EC = 1024
PEER_NCH = PEER_EXPERTS // PEER_EC
PEER_ROWS_PER_STEP = PEER_EC // PEER_NKEYS
HALF_KEY = PEER_DKEY // 2


def _sorting_network(n):
    pairs = []
    t = (n - 1).bit_length()
    p = 1 << (t - 1)
    while p > 0:
        q, r, d = 1 << (t - 1), 0, p
        while d > 0:
            pairs.extend((i, i + d) for i in range(n - d) if (i & p) == r)
            d, q, r = q - p, q >> 1, p
        p >>= 1
    return pairs


def _top_sorted(s, k):
    assert s.shape[0] == k * SUBLANES and k & (k - 1) == 0
    v = [s[i * SUBLANES:(i + 1) * SUBLANES, :] for i in range(k)]
    for i, j in _sorting_network(k):
        v[i], v[j] = jnp.maximum(v[i], v[j]), jnp.minimum(v[i], v[j])
    shift = SUBLANES // 2
    while shift:
        other = [pltpu.roll(x, shift, axis=0) for x in v]
        v = [jnp.maximum(v[i], other[k - 1 - i]) for i in range(k)]
        stride = k // 2
        while stride:
            for i in range(k):
                if (i // stride) % 2 == 0:
                    lo, hi = v[i], v[i + stride]
                    v[i], v[i + stride] = jnp.maximum(lo, hi), jnp.minimum(lo, hi)
            stride //= 2
        shift //= 2
    return v


def _peer_tables(s1, s2):
    k = PEER_TOPK
    c = s1.shape[1]
    groups3 = (k, SUBLANES, c)
    v1 = _top_sorted(s1, k)
    v2 = _top_sorted(s2, k)
    s2g = s2.reshape(groups3)
    rank2 = jnp.zeros(groups3, F32)
    for b in range(k):
        rank2 = jnp.where(s2g < v2[b][None], float(b + 1), rank2)
    sub = lax.broadcasted_iota(jnp.int32, (SUBLANES, c), 0)

    def pack(rows):
        out = rows[0]
        for r in range(1, SUBLANES):
            out = jnp.where(sub == r, rows[r], out)
        return out

    v2_lo, v2_hi, v1_hi = pack(v2[:SUBLANES]), pack(v2[SUBLANES:]), pack(v1[SUBLANES:])
    cands = [v1[0] + v2_lo, v1[0] + v2_hi, v1[1] + v2_lo]
    for a in range(2, SUBLANES):
        cands.append(jnp.where(sub < k // (a + 1), v1[a] + v2_lo, NEG_INF))
    cands.append(v1_hi + v2[0])
    cands += [jnp.full((SUBLANES, c), NEG_INF, F32)] * (k - len(cands))
    sc = _top_sorted(jnp.concatenate(cands, axis=0), k)
    tau = sc[k - 1]
    z = jnp.zeros_like(tau)
    for r in range(k):
        z = z + jnp.exp(sc[r] - sc[0])
    s1g = s1.reshape(groups3)
    count = jnp.zeros(groups3, F32)
    for a in range(k):
        n_sel = jnp.zeros_like(tau)
        for b in range(k // (a + 1)):
            n_sel = n_sel + jnp.where(v1[a] + v2[b] >= tau, 1.0, 0.0)
        count = jnp.where(s1g == v1[a][None], n_sel[None], count)
    row_weight = jnp.exp(s1g - v1[0][None]) / z[None]
    p2 = jnp.exp(s2g - v2[0][None])
    flat = lambda a: a.reshape(k * SUBLANES, c)
    return flat(rank2), flat(p2), flat(count), flat(row_weight)


def _peer_kernel(x_ref, qt_ref, k1_ref, k2_ref,
                 u0_ref, un_ref, vp_ref, vl_ref, lg_ref, lb_ref, o_ref,
                 xb_ref, s_ref, r2_ref, p2_ref, cnt_ref, cw_ref, acc_ref, st_ref, ht_ref, w_ref):
    c = pl.program_id(1)
    ts = x_ref.shape[0]
    groups = PEER_NKEYS // BF16_ROWS
    cur = c % 2
    nxt = 1 - cur

    @pl.when(c == 0)
    def _prologue():
        xb = x_ref[...].astype(BF16)
        xb_ref[...] = xb
        st_ref[0] = _dot_nt(u0_ref[...], xb).astype(BF16)
        for h in range(PEER_HEADS):
            base = h * PEER_DKEY
            s_ref[0] = _dot(k1_ref[h], qt_ref[base:base + HALF_KEY, :])
            s_ref[1] = _dot(k2_ref[h], qt_ref[base + HALF_KEY:base + PEER_DKEY, :])

            def lane_tile(lt, _):
                cols = pl.ds(pl.multiple_of(lt * LANES, LANES), LANES)
                rank2, p2, count, row_weight = _peer_tables(s_ref[0, :, cols], s_ref[1, :, cols])
                r2_ref[h, :, :, cols] = rank2.astype(BF16).reshape(groups, BF16_ROWS, LANES)
                p2_ref[h, :, :, cols] = p2.astype(BF16).reshape(groups, BF16_ROWS, LANES)
                cnt_ref[h, :, cols] = count
                cw_ref[h, :, cols] = row_weight
                return 0

            lax.fori_loop(0, ts // LANES, lane_tile, 0)
        acc_ref[...] = jnp.zeros_like(acc_ref)
        ht_ref[1] = jnp.zeros((PEER_EC, ts), BF16)

    i1_base = pl.multiple_of(c * PEER_ROWS_PER_STEP, PEER_ROWS_PER_STEP)
    cnts = [cnt_ref[h, pl.ds(i1_base, PEER_ROWS_PER_STEP), :] for h in range(PEER_HEADS)]
    cws = [cw_ref[h, pl.ds(i1_base, PEER_ROWS_PER_STEP), :] for h in range(PEER_HEADS)]
    mblk = 2 * PEER_NKEYS
    for j in range(PEER_ROWS_PER_STEP):
        rows = slice(j * PEER_NKEYS, (j + 1) * PEER_NKEYS)
        w = jnp.zeros((groups, BF16_ROWS, ts), BF16)
        for h in range(PEER_HEADS):
            cnt = jnp.broadcast_to(cnts[h][j:j + 1, :], (BF16_ROWS, ts)).astype(BF16)
            cw = jnp.broadcast_to(cws[h][j:j + 1, :], (BF16_ROWS, ts)).astype(BF16)
            w = w + jnp.where(r2_ref[h] < cnt[None], cw[None], 0.0) * p2_ref[h]
        w_ref[rows, :] = w.reshape(PEER_NKEYS, ts)
        if j % 2 == 1:
            er = slice((j // 2) * mblk, (j // 2 + 1) * mblk)
            st_ref[nxt, er, :] = _dot_nt(un_ref[er, :], xb_ref[...]).astype(BF16)
    acc_ref[...] += _dot(vp_ref[...], ht_ref[nxt])
    for j in range(PEER_ROWS_PER_STEP):
        rows = slice(j * PEER_NKEYS, (j + 1) * PEER_NKEYS)
        ht_ref[cur, rows, :] = w_ref[rows, :] * _gelu(st_ref[cur, rows, :])

    @pl.when(c == PEER_NCH - 1)
    def _epilogue():
        acc = acc_ref[...] + _dot(vl_ref[...], ht_ref[cur])
        y = DN_ALPHA * x_ref[...] + acc.T
        o_ref[...] = _layer_norm(y, lg_ref[...], lb_ref[...])


def _peer(x1, q_t, k1_bf, k2_bf, u_bf, vt_bf, lg_row, lb_row):
    n = x1.shape[0]
    ts = PEER_TS
    last = PEER_NCH - 1
    full = lambda a: pl.BlockSpec(a.shape, lambda i, c: (0,) * a.ndim)
    tok = pl.BlockSpec((ts, D_MODEL), lambda i, c: (i, 0))
    tables = pltpu.VMEM((PEER_HEADS, PEER_NKEYS, ts), F32)
    packed = pltpu.VMEM((PEER_HEADS, PEER_NKEYS // BF16_ROWS, BF16_ROWS, ts), BF16)
    return pl.pallas_call(
        _peer_kernel,
        grid=(n // ts, PEER_NCH),
        in_specs=[tok, pl.BlockSpec((q_t.shape[0], ts), lambda i, c: (0, i)), full(k1_bf), full(k2_bf),
                  pl.BlockSpec((PEER_EC, D_MODEL), lambda i, c: (0, 0)),
                  pl.BlockSpec((PEER_EC, D_MODEL), lambda i, c: (jnp.minimum(c + 1, last), 0)),
                  pl.BlockSpec((D_MODEL, PEER_EC), lambda i, c: (0, jnp.maximum(c - 1, 0))),
                  pl.BlockSpec((D_MODEL, PEER_EC), lambda i, c: (0, last)),
                  full(lg_row), full(lb_row)],
        out_specs=tok,
        out_shape=jax.ShapeDtypeStruct((n, D_MODEL), F32),
        scratch_shapes=[
            pltpu.VMEM((ts, D_MODEL), BF16),
            pltpu.VMEM((2, PEER_NKEYS, ts), F32),
            packed, packed,
            tables, tables,
            pltpu.VMEM((D_MODEL, ts), F32),
            pltpu.VMEM((2, PEER_EC, ts), BF16),
            pltpu.VMEM((2, PEER_EC, ts), BF16),
            pltpu.VMEM((PEER_EC, ts), BF16),
        ],
        compiler_params=_cparams("parallel", "arbitrary"),
        name="peer",
    )(x1, q_t, k1_bf, k2_bf, u_bf, u_bf, vt_bf, vt_bf, lg_row, lb_row)


def kernel(x_prompt, x_sample, cache_k, cache_v, state_ssm_re, state_ssm_im, page_table, w_in, b_in, w_a, w_b, w_o, ln1_g, ln1_b, a_re, a_im, log_dt, b_re, b_im, c_re, c_im, d_skip, w_glu, b_glu, ln2_g, ln2_b, w_pq, sub_k1, sub_k2, peer_u, peer_v):
    bn, t, _ = x_prompt.shape
    db, dt_, _ = x_sample.shape
    row = lambda a: a.reshape(1, -1).astype(F32)
    slopes = jnp.exp2(-8.0 * (jnp.arange(N_HEADS, dtype=F32) + 1.0) / N_HEADS)

    w_in_bf = w_in.astype(BF16)
    proj_w = (w_in_bf, row(b_in), w_in[:, ATT_W:3 * ATT_W].T.astype(BF16),
              b_in[ATT_W:3 * ATT_W].reshape(2 * ATT_W, 1).astype(F32))
    wa_bf, wb_bf, wo_bf, wglu_bf = (w.astype(BF16) for w in (w_a, w_b, w_o, w_glu))
    wq_bf = w_pq.T.astype(BF16)
    peer_w = (sub_k1.astype(BF16), sub_k2.astype(BF16))
    u_bf = peer_u.astype(BF16)
    vt_bf = peer_v.T.astype(BF16)

    bbar, ccat, apr, api = _s5_prep(a_re, a_im, log_dt, b_re, b_im, c_re, c_im)
    s5_w = (bbar, apr, api, ccat, row(d_skip), wglu_bf, row(b_glu))

    xp = x_prompt.reshape(bn * t, D_MODEL)
    q, k_t, v_t, u, gates = _proj(xp, *proj_w, seq_len=t)
    att = _moba_prompt(q.reshape(bn, t, ATT_W), k_t, v_t, slopes)
    ssm, hre_p, him_p = _s5_prompt(u.reshape(bn, t, SSM_W), *s5_w)
    x1, q_t = _merge(xp, att.reshape(bn * t, ATT_W), ssm.reshape(bn * t, SSM_W), gates,
                     wa_bf, wb_bf, wo_bf, row(ln1_g), row(ln1_b), wq_bf)
    y_prompt = _peer(x1, q_t, *peer_w, u_bf, vt_bf, row(ln2_g), row(ln2_b)).reshape(bn, t, D_MODEL)
    per_token = lambda a: a.reshape(bn, N_HEADS, HEAD_DIM, t).transpose(0, 3, 1, 2)
    k_prompt, v_prompt = per_token(k_t), per_token(v_t)

    xs = x_sample.reshape(db * dt_, D_MODEL)
    q, k, v, u, gates = _proj(xs, *proj_w)
    att = _moba_sample(q.reshape(db, dt_, ATT_W), k.reshape(db, dt_, ATT_W), v.reshape(db, dt_, ATT_W),
                       cache_k, cache_v, page_table, slopes)
    ssm, hre_s, him_s = _s5_sample(u.reshape(db, dt_, SSM_W), state_ssm_re, state_ssm_im, *s5_w)
    x1, q_t = _merge(xs, att.reshape(db * dt_, ATT_W), ssm.reshape(db * dt_, SSM_W), gates,
                     wa_bf, wb_bf, wo_bf, row(ln1_g), row(ln1_b), wq_bf)
    y_sample = _peer(x1, q_t, *peer_w, u_bf, vt_bf, row(ln2_g), row(ln2_b)).reshape(db, dt_, D_MODEL)
    k_sample = k.reshape(db, dt_, N_HEADS, HEAD_DIM)
    v_sample = v.reshape(db, dt_, N_HEADS, HEAD_DIM)

    return (y_prompt, y_sample, k_prompt, v_prompt, k_sample, v_sample,
            hre_p, him_p, hre_s, him_s)
```
